```python
import math
import jax, jax.numpy as jnp
from jax import lax
import numpy as np

D_MODEL = 1024
BATCH = 8
SEQ = 4096
DEPTH = 1

HG_HEADS = 4
HG_KDIM = 128
HG_VDIM = 128
HG_WIDTH = HG_HEADS * HG_KDIM
CHUNK = 32
LRU_WIDTH = 512
LRU_BLOCKS = 8
LRU_BLOCK_DIM = LRU_WIDTH // LRU_BLOCKS
CONV_WIDTH = 4
LRU_C = 8.0
D_MIX = HG_WIDTH + LRU_WIDTH
D_IN = 4 * HG_WIDTH + 2 * LRU_WIDTH
D_FF = -(-8 * D_MODEL // (3 * 256)) * 256
EPS = 1e-6

kernel_name = 'hybrid_hgrn2_rglru_parallel_heads'


def rms_norm(x, w):
    xf = x.astype(jnp.float32)
    y = xf * lax.rsqrt(jnp.mean(xf * xf, axis=-1, keepdims=True) + EPS)
    return (y * w.astype(jnp.float32)).astype(x.dtype)


def hgrn2_mix(q, f_logit, i, g, lb, norm_w):
    B, S, _ = q.shape
    out_dtype = q.dtype
    qf = jax.nn.silu(q.astype(jnp.float32))
    f = lb + (1.0 - lb) * jax.nn.sigmoid(f_logit.astype(jnp.float32))
    log_f = jnp.log(f)
    k = 1.0 - f
    v = i.astype(jnp.float32)
    n_chunks = S // CHUNK

    def to_chunks(t, d):
        return t.reshape(B, n_chunks, CHUNK, HG_HEADS, d).transpose(1, 0, 3, 2, 4)

    qc, kc, gc, vc = to_chunks(qf, HG_KDIM), to_chunks(k, HG_KDIM), to_chunks(log_f, HG_KDIM), to_chunks(v, HG_VDIM)
    causal = jnp.tril(jnp.ones((CHUNK, CHUNK), dtype=bool))

    def step(state, xs):
        q_, k_, lg, v_ = xs
        b = jnp.cumsum(lg, axis=-2)
        b_last = b[..., -1:, :]
        q_dec = q_ * jnp.exp(b)
        k_dec = k_ * jnp.exp(-b)
        scores = jnp.where(causal, jnp.einsum('bhtk,bhsk->bhts', q_dec, k_dec), 0.0)
        o = jnp.einsum('bhts,bhsv->bhtv', scores, v_) + jnp.einsum('bhtk,bhkv->bhtv', q_dec, state)
        k_state = k_ * jnp.exp(b_last - b)
        new_state = jnp.exp(b_last)[..., 0, :, None] * state + jnp.einsum('bhsk,bhsv->bhkv', k_state, v_)
        return new_state, o

    state0 = jnp.zeros((B, HG_HEADS, HG_KDIM, HG_VDIM), jnp.float32)
    _, o = lax.scan(step, state0, (qc, kc, gc, vc))
    o = o.transpose(1, 0, 3, 2, 4).reshape(B, S, HG_HEADS, HG_VDIM)
    o = o * lax.rsqrt(jnp.mean(o * o, axis=-1, keepdims=True) + EPS) * norm_w.astype(jnp.float32)
    o = o * jax.nn.silu(g.astype(jnp.float32).reshape(B, S, HG_HEADS, HG_VDIM))
    return o.reshape(B, S, HG_WIDTH).astype(out_dtype)


def rglru_mix(xb, gate_b, conv_w, conv_b, wa, ba, wx, bx, a_param):
    B, S, W = xb.shape
    out_dtype = xb.dtype
    x_pad = jnp.pad(xb, ((0, 0), (CONV_WIDTH - 1, 0), (0, 0)))
    xc = conv_b + sum(x_pad[:, tap:tap + S] * conv_w[tap] for tap in range(CONV_WIDTH))
    xc = xc.astype(jnp.float32)
    xblk = xc.reshape(B, S, LRU_BLOCKS, LRU_BLOCK_DIM)
    r = jax.nn.sigmoid(jnp.einsum('bsnd,nde->bsne', xblk, wa.astype(jnp.float32)).reshape(B, S, W) + ba)
    ig = jax.nn.sigmoid(jnp.einsum('bsnd,nde->bsne', xblk, wx.astype(jnp.float32)).reshape(B, S, W) + bx)
    log_a = -LRU_C * r * jax.nn.softplus(-a_param.astype(jnp.float32))
    a = jnp.exp(log_a)
    b_in = jnp.sqrt(-jnp.expm1(2.0 * log_a)) * (ig * xc)

    def combine(lhs, rhs):
        a1, b1 = lhs
        a2, b2 = rhs
        return a1 * a2, a2 * b1 + b2

    _, h = lax.associative_scan(combine, (a, b_in), axis=1)
    y = h * jax.nn.gelu(gate_b.astype(jnp.float32), approximate=True)
    return y.astype(out_dtype)


def setup_inputs(seed: int = 0) -> dict:
    key = jax.random.key(seed)
    ks = jax.random.split(key, 20)
    f32 = jnp.float32
    nrm = lambda k, shape, scale: (jax.random.normal(k, shape, f32) * scale)
    s_lru = jax.random.uniform(ks[12], (DEPTH, LRU_WIDTH), f32, 0.9, 0.999) ** (1.0 / LRU_C)
    return {
        'x': nrm(ks[0], (BATCH, SEQ, D_MODEL), 1.0),
        'mix_norm_w': 1.0 + nrm(ks[1], (DEPTH, D_MODEL), 0.02),
        'w_in': nrm(ks[2], (DEPTH, D_MODEL, D_IN), D_MODEL ** -0.5),
        'hg_lb': nrm(ks[3], (DEPTH + 1, HG_WIDTH), 0.5),
        'hg_norm_w': 1.0 + nrm(ks[4], (DEPTH, HG_VDIM), 0.02),
        'conv_w': nrm(ks[5], (DEPTH, CONV_WIDTH, LRU_WIDTH), CONV_WIDTH ** -0.5),
        'conv_b': nrm(ks[6], (DEPTH, LRU_WIDTH), 0.02),
        'lru_wa': nrm(ks[7], (DEPTH, LRU_BLOCKS, LRU_BLOCK_DIM, LRU_BLOCK_DIM), LRU_BLOCK_DIM ** -0.5),
        'lru_ba': nrm(ks[8], (DEPTH, LRU_WIDTH), 0.02),
        'lru_wx': nrm(ks[9], (DEPTH, LRU_BLOCKS, LRU_BLOCK_DIM, LRU_BLOCK_DIM), LRU_BLOCK_DIM ** -0.5),
        'lru_bx': nrm(ks[10], (DEPTH, LRU_WIDTH), 0.02),
        'lru_a': jnp.log(s_lru) - jnp.log1p(-s_lru),
        'w_out': nrm(ks[11], (DEPTH, D_MIX, D_MODEL), D_MIX ** -0.5),
        'ffn_norm_w': 1.0 + nrm(ks[13], (DEPTH, D_MODEL), 0.02),
        'w_gate_up': nrm(ks[14], (DEPTH, D_MODEL, 2 * D_FF), D_MODEL ** -0.5),
        'w_down': nrm(ks[15], (DEPTH, D_FF, D_MODEL), D_FF ** -0.5),
        'final_norm_w': 1.0 + nrm(ks[16], (D_MODEL,), 0.02),
    }


def reference(x, mix_norm_w, w_in, hg_lb, hg_norm_w, conv_w, conv_b, lru_wa, lru_ba,
              lru_wx, lru_bx, lru_a, w_out, ffn_norm_w, w_gate_up, w_down, final_norm_w):
    lb_all = jnp.cumsum(jax.nn.softmax(hg_lb.astype(jnp.float32), axis=0), axis=0)
    h = x
    for l in range(DEPTH):
        xn = rms_norm(h, mix_norm_w[l])
        proj = jnp.einsum('bsd,de->bse', xn, w_in[l])
        q, f_logit, i_v, g, lru_x, lru_gate = jnp.split(
            proj, [HG_WIDTH, 2 * HG_WIDTH, 3 * HG_WIDTH, 4 * HG_WIDTH, 4 * HG_WIDTH + LRU_WIDTH], axis=-1)
        o_hg = hgrn2_mix(q, f_logit, i_v, g, lb_all[l], hg_norm_w[l])
        o_lru = rglru_mix(lru_x, lru_gate, conv_w[l], conv_b[l], lru_wa[l], lru_ba[l],
                          lru_wx[l], lru_bx[l], lru_a[l])
        mixed = jnp.concatenate([o_hg, o_lru], axis=-1)
        h = h + jnp.einsum('bse,ed->bsd', mixed, w_out[l])
        hn = rms_norm(h, ffn_norm_w[l])
        gate, up = jnp.split(jnp.einsum('bsd,df->bsf', hn, w_gate_up[l]), 2, axis=-1)
        h = h + jnp.einsum('bsf,fd->bsd', jax.nn.silu(gate) * up, w_down[l])
    return rms_norm(h, final_norm_w)
```

```python
import functools

import jax
import jax.numpy as jnp
from jax import lax
from jax.experimental import pallas as pl
from jax.experimental.pallas import tpu as pltpu

D_MODEL = 1024
HG_HEADS = 4
HG_DIM = 128
HG_WIDTH = HG_HEADS * HG_DIM
HG_BASE = 32
LRU_WIDTH = 512
LRU_BLOCKS = 8
LRU_GROUP = 256
CONV_WIDTH = 4
LRU_C = 8.0
D_MIX = HG_WIDTH + LRU_WIDTH
D_IN = 4 * HG_WIDTH + 2 * LRU_WIDTH
EPS = 1e-6

SUB = 128
SEQ_TILE = 512
FFN_TILE = 512
FF_CHUNK = 256
TAIL = 8
VMEM_LIMIT = 56 * 1024 * 1024

_NT = (((1,), (1,)), ((), ()))
_TN = (((0,), (0,)), ((), ()))


def _dot(a, b, dims=None):
    if dims is None:
        return jnp.dot(a, b, preferred_element_type=jnp.float32)
    return lax.dot_general(a, b, dims, preferred_element_type=jnp.float32)


def _rms(x, w):
    return x * lax.rsqrt(jnp.mean(x * x, axis=-1, keepdims=True) + EPS) * w


def _rows(shape):
    return lax.broadcasted_iota(jnp.int32, shape, 0)


def _cumsum_rows(x):
    n = x.shape[0]
    row = _rows(x.shape)
    s = 1
    while s < n:
        x = x + jnp.where(row >= s, pltpu.roll(x, s, axis=0), 0.0)
        s *= 2
    return x


def _linear_scan_rows(a, b):
    n = a.shape[0]
    row = _rows(a.shape)
    s = 1
    while s < n:
        keep = row >= s
        a_prev = jnp.where(keep, pltpu.roll(a, s, axis=0), 1.0)
        b_prev = jnp.where(keep, pltpu.roll(b, s, axis=0), 0.0)
        b = a * b_prev + b
        a = a * a_prev
        s *= 2
    return a, b


def _bcast_rows(r, n):
    return jnp.broadcast_to(r, (n, r.shape[1]))


def _mixer_kernel(x_ref, nw_ref, win_ref, lb_ref, hgw_ref, cw_ref, cb_ref, wg_ref, bg_ref,
                  ap_ref, wout_ref, o_ref, proj_ref, xl_ref, mixed_ref, state_ref, hcar_ref):
    si = pl.program_id(1)
    ts = x_ref.shape[1]

    @pl.when(si == 0)
    def _():
        state_ref[...] = jnp.zeros_like(state_ref)
        hcar_ref[...] = jnp.zeros_like(hcar_ref)
        xl_ref[0:TAIL, :] = jnp.zeros((TAIL, LRU_WIDTH), jnp.float32)

    @pl.when(si != 0)
    def _():
        xl_ref[0:TAIL, :] = xl_ref[ts:ts + TAIL, :]

    x = x_ref[0]
    xn = _rms(x, nw_ref[...]).astype(jnp.bfloat16)
    n_hg = 4 * HG_WIDTH
    for c in range(0, n_hg, 512):
        proj_ref[:, c:c + 512] = _dot(xn, win_ref[:, c:c + 512])
    xl_ref[TAIL:TAIL + ts, :] = _dot(xn, win_ref[:, n_hg:n_hg + LRU_WIDTH])
    proj_ref[:, n_hg:n_hg + LRU_WIDTH] = _dot(xn, win_ref[:, n_hg + LRU_WIDTH:])

    lbp = lb_ref[...]
    lbe = jnp.exp(lbp - jnp.max(lbp, axis=0, keepdims=True))
    lb = lbe[0:1] / jnp.sum(lbe, axis=0, keepdims=True)
    neg_c_sp = -LRU_C * jax.nn.softplus(-ap_ref[...])
    hgw = hgw_ref[...]
    cw = cw_ref[...]
    cb = cb_ref[...]
    bg = bg_ref[...]

    row = _rows((SUB, SUB))
    col = lax.broadcasted_iota(jnp.int32, (SUB, SUB), 1)
    mask0 = (row // HG_BASE == col // HG_BASE) & (col <= row)
    mask1 = row // (2 * HG_BASE) == col // (2 * HG_BASE)
    rowk = _rows((SUB, HG_WIDTH))
    upper1 = (rowk % (2 * HG_BASE)) >= HG_BASE
    upper2 = rowk >= 2 * HG_BASE

    for j in range(ts // SUB):
        r0 = j * SUB
        q = proj_ref[r0:r0 + SUB, 0:HG_WIDTH]
        fl = proj_ref[r0:r0 + SUB, HG_WIDTH:2 * HG_WIDTH]
        v = proj_ref[r0:r0 + SUB, 2 * HG_WIDTH:3 * HG_WIDTH].astype(jnp.bfloat16)
        g = proj_ref[r0:r0 + SUB, 3 * HG_WIDTH:4 * HG_WIDTH]
        q = q * jax.nn.sigmoid(q)
        f = lb + (1.0 - lb) * jax.nn.sigmoid(fl)
        k = 1.0 - f
        beta = _cumsum_rows(jnp.log(f))
        r31 = beta[HG_BASE - 1:HG_BASE]
        r63 = beta[2 * HG_BASE - 1:2 * HG_BASE]
        r95 = beta[3 * HG_BASE - 1:3 * HG_BASE]
        r127 = beta[4 * HG_BASE - 1:4 * HG_BASE]
        ref0 = jnp.concatenate([jnp.zeros((HG_BASE, HG_WIDTH), jnp.float32), _bcast_rows(r31, HG_BASE),
                                _bcast_rows(r63, HG_BASE), _bcast_rows(r95, HG_BASE)], axis=0)
        ref1 = jnp.concatenate([_bcast_rows(r31, 2 * HG_BASE), _bcast_rows(r95, 2 * HG_BASE)], axis=0)
        b0 = beta - ref0
        e1 = beta - ref1
        e2 = beta - r63
        dec1 = jnp.exp(jnp.where(upper1, e1, -e1))
        dec2 = jnp.exp(jnp.where(upper2, e2, -e2))
        q0 = (q * jnp.exp(b0)).astype(jnp.bfloat16)
        k0 = (k * jnp.exp(-b0)).astype(jnp.bfloat16)
        q1 = jnp.where(upper1, q * dec1, 0.0).astype(jnp.bfloat16)
        k1 = jnp.where(upper1, 0.0, k * dec1).astype(jnp.bfloat16)
        q2 = jnp.where(upper2, q * dec2, 0.0).astype(jnp.bfloat16)
        k2 = jnp.where(upper2, 0.0, k * dec2).astype(jnp.bfloat16)
        qs = (q * jnp.exp(beta)).astype(jnp.bfloat16)
        ks = (k * jnp.exp(r127 - beta)).astype(jnp.bfloat16)
        dec_end = jnp.exp(r127)
        gate = g * jax.nn.sigmoid(g)
        for h in range(HG_HEADS):
            hs = slice(h * HG_DIM, (h + 1) * HG_DIM)
            sc = jnp.where(mask0, _dot(q0[:, hs], k0[:, hs], _NT), 0.0)
            sc = sc + jnp.where(mask1, _dot(q1[:, hs], k1[:, hs], _NT), 0.0)
            sc = sc + _dot(q2[:, hs], k2[:, hs], _NT)
            st = state_ref[h]
            o = _dot(sc.astype(jnp.bfloat16), v[:, hs]) + _dot(qs[:, hs], st.astype(jnp.bfloat16), _NT)
            state_ref[h] = st * dec_end[:, hs] + _dot(v[:, hs], ks[:, hs], _TN)
            o = o * lax.rsqrt(jnp.mean(o * o, axis=-1, keepdims=True) + EPS) * hgw
            mixed_ref[r0:r0 + SUB, hs] = (o * gate[:, hs]).astype(jnp.bfloat16)

        base = TAIL + r0
        xc = cb + xl_ref[base:base + SUB, :] * cw[CONV_WIDTH - 1:CONV_WIDTH]
        for tap in range(CONV_WIDTH - 1):
            sh = CONV_WIDTH - 1 - tap
            xc = xc + xl_ref[base - sh:base - sh + SUB, :] * cw[tap:tap + 1]
        xcb = xc.astype(jnp.bfloat16)
        gates = jnp.concatenate(
            [_dot(xcb[:, gi * LRU_GROUP:(gi + 1) * LRU_GROUP], wg_ref[gi]) for gi in range(LRU_WIDTH // LRU_GROUP)],
            axis=1)
        rr = jnp.concatenate([gates[:, 0:256], gates[:, 512:768]], axis=1)
        ii = jnp.concatenate([gates[:, 256:512], gates[:, 768:1024]], axis=1)
        rr = jax.nn.sigmoid(rr + bg[0:1])
        ii = jax.nn.sigmoid(ii + bg[1:2])
        log_a = rr * neg_c_sp
        a = jnp.exp(log_a)
        b_in = jnp.sqrt(1.0 - a * a) * (ii * xc)
        a_cum, hh = _linear_scan_rows(a, b_in)
        hh = hh + a_cum * hcar_ref[...]
        hcar_ref[...] = hh[SUB - 1:SUB]
        gl = proj_ref[r0:r0 + SUB, 4 * HG_WIDTH:4 * HG_WIDTH + LRU_WIDTH]
        mixed_ref[r0:r0 + SUB, HG_WIDTH:] = (hh * jax.nn.gelu(gl, approximate=True)).astype(jnp.bfloat16)

    o_ref[0] = x_ref[0] + _dot(mixed_ref[...], wout_ref[...])


def _ffn_kernel(h_ref, nw_ref, wgu_ref, wd_ref, fw_ref, o_ref, acc_ref):
    h = h_ref[...]
    hn = _rms(h, nw_ref[...]).astype(jnp.bfloat16)
    acc_ref[...] = h

    def body(c, carry):
        gu = _dot(hn, wgu_ref[c])
        gt = gu[:, :FF_CHUNK]
        act = (gt * jax.nn.sigmoid(gt) * gu[:, FF_CHUNK:]).astype(jnp.bfloat16)
        acc_ref[...] += _dot(act, wd_ref[c])
        return carry

    lax.fori_loop(0, wgu_ref.shape[0], body, 0)
    o_ref[...] = _rms(acc_ref[...], fw_ref[...])


def _block_diag_groups(w):
    per = LRU_GROUP // (LRU_WIDTH // LRU_BLOCKS)
    bd = LRU_WIDTH // LRU_BLOCKS
    w = w.reshape(LRU_WIDTH // LRU_GROUP, per, bd, bd)
    eye = jnp.eye(per, dtype=w.dtype)
    return jnp.einsum('gade,ab->gadbe', w, eye).reshape(LRU_WIDTH // LRU_GROUP, LRU_GROUP, LRU_GROUP)


def kernel(x, mix_norm_w, w_in, hg_lb, hg_norm_w, conv_w, conv_b, lru_wa, lru_ba, lru_wx, lru_bx,
           lru_a, w_out, ffn_norm_w, w_gate_up, w_down, final_norm_w):
    B, S, D = x.shape
    assert D == D_MODEL and w_in.shape == (1, D_MODEL, D_IN) and S % SEQ_TILE == 0
    bf = jnp.bfloat16
    d_ff = w_down.shape[1]
    assert d_ff % FF_CHUNK == 0
    n_ff = d_ff // FF_CHUNK

    win = w_in[0].astype(bf)
    wout = w_out[0].astype(bf)
    wg = jnp.concatenate([_block_diag_groups(lru_wa[0]), _block_diag_groups(lru_wx[0])], axis=-1).astype(bf)
    bg = jnp.stack([lru_ba[0], lru_bx[0]], axis=0)
    row = lambda a: a.reshape(1, -1)

    full = lambda shape: pl.BlockSpec(shape, lambda b, s: (0,) * len(shape))
    h = pl.pallas_call(
        _mixer_kernel,
        grid=(B, S // SEQ_TILE),
        in_specs=[
            pl.BlockSpec((1, SEQ_TILE, D), lambda b, s: (b, s, 0)),
            full((1, D)), full((D, D_IN)), full((2, HG_WIDTH)), full((1, HG_DIM)),
            full((CONV_WIDTH, LRU_WIDTH)), full((1, LRU_WIDTH)),
            full((LRU_WIDTH // LRU_GROUP, LRU_GROUP, 2 * LRU_GROUP)), full((2, LRU_WIDTH)),
            full((1, LRU_WIDTH)), full((D_MIX, D)),
        ],
        out_specs=pl.BlockSpec((1, SEQ_TILE, D), lambda b, s: (b, s, 0)),
        out_shape=jax.ShapeDtypeStruct((B, S, D), jnp.float32),
        scratch_shapes=[
            pltpu.VMEM((SEQ_TILE, 4 * HG_WIDTH + LRU_WIDTH), jnp.float32),
            pltpu.VMEM((TAIL + SEQ_TILE, LRU_WIDTH), jnp.float32),
            pltpu.VMEM((SEQ_TILE, D_MIX), jnp.bfloat16),
            pltpu.VMEM((HG_HEADS, HG_DIM, HG_DIM), jnp.float32),
            pltpu.VMEM((1, LRU_WIDTH), jnp.float32),
        ],
        compiler_params=pltpu.CompilerParams(
            dimension_semantics=("arbitrary", "arbitrary"), vmem_limit_bytes=VMEM_LIMIT),
        name="mixer",
    )(x, row(mix_norm_w[0]), win, hg_lb, row(hg_norm_w[0]), conv_w[0], row(conv_b[0]), wg, bg,
      row(lru_a[0]), wout)

    wgu = w_gate_up[0].astype(bf)
    wgu = jnp.concatenate([wgu[:, :d_ff].reshape(D, n_ff, FF_CHUNK), wgu[:, d_ff:].reshape(D, n_ff, FF_CHUNK)],
                          axis=-1).transpose(1, 0, 2)
    wd = w_down[0].astype(bf).reshape(n_ff, FF_CHUNK, D)
    T = B * S
    full1 = lambda shape: pl.BlockSpec(shape, lambda i: (0,) * len(shape))
    out = pl.pallas_call(
        _ffn_kernel,
        grid=(T // FFN_TILE,),
        in_specs=[
            pl.BlockSpec((FFN_TILE, D), lambda i: (i, 0)),
            full1((1, D)), full1((n_ff, D, 2 * FF_CHUNK)), full1((n_ff, FF_CHUNK, D)), full1((1, D)),
        ],
        out_specs=pl.BlockSpec((FFN_TILE, D), lambda i: (i, 0)),
        out_shape=jax.ShapeDtypeStruct((T, D), jnp.float32),
        scratch_shapes=[pltpu.VMEM((FFN_TILE, D), jnp.float32)],
        compiler_params=pltpu.CompilerParams(
            dimension_semantics=("arbitrary",), vmem_limit_bytes=VMEM_LIMIT),
        name="ffn",
    )(h.reshape(T, D), row(ffn_norm_w[0]), wgu, wd, row(final_norm_w))
    return out.reshape(B, S, D)
```

```python
import functools

import jax
import jax.numpy as jnp
from jax import lax
from jax.experimental import pallas as pl
from jax.experimental.pallas import tpu as pltpu

D_MODEL = 1024
HG_HEADS = 4
HG_DIM = 128
HG_WIDTH = HG_HEADS * HG_DIM
HG_BASE = 32
LRU_WIDTH = 512
LRU_BLOCKS = 8
LRU_GROUP = 256
CONV_WIDTH = 4
LRU_C = 8.0
D_MIX = HG_WIDTH + LRU_WIDTH
D_IN = 4 * HG_WIDTH + 2 * LRU_WIDTH
EPS = 1e-6
LOG2E = 1.4426950408889634

SUB = 128
PROJ_ROWS = 512
SEQ_TILE = 1024
FFN_TILE = 512
FF_CHUNK = 256
ROW_TILE = 8
TAIL = ROW_TILE
VMEM_LIMIT = 56 * 1024 * 1024

_NT = (((1,), (1,)), ((), ()))
_TN = (((0,), (0,)), ((), ()))


def _dot(a, b, dims=None):
    if dims is None:
        return jnp.dot(a, b, preferred_element_type=jnp.float32)
    return lax.dot_general(a, b, dims, preferred_element_type=jnp.float32)


def _rms(x, w):
    return x * lax.rsqrt(jnp.mean(x * x, axis=-1, keepdims=True) + EPS) * w


def _rows(shape):
    return lax.broadcasted_iota(jnp.int32, shape, 0)


def _tile_shift(x, s):
    n, w = x.shape
    return pltpu.roll(x.reshape(n // ROW_TILE, ROW_TILE, w), s, axis=1).reshape(n, w)


def _cumsum_rows(x):
    n = x.shape[0]
    sub = _rows(x.shape) % ROW_TILE
    s = 1
    while s < ROW_TILE:
        x = x + jnp.where(sub >= s, _tile_shift(x, s), 0.0)
        s *= 2
    outs = [x[0:ROW_TILE]]
    for t in range(1, n // ROW_TILE):
        outs.append(x[t * ROW_TILE:(t + 1) * ROW_TILE] + outs[-1][ROW_TILE - 1:ROW_TILE])
    return jnp.concatenate(outs, axis=0)


def _linear_scan_rows(a, b, h0):
    n = a.shape[0]
    sub = _rows(a.shape) % ROW_TILE
    s = 1
    while s < ROW_TILE:
        keep = sub >= s
        a_prev = jnp.where(keep, _tile_shift(a, s), 1.0)
        b_prev = jnp.where(keep, _tile_shift(b, s), 0.0)
        b = a * b_prev + b
        a = a * a_prev
        s *= 2
    outs = []
    h = h0
    for t in range(n // ROW_TILE):
        rows = slice(t * ROW_TILE, (t + 1) * ROW_TILE)
        outs.append(a[rows] * h + b[rows])
        h = outs[-1][ROW_TILE - 1:ROW_TILE]
    return jnp.concatenate(outs, axis=0), h


def _bcast_rows(r, n):
    return jnp.broadcast_to(r, (n, r.shape[1]))


def _mixer_kernel(x_ref, nw_ref, win_ref, lb_ref, hgw_ref, cw_ref, cb_ref, wg_ref, bg_ref,
                  ap_ref, wout_ref, o_ref, proj0_ref, proj1_ref, xl0_ref, xl1_ref, mixed_ref, state_ref,
                  hcar_ref):
    si = pl.program_id(1)
    ts = x_ref.shape[1]
    n_blk = ts // PROJ_ROWS
    proj_refs = (proj0_ref, proj1_ref)
    xl_refs = (xl0_ref, xl1_ref)

    @pl.when(si == 0)
    def _():
        state_ref[...] = jnp.zeros_like(state_ref)
        hcar_ref[...] = jnp.zeros_like(hcar_ref)
        xl0_ref[0:TAIL, :] = jnp.zeros((TAIL, LRU_WIDTH), jnp.float32)

    @pl.when(si != 0)
    def _():
        xl0_ref[0:TAIL, :] = xl_refs[(n_blk - 1) % 2][PROJ_ROWS:PROJ_ROWS + TAIL, :]

    nw = nw_ref[...]
    n_hg = 4 * HG_WIDTH

    lbp = lb_ref[...]
    lbe = jnp.exp(lbp - jnp.max(lbp, axis=0, keepdims=True))
    lb = lbe[0:1] / jnp.sum(lbe, axis=0, keepdims=True)
    neg_c_sp2 = (-LRU_C * LOG2E) * jax.nn.softplus(-ap_ref[...])
    hgw = hgw_ref[...]
    cw = cw_ref[...]
    cb = cb_ref[...]
    bg = bg_ref[...]

    row = _rows((SUB, SUB))
    col = lax.broadcasted_iota(jnp.int32, (SUB, SUB), 1)
    mask0 = (row // HG_BASE == col // HG_BASE) & (col <= row)
    mask1 = ((row // (2 * HG_BASE) == col // (2 * HG_BASE))
             & (row % (2 * HG_BASE) >= HG_BASE) & (col % (2 * HG_BASE) < HG_BASE))

    def in_proj(blk):
        b0 = blk * PROJ_ROWS
        proj_ref, xl_ref, xl_prev = proj_refs[blk % 2], xl_refs[blk % 2], xl_refs[(blk - 1) % 2]
        cell = []

        def xn():
            if not cell:
                cell.append(_rms(x_ref[0, b0:b0 + PROJ_ROWS, :], nw).astype(jnp.bfloat16))
            return cell[0]

        def hg_chunk(c):
            proj_ref[:, c:c + 512] = _dot(xn(), win_ref[:, c:c + 512])

        def lru_x():
            if blk > 0:
                xl_ref[0:TAIL, :] = xl_prev[PROJ_ROWS:PROJ_ROWS + TAIL, :]
            xl_ref[TAIL:TAIL + PROJ_ROWS, :] = _dot(xn(), win_ref[:, n_hg:n_hg + LRU_WIDTH])

        def lru_gate():
            proj_ref[:, n_hg:n_hg + LRU_WIDTH] = _dot(xn(), win_ref[:, n_hg + LRU_WIDTH:])

        return [functools.partial(hg_chunk, c) for c in range(0, n_hg, 512)] + [lru_x, lru_gate]

    def out_proj(blk):
        b0 = blk * PROJ_ROWS

        def chunk(c):
            o_ref[0, b0:b0 + PROJ_ROWS, c:c + 512] = (x_ref[0, b0:b0 + PROJ_ROWS, c:c + 512]
                                                      + _dot(mixed_ref[b0:b0 + PROJ_ROWS, :], wout_ref[:, c:c + 512]))

        return [functools.partial(chunk, c) for c in range(0, D_MODEL, 512)]

    def sub_tile(blk, j):
        proj_ref, xl_ref = proj_refs[blk % 2], xl_refs[blk % 2]
        p0 = j * SUB
        r0 = blk * PROJ_ROWS + p0
        q = proj_ref[p0:p0 + SUB,0:HG_WIDTH]
        fl = proj_ref[p0:p0 + SUB,HG_WIDTH:2 * HG_WIDTH]
        v = proj_ref[p0:p0 + SUB,2 * HG_WIDTH:3 * HG_WIDTH].astype(jnp.bfloat16)
        g = proj_ref[p0:p0 + SUB,3 * HG_WIDTH:4 * HG_WIDTH]
        q = q * jax.nn.sigmoid(q)
        f = lb + (1.0 - lb) * jax.nn.sigmoid(fl)
        k = 1.0 - f
        beta = _cumsum_rows(jnp.log(f) * LOG2E)
        r31 = beta[HG_BASE - 1:HG_BASE]
        r63 = beta[2 * HG_BASE - 1:2 * HG_BASE]
        r95 = beta[3 * HG_BASE - 1:3 * HG_BASE]
        r127 = beta[4 * HG_BASE - 1:4 * HG_BASE]
        ref0 = jnp.concatenate([jnp.zeros((HG_BASE, HG_WIDTH), jnp.float32), _bcast_rows(r31, HG_BASE),
                                _bcast_rows(r63, HG_BASE), _bcast_rows(r95, HG_BASE)], axis=0)
        ref1 = jnp.concatenate([_bcast_rows(r31, 2 * HG_BASE), _bcast_rows(r95, 2 * HG_BASE)], axis=0)
        b0 = beta - ref0
        q0 = (q * jnp.exp2(b0)).astype(jnp.bfloat16)
        k0 = (k * jnp.exp2(-b0)).astype(jnp.bfloat16)
        k1 = (k * jnp.exp2(jnp.minimum(ref1 - beta, 0.0))).astype(jnp.bfloat16)
        half = SUB // 2
        q2 = (q[half:] * jnp.exp2(beta[half:] - r63)).astype(jnp.bfloat16)
        k2 = jnp.concatenate([(k[:half] * jnp.exp2(r63 - beta[:half])).astype(jnp.bfloat16),
                              jnp.zeros((half, HG_WIDTH), jnp.bfloat16)], axis=0)
        qs = (q * jnp.exp2(beta)).astype(jnp.bfloat16)
        ks = (k * jnp.exp2(r127 - beta)).astype(jnp.bfloat16)
        dec_end = jnp.exp2(r127)
        gate = g * jax.nn.sigmoid(g)
        for h in range(HG_HEADS):
            hs = slice(h * HG_DIM, (h + 1) * HG_DIM)
            s01 = _dot(q0[:, hs], jnp.concatenate([k0[:, hs], k1[:, hs]], axis=0), _NT)
            sc = jnp.where(mask0, s01[:, :SUB], 0.0) + jnp.where(mask1, s01[:, SUB:], 0.0)
            s2 = _dot(q2[:, hs], k2[:, hs], _NT)
            sc = jnp.concatenate([sc[:half], sc[half:] + s2], axis=0)
            st = state_ref[h]
            vt = v[:, hs].T
            o = _dot(jnp.concatenate([sc.astype(jnp.bfloat16), qs[:, hs]], axis=1),
                     jnp.concatenate([vt, st.astype(jnp.bfloat16)], axis=1), _NT)
            state_ref[h] = st * dec_end[:, hs] + _dot(vt, ks[:, hs])
            o = o * lax.rsqrt(jnp.mean(o * o, axis=-1, keepdims=True) + EPS) * hgw
            mixed_ref[r0:r0 + SUB, hs] = (o * gate[:, hs]).astype(jnp.bfloat16)

        base = TAIL + p0
        xc = cb + xl_ref[base:base + SUB, :] * cw[CONV_WIDTH - 1:CONV_WIDTH]
        for tap in range(CONV_WIDTH - 1):
            sh = CONV_WIDTH - 1 - tap
            xc = xc + xl_ref[base - sh:base - sh + SUB, :] * cw[tap:tap + 1]
        xcb = xc.astype(jnp.bfloat16)
        gates = jnp.concatenate(
            [_dot(xcb[:, gi * LRU_GROUP:(gi + 1) * LRU_GROUP], wg_ref[gi]) for gi in range(LRU_WIDTH // LRU_GROUP)],
            axis=1)
        rr = jnp.concatenate([gates[:, 0:256], gates[:, 512:768]], axis=1)
        ii = jnp.concatenate([gates[:, 256:512], gates[:, 768:1024]], axis=1)
        rr = jax.nn.sigmoid(rr + bg[0:1])
        ii = jax.nn.sigmoid(ii + bg[1:2])
        a = jnp.exp2(rr * neg_c_sp2)
        b_in = jnp.sqrt(1.0 - a * a) * (ii * xc)
        hh, hcar_ref[...] = _linear_scan_rows(a, b_in, hcar_ref[...])
        gl = proj_ref[p0:p0 + SUB,n_hg:n_hg + LRU_WIDTH]
        mixed_ref[r0:r0 + SUB, HG_WIDTH:] = (hh * jax.nn.gelu(gl, approximate=True)).astype(jnp.bfloat16)

    for step in in_proj(0):
        step()
    n_sub = PROJ_ROWS // SUB
    for blk in range(n_blk):
        fill = (in_proj(blk + 1) if blk + 1 < n_blk else []) + (out_proj(blk - 1) if blk > 0 else [])
        for j in range(n_sub):
            for step in fill[j * len(fill) // n_sub:(j + 1) * len(fill) // n_sub]:
                step()
            sub_tile(blk, j)
    for step in out_proj(n_blk - 1):
        step()


def _ffn_kernel(h_ref, nw_ref, wgu_ref, wd_ref, fw_ref, o_ref, acc_ref):
    d_ff = wd_ref.shape[0]
    n_ff = d_ff // FF_CHUNK
    h = h_ref[...]
    hn = _rms(h, nw_ref[...]).astype(jnp.bfloat16)

    def act_of(c):
        cols = slice(c * FF_CHUNK, (c + 1) * FF_CHUNK)
        gt = _dot(hn, wgu_ref[:, cols])
        up = _dot(hn, wgu_ref[:, d_ff + c * FF_CHUNK:d_ff + (c + 1) * FF_CHUNK])
        return (gt * jax.nn.sigmoid(gt) * up).astype(jnp.bfloat16)

    act = act_of(0)
    for c in range(n_ff):
        nxt = act_of(c + 1) if c + 1 < n_ff else None
        down = _dot(act, wd_ref[c * FF_CHUNK:(c + 1) * FF_CHUNK, :])
        if c == 0:
            acc_ref[...] = h + down
        else:
            acc_ref[...] += down
        act = nxt
    o_ref[...] = _rms(acc_ref[...], fw_ref[...])


def _block_diag_groups(w):
    per = LRU_GROUP // (LRU_WIDTH // LRU_BLOCKS)
    bd = LRU_WIDTH // LRU_BLOCKS
    w = w.reshape(LRU_WIDTH // LRU_GROUP, per, bd, bd)
    eye = jnp.eye(per, dtype=w.dtype)
    return jnp.einsum('gade,ab->gadbe', w, eye).reshape(LRU_WIDTH // LRU_GROUP, LRU_GROUP, LRU_GROUP)


def kernel(x, mix_norm_w, w_in, hg_lb, hg_norm_w, conv_w, conv_b, lru_wa, lru_ba, lru_wx, lru_bx,
           lru_a, w_out, ffn_norm_w, w_gate_up, w_down, final_norm_w):
    B, S, D = x.shape
    assert D == D_MODEL and w_in.shape == (1, D_MODEL, D_IN) and S % SEQ_TILE == 0
    bf = jnp.bfloat16
    d_ff = w_down.shape[1]
    assert d_ff % FF_CHUNK == 0
    n_ff = d_ff // FF_CHUNK

    win = w_in[0].astype(bf)
    wout = w_out[0].astype(bf)
    wg = jnp.concatenate([_block_diag_groups(lru_wa[0]), _block_diag_groups(lru_wx[0])], axis=-1).astype(bf)
    bg = jnp.stack([lru_ba[0], lru_bx[0]], axis=0)
    row = lambda a: a.reshape(1, -1)

    full = lambda shape: pl.BlockSpec(shape, lambda b, s: (0,) * len(shape))
    h = pl.pallas_call(
        _mixer_kernel,
        grid=(B, S // SEQ_TILE),
        in_specs=[
            pl.BlockSpec((1, SEQ_TILE, D), lambda b, s: (b, s, 0)),
            full((1, D)), full((D, D_IN)), full((2, HG_WIDTH)), full((1, HG_DIM)),
            full((CONV_WIDTH, LRU_WIDTH)), full((1, LRU_WIDTH)),
            full((LRU_WIDTH // LRU_GROUP, LRU_GROUP, 2 * LRU_GROUP)), full((2, LRU_WIDTH)),
            full((1, LRU_WIDTH)), full((D_MIX, D)),
        ],
        out_specs=pl.BlockSpec((1, SEQ_TILE, D), lambda b, s: (b, s, 0)),
        out_shape=jax.ShapeDtypeStruct((B, S, D), jnp.float32),
        scratch_shapes=[
            pltpu.VMEM((PROJ_ROWS, 4 * HG_WIDTH + LRU_WIDTH), jnp.float32),
            pltpu.VMEM((PROJ_ROWS, 4 * HG_WIDTH + LRU_WIDTH), jnp.float32),
            pltpu.VMEM((TAIL + PROJ_ROWS, LRU_WIDTH), jnp.float32),
            pltpu.VMEM((TAIL + PROJ_ROWS, LRU_WIDTH), jnp.float32),
            pltpu.VMEM((SEQ_TILE, D_MIX), jnp.bfloat16),
            pltpu.VMEM((HG_HEADS, HG_DIM, HG_DIM), jnp.float32),
            pltpu.VMEM((1, LRU_WIDTH), jnp.float32),
        ],
        compiler_params=pltpu.CompilerParams(
            dimension_semantics=("arbitrary", "arbitrary"), vmem_limit_bytes=VMEM_LIMIT),
        name="mixer",
    )(x, row(mix_norm_w[0]), win, hg_lb, row(hg_norm_w[0]), conv_w[0], row(conv_b[0]), wg, bg,
      row(lru_a[0]), wout)

    wgu = w_gate_up[0].astype(bf)
    wd = w_down[0].astype(bf)
    T = B * S
    full1 = lambda shape: pl.BlockSpec(shape, lambda i: (0,) * len(shape))
    out = pl.pallas_call(
        _ffn_kernel,
        grid=(T // FFN_TILE,),
        in_specs=[
            pl.BlockSpec((FFN_TILE, D), lambda i: (i, 0)),
            full1((1, D)), full1((D, 2 * d_ff)), full1((d_ff, D)), full1((1, D)),
        ],
        out_specs=pl.BlockSpec((FFN_TILE, D), lambda i: (i, 0)),
        out_shape=jax.ShapeDtypeStruct((T, D), jnp.float32),
        scratch_shapes=[pltpu.VMEM((FFN_TILE, D), jnp.float32)],
        compiler_params=pltpu.CompilerParams(
            dimension_semantics=("arbitrary",), vmem_limit_bytes=VMEM_LIMIT),
        name="ffn",
    )(h.reshape(T, D), row(ffn_norm_w[0]), wgu, wd, row(final_norm_w))
    return out.reshape(B, S, D)
```

```python
import functools

import jax
import jax.numpy as jnp
from jax import lax
from jax.experimental import pallas as pl
from jax.experimental.pallas import tpu as pltpu

D_MODEL = 1024
HG_HEADS = 4
HG_DIM = 128
HG_WIDTH = HG_HEADS * HG_DIM
HG_BASE = 32
LRU_WIDTH = 512
LRU_BLOCKS = 8
LRU_GROUP = 256
CONV_WIDTH = 4
LRU_C = 8.0
D_MIX = HG_WIDTH + LRU_WIDTH
D_IN = 4 * HG_WIDTH + 2 * LRU_WIDTH
EPS = 1e-6
LOG2E = 1.4426950408889634

SUB = 128
PROJ_ROWS = 512
SEQ_TILE = 1024
FFN_TILE = 512
FF_CHUNK = 256
ROW_TILE = 8
TAIL = ROW_TILE
VMEM_LIMIT = 56 * 1024 * 1024

_NT = (((1,), (1,)), ((), ()))
_TN = (((0,), (0,)), ((), ()))


def _dot(a, b, dims=None):
    if dims is None:
        return jnp.dot(a, b, preferred_element_type=jnp.float32)
    return lax.dot_general(a, b, dims, preferred_element_type=jnp.float32)


def _rms(x, w):
    return x * lax.rsqrt(jnp.mean(x * x, axis=-1, keepdims=True) + EPS) * w


def _rows(shape):
    return lax.broadcasted_iota(jnp.int32, shape, 0)


def _tile_shift(x, s):
    n, w = x.shape
    return pltpu.roll(x.reshape(n // ROW_TILE, ROW_TILE, w), s, axis=1).reshape(n, w)


def _cumsum_rows(x):
    n = x.shape[0]
    sub = _rows(x.shape) % ROW_TILE
    s = 1
    while s < ROW_TILE:
        x = x + jnp.where(sub >= s, _tile_shift(x, s), 0.0)
        s *= 2
    outs = [x[0:ROW_TILE]]
    for t in range(1, n // ROW_TILE):
        outs.append(x[t * ROW_TILE:(t + 1) * ROW_TILE] + outs[-1][ROW_TILE - 1:ROW_TILE])
    return jnp.concatenate(outs, axis=0)


def _linear_scan_rows(a, b, h0):
    n = a.shape[0]
    sub = _rows(a.shape) % ROW_TILE
    s = 1
    while s < ROW_TILE:
        keep = sub >= s
        a_prev = jnp.where(keep, _tile_shift(a, s), 1.0)
        b_prev = jnp.where(keep, _tile_shift(b, s), 0.0)
        b = a * b_prev + b
        a = a * a_prev
        s *= 2
    outs = []
    h = h0
    for t in range(n // ROW_TILE):
        rows = slice(t * ROW_TILE, (t + 1) * ROW_TILE)
        outs.append(a[rows] * h + b[rows])
        h = outs[-1][ROW_TILE - 1:ROW_TILE]
    return jnp.concatenate(outs, axis=0), h


def _bcast_rows(r, n):
    return jnp.broadcast_to(r, (n, r.shape[1]))


def _mixer_kernel(x_ref, xnext_ref, nw_ref, win_ref, lb_ref, hgw_ref, cw_ref, cb_ref, wg_ref, bg_ref,
                  ap_ref, wout_ref, o_ref, proj0_ref, proj1_ref, xl0_ref, xl1_ref, mixed_ref, state_ref,
                  hcar_ref):
    bi = pl.program_id(0)
    si = pl.program_id(1)
    last_si = pl.num_programs(1) - 1
    proj_refs = (proj0_ref, proj1_ref)
    xl_refs = (xl0_ref, xl1_ref)

    @pl.when(si == 0)
    def _():
        state_ref[...] = jnp.zeros_like(state_ref)
        hcar_ref[...] = jnp.zeros_like(hcar_ref)

    nw = nw_ref[...]
    n_hg = 4 * HG_WIDTH

    lbp = lb_ref[...]
    lbe = jnp.exp(lbp - jnp.max(lbp, axis=0, keepdims=True))
    lb = lbe[0:1] / jnp.sum(lbe, axis=0, keepdims=True)
    neg_c_sp2 = (-LRU_C * LOG2E) * jax.nn.softplus(-ap_ref[...])
    hgw = hgw_ref[...]
    cw = cw_ref[...]
    cb = cb_ref[...]
    bg = bg_ref[...]

    row = _rows((SUB, SUB))
    col = lax.broadcasted_iota(jnp.int32, (SUB, SUB), 1)
    mask0 = (row // HG_BASE == col // HG_BASE) & (col <= row)
    mask1 = ((row // (2 * HG_BASE) == col // (2 * HG_BASE))
             & (row % (2 * HG_BASE) >= HG_BASE) & (col % (2 * HG_BASE) < HG_BASE))

    def in_proj(load_x, parity, load_tail):
        proj_ref, xl_ref = proj_refs[parity], xl_refs[parity]
        cell = []

        def xn():
            if not cell:
                cell.append(_rms(load_x(), nw).astype(jnp.bfloat16))
            return cell[0]

        def hg_chunk(c):
            proj_ref[:, c:c + 512] = _dot(xn(), win_ref[:, c:c + 512])

        def lru_x():
            xl_ref[0:TAIL, :] = load_tail()
            xl_ref[TAIL:TAIL + PROJ_ROWS, :] = _dot(xn(), win_ref[:, n_hg:n_hg + LRU_WIDTH])

        def lru_gate():
            proj_ref[:, n_hg:n_hg + LRU_WIDTH] = _dot(xn(), win_ref[:, n_hg + LRU_WIDTH:])

        return [functools.partial(hg_chunk, c) for c in range(0, n_hg, 512)] + [lru_x, lru_gate]

    def out_proj(blk):
        b0 = blk * PROJ_ROWS

        def chunk(c):
            o_ref[0, b0:b0 + PROJ_ROWS, c:c + 512] = (x_ref[0, b0:b0 + PROJ_ROWS, c:c + 512]
                                                      + _dot(mixed_ref[b0:b0 + PROJ_ROWS, :], wout_ref[:, c:c + 512]))

        return [functools.partial(chunk, c) for c in range(0, D_MODEL, 512)]

    def sub_tile(blk, j):
        proj_ref, xl_ref = proj_refs[blk % 2], xl_refs[blk % 2]
        p0 = j * SUB
        r0 = blk * PROJ_ROWS + p0
        q = proj_ref[p0:p0 + SUB,0:HG_WIDTH]
        fl = proj_ref[p0:p0 + SUB,HG_WIDTH:2 * HG_WIDTH]
        v = proj_ref[p0:p0 + SUB,2 * HG_WIDTH:3 * HG_WIDTH].astype(jnp.bfloat16)
        g = proj_ref[p0:p0 + SUB,3 * HG_WIDTH:4 * HG_WIDTH]
        q = q * jax.nn.sigmoid(q)
        f = lb + (1.0 - lb) * jax.nn.sigmoid(fl)
        k = 1.0 - f
        beta = _cumsum_rows(jnp.log(f) * LOG2E)
        r31 = beta[HG_BASE - 1:HG_BASE]
        r63 = beta[2 * HG_BASE - 1:2 * HG_BASE]
        r95 = beta[3 * HG_BASE - 1:3 * HG_BASE]
        r127 = beta[4 * HG_BASE - 1:4 * HG_BASE]
        ref0 = jnp.concatenate([jnp.zeros((HG_BASE, HG_WIDTH), jnp.float32), _bcast_rows(r31, HG_BASE),
                                _bcast_rows(r63, HG_BASE), _bcast_rows(r95, HG_BASE)], axis=0)
        ref1 = jnp.concatenate([_bcast_rows(r31, 2 * HG_BASE), _bcast_rows(r95, 2 * HG_BASE)], axis=0)
        b0 = beta - ref0
        q0 = (q * jnp.exp2(b0)).astype(jnp.bfloat16)
        k0 = (k * jnp.exp2(-b0)).astype(jnp.bfloat16)
        k1 = (k * jnp.exp2(jnp.minimum(ref1 - beta, 0.0))).astype(jnp.bfloat16)
        half = SUB // 2
        q2 = (q[half:] * jnp.exp2(beta[half:] - r63)).astype(jnp.bfloat16)
        k2 = jnp.concatenate([(k[:half] * jnp.exp2(r63 - beta[:half])).astype(jnp.bfloat16),
                              jnp.zeros((half, HG_WIDTH), jnp.bfloat16)], axis=0)
        qs = (q * jnp.exp2(beta)).astype(jnp.bfloat16)
        ks = (k * jnp.exp2(r127 - beta)).astype(jnp.bfloat16)
        dec_end = jnp.exp2(r127)
        gate = g * jax.nn.sigmoid(g)
        for h in range(HG_HEADS):
            hs = slice(h * HG_DIM, (h + 1) * HG_DIM)
            s01 = _dot(q0[:, hs], jnp.concatenate([k0[:, hs], k1[:, hs]], axis=0), _NT)
            sc = jnp.where(mask0, s01[:, :SUB], 0.0) + jnp.where(mask1, s01[:, SUB:], 0.0)
            s2 = _dot(q2[:, hs], k2[:, hs], _NT)
            sc = jnp.concatenate([sc[:half], sc[half:] + s2], axis=0)
            st = state_ref[h]
            vt = v[:, hs].T
            o = _dot(jnp.concatenate([sc.astype(jnp.bfloat16), qs[:, hs]], axis=1),
                     jnp.concatenate([vt, st.astype(jnp.bfloat16)], axis=1), _NT)
            state_ref[h] = st * dec_end[:, hs] + _dot(vt, ks[:, hs])
            o = o * lax.rsqrt(jnp.mean(o * o, axis=-1, keepdims=True) + EPS) * hgw
            mixed_ref[r0:r0 + SUB, hs] = (o * gate[:, hs]).astype(jnp.bfloat16)

        base = TAIL + p0
        xc = cb + xl_ref[base:base + SUB, :] * cw[CONV_WIDTH - 1:CONV_WIDTH]
        for tap in range(CONV_WIDTH - 1):
            sh = CONV_WIDTH - 1 - tap
            xc = xc + xl_ref[base - sh:base - sh + SUB, :] * cw[tap:tap + 1]
        xcb = xc.astype(jnp.bfloat16)
        gates = jnp.concatenate(
            [_dot(xcb[:, gi * LRU_GROUP:(gi + 1) * LRU_GROUP], wg_ref[gi]) for gi in range(LRU_WIDTH // LRU_GROUP)],
            axis=1)
        rr = jnp.concatenate([gates[:, 0:256], gates[:, 512:768]], axis=1)
        ii = jnp.concatenate([gates[:, 256:512], gates[:, 768:1024]], axis=1)
        rr = jax.nn.sigmoid(rr + bg[0:1])
        ii = jax.nn.sigmoid(ii + bg[1:2])
        a = jnp.exp2(rr * neg_c_sp2)
        b_in = jnp.sqrt(1.0 - a * a) * (ii * xc)
        hh, hcar_ref[...] = _linear_scan_rows(a, b_in, hcar_ref[...])
        gl = proj_ref[p0:p0 + SUB,n_hg:n_hg + LRU_WIDTH]
        mixed_ref[r0:r0 + SUB, HG_WIDTH:] = (hh * jax.nn.gelu(gl, approximate=True)).astype(jnp.bfloat16)

    zero_tail = lambda: jnp.zeros((TAIL, LRU_WIDTH), jnp.float32)
    tail_of = lambda parity: xl_refs[parity][PROJ_ROWS:PROJ_ROWS + TAIL, :]

    @pl.when((bi == 0) & (si == 0))
    def _():
        for step in in_proj(lambda: x_ref[0, 0:PROJ_ROWS, :], 0, zero_tail):
            step()

    next_tail = lambda: jnp.where(si == last_si, 0.0, tail_of(1))
    fills = (
        in_proj(lambda: x_ref[0, PROJ_ROWS:2 * PROJ_ROWS, :], 1, lambda: tail_of(0)),
        in_proj(lambda: xnext_ref[0], 0, next_tail) + out_proj(0),
    )
    n_sub = PROJ_ROWS // SUB
    for blk, fill in enumerate(fills):
        for j in range(n_sub):
            for step in fill[j * len(fill) // n_sub:(j + 1) * len(fill) // n_sub]:
                step()
            sub_tile(blk, j)
    for step in out_proj(1):
        step()


def _ffn_kernel(h_ref, nw_ref, wgu_ref, wd_ref, fw_ref, o_ref, acc_ref):
    d_ff = wd_ref.shape[0]
    n_ff = d_ff // FF_CHUNK
    h = h_ref[...]
    hn = _rms(h, nw_ref[...]).astype(jnp.bfloat16)

    def act_of(c):
        cols = slice(c * FF_CHUNK, (c + 1) * FF_CHUNK)
        gt = _dot(hn, wgu_ref[:, cols])
        up = _dot(hn, wgu_ref[:, d_ff + c * FF_CHUNK:d_ff + (c + 1) * FF_CHUNK])
        return (gt * jax.nn.sigmoid(gt) * up).astype(jnp.bfloat16)

    act = act_of(0)
    for c in range(n_ff):
        nxt = act_of(c + 1) if c + 1 < n_ff else None
        down = _dot(act, wd_ref[c * FF_CHUNK:(c + 1) * FF_CHUNK, :])
        if c == 0:
            acc_ref[...] = h + down
        else:
            acc_ref[...] += down
        act = nxt
    o_ref[...] = _rms(acc_ref[...], fw_ref[...])


def _block_diag_groups(w):
    per = LRU_GROUP // (LRU_WIDTH // LRU_BLOCKS)
    bd = LRU_WIDTH // LRU_BLOCKS
    w = w.reshape(LRU_WIDTH // LRU_GROUP, per, bd, bd)
    eye = jnp.eye(per, dtype=w.dtype)
    return jnp.einsum('gade,ab->gadbe', w, eye).reshape(LRU_WIDTH // LRU_GROUP, LRU_GROUP, LRU_GROUP)


def kernel(x, mix_norm_w, w_in, hg_lb, hg_norm_w, conv_w, conv_b, lru_wa, lru_ba, lru_wx, lru_bx,
           lru_a, w_out, ffn_norm_w, w_gate_up, w_down, final_norm_w):
    B, S, D = x.shape
    assert D == D_MODEL and w_in.shape == (1, D_MODEL, D_IN) and S % SEQ_TILE == 0 and SEQ_TILE == 2 * PROJ_ROWS
    bf = jnp.bfloat16
    d_ff = w_down.shape[1]
    assert d_ff % FF_CHUNK == 0
    n_ff = d_ff // FF_CHUNK

    win = w_in[0].astype(bf)
    wout = w_out[0].astype(bf)
    wg = jnp.concatenate([_block_diag_groups(lru_wa[0]), _block_diag_groups(lru_wx[0])], axis=-1).astype(bf)
    bg = jnp.stack([lru_ba[0], lru_bx[0]], axis=0)
    row = lambda a: a.reshape(1, -1)

    full = lambda shape: pl.BlockSpec(shape, lambda b, s: (0,) * len(shape))
    n_s = S // SEQ_TILE

    def next_block(b, s):
        wrap = s == n_s - 1
        return (jnp.minimum(jnp.where(wrap, b + 1, b), B - 1), jnp.where(wrap, 0, s + 1) * (SEQ_TILE // PROJ_ROWS), 0)

    h = pl.pallas_call(
        _mixer_kernel,
        grid=(B, n_s),
        in_specs=[
            pl.BlockSpec((1, SEQ_TILE, D), lambda b, s: (b, s, 0)),
            pl.BlockSpec((1, PROJ_ROWS, D), next_block),
            full((1, D)), full((D, D_IN)), full((2, HG_WIDTH)), full((1, HG_DIM)),
            full((CONV_WIDTH, LRU_WIDTH)), full((1, LRU_WIDTH)),
            full((LRU_WIDTH // LRU_GROUP, LRU_GROUP, 2 * LRU_GROUP)), full((2, LRU_WIDTH)),
            full((1, LRU_WIDTH)), full((D_MIX, D)),
        ],
        out_specs=pl.BlockSpec((1, SEQ_TILE, D), lambda b, s: (b, s, 0)),
        out_shape=jax.ShapeDtypeStruct((B, S, D), jnp.float32),
        scratch_shapes=[
            pltpu.VMEM((PROJ_ROWS, 4 * HG_WIDTH + LRU_WIDTH), jnp.float32),
            pltpu.VMEM((PROJ_ROWS, 4 * HG_WIDTH + LRU_WIDTH), jnp.float32),
            pltpu.VMEM((TAIL + PROJ_ROWS, LRU_WIDTH), jnp.float32),
            pltpu.VMEM((TAIL + PROJ_ROWS, LRU_WIDTH), jnp.float32),
            pltpu.VMEM((SEQ_TILE, D_MIX), jnp.bfloat16),
            pltpu.VMEM((HG_HEADS, HG_DIM, HG_DIM), jnp.float32),
            pltpu.VMEM((1, LRU_WIDTH), jnp.float32),
        ],
        compiler_params=pltpu.CompilerParams(
            dimension_semantics=("arbitrary", "arbitrary"), vmem_limit_bytes=VMEM_LIMIT),
        name="mixer",
    )(x, x, row(mix_norm_w[0]), win, hg_lb, row(hg_norm_w[0]), conv_w[0], row(conv_b[0]), wg, bg,
      row(lru_a[0]), wout)

    wgu = w_gate_up[0].astype(bf)
    wd = w_down[0].astype(bf)
    T = B * S
    full1 = lambda shape: pl.BlockSpec(shape, lambda i: (0,) * len(shape))
    out = pl.pallas_call(
        _ffn_kernel,
        grid=(T // FFN_TILE,),
        in_specs=[
            pl.BlockSpec((FFN_TILE, D), lambda i: (i, 0)),
            full1((1, D)), full1((D, 2 * d_ff)), full1((d_ff, D)), full1((1, D)),
        ],
        out_specs=pl.BlockSpec((FFN_TILE, D), lambda i: (i, 0)),
        out_shape=jax.ShapeDtypeStruct((T, D), jnp.float32),
        scratch_shapes=[pltpu.VMEM((FFN_TILE, D), jnp.float32)],
        compiler_params=pltpu.CompilerParams(
            dimension_semantics=("arbitrary",), vmem_limit_bytes=VMEM_LIMIT),
        name="ffn",
    )(h.reshape(T, D), row(ffn_norm_w[0]), wgu, wd, row(final_norm_w))
    return out.reshape(B, S, D)
```

```python
import functools

import jax
import jax.numpy as jnp
from jax import lax
from jax.experimental import pallas as pl
from jax.experimental.pallas import tpu as pltpu

D_MODEL = 1024
HG_HEADS = 4
HG_DIM = 128
HG_WIDTH = HG_HEADS * HG_DIM
HG_BASE = 32
LRU_WIDTH = 512
LRU_BLOCKS = 8
LRU_GROUP = 256
CONV_WIDTH = 4
LRU_C = 8.0
D_MIX = HG_WIDTH + LRU_WIDTH
D_IN = 4 * HG_WIDTH + 2 * LRU_WIDTH
EPS = 1e-6
LOG2E = 1.4426950408889634

SUB = 128
PROJ_ROWS = 512
SEQ_TILE = 1024
FFN_TILE = 512
FF_CHUNK = 256
LANES = 128
ROW_TILE = 8
N_SEG = ROW_TILE
SEG = SUB // N_SEG
TAIL = ROW_TILE
SEG_PITCH = TAIL + SEG
YIELDS_PER_SUB_TILE = 4
VMEM_LIMIT = 56 * 1024 * 1024

_NT = (((1,), (1,)), ((), ()))


def _dot(a, b, dims=None):
    if dims is None:
        return jnp.dot(a, b, preferred_element_type=jnp.float32)
    return lax.dot_general(a, b, dims, preferred_element_type=jnp.float32)


def _rms(x, w):
    return x * lax.rsqrt(jnp.mean(x * x, axis=-1, keepdims=True) + EPS) * w


def _rows(shape):
    return lax.broadcasted_iota(jnp.int32, shape, 0)


def _bcast_rows(r, n):
    return jnp.broadcast_to(r, (n, r.shape[1]))


def _mixer_kernel(x_ref, xnext_ref, nw_ref, win_ref, lb_ref, hgw_ref, cw_ref, cb_ref, wg_ref, bg_ref,
                  ap_ref, wout_ref, o_ref, proj0_ref, proj1_ref, xl0_ref, xl1_ref, hs_ref, mixed_ref, state_ref,
                  hcar_ref):
    bi = pl.program_id(0)
    si = pl.program_id(1)
    last_si = pl.num_programs(1) - 1
    proj_refs = (proj0_ref, proj1_ref)
    xl_refs = (xl0_ref, xl1_ref)
    n_ls = LRU_WIDTH // LANES

    @pl.when(si == 0)
    def _():
        state_ref[...] = jnp.zeros_like(state_ref)
        hcar_ref[...] = jnp.zeros_like(hcar_ref)

    nw = nw_ref[...]
    n_hg = 4 * HG_WIDTH

    lbp = lb_ref[...]
    lbe = jnp.exp(lbp - jnp.max(lbp, axis=0, keepdims=True))
    lb = lbe[0:1] / jnp.sum(lbe, axis=0, keepdims=True)
    neg_c_sp2 = (-LRU_C * LOG2E) * jax.nn.softplus(-ap_ref[...])
    hgw = hgw_ref[...]
    cw_rows = [jnp.broadcast_to(cw_ref[t:t + 1, :], (N_SEG, LRU_WIDTH)) for t in range(CONV_WIDTH)]
    cb_rows = jnp.broadcast_to(cb_ref[...], (N_SEG, LRU_WIDTH))
    bg = bg_ref[...]
    tri_r = _rows((SUB, 2 * SUB))
    tri_c = lax.broadcasted_iota(jnp.int32, (SUB, 2 * SUB), 1) % SUB
    tri2 = jnp.where(tri_c <= tri_r, 1.0, 0.0).astype(jnp.bfloat16)

    row = _rows((SUB, SUB))
    col = lax.broadcasted_iota(jnp.int32, (SUB, SUB), 1)
    mask0 = (row // HG_BASE == col // HG_BASE) & (col <= row)
    mask1 = ((row // (2 * HG_BASE) == col // (2 * HG_BASE))
             & (row % (2 * HG_BASE) >= HG_BASE) & (col % (2 * HG_BASE) < HG_BASE))

    def in_proj(load_x, parity, load_tail):
        proj_ref, xl_ref = proj_refs[parity], xl_refs[parity]
        cell = []

        def norm():
            cell.append(_rms(load_x(), nw).astype(jnp.bfloat16))

        def hg_chunk(c):
            proj_ref[:, c:c + 512] = _dot(cell[0], win_ref[:, c:c + 512])

        def lru_x():
            val = _dot(cell[0], win_ref[:, n_hg:n_hg + LRU_WIDTH])
            tail = load_tail()
            n_seg = PROJ_ROWS // SEG
            for ls in range(n_ls):
                lanes = slice(ls * LANES, (ls + 1) * LANES)
                xl_ref[ls, 0:TAIL, :] = tail[:, lanes]
                for s in range(n_seg):
                    xl_ref[ls, SEG_PITCH * s + TAIL:SEG_PITCH * (s + 1), :] = val[SEG * s:SEG * (s + 1), lanes]
                    if s + 1 < n_seg:
                        xl_ref[ls, SEG_PITCH * (s + 1):SEG_PITCH * (s + 1) + TAIL, :] = (
                            val[SEG * (s + 1) - TAIL:SEG * (s + 1), lanes])

        def lru_gate():
            proj_ref[:, n_hg:n_hg + LRU_WIDTH] = _dot(cell[0], win_ref[:, n_hg + LRU_WIDTH:])

        return [norm] + [functools.partial(hg_chunk, c) for c in range(0, n_hg, 512)] + [lru_x, lru_gate]

    def out_proj(blk):
        b0 = blk * PROJ_ROWS

        def chunk(c):
            o_ref[0, b0:b0 + PROJ_ROWS, c:c + 512] = (x_ref[0, b0:b0 + PROJ_ROWS, c:c + 512]
                                                      + _dot(mixed_ref[b0:b0 + PROJ_ROWS, :], wout_ref[:, c:c + 512]))

        return [functools.partial(chunk, c) for c in range(0, D_MODEL, 512)]

    def sub_tile(blk, j):
        proj_ref, xl_ref = proj_refs[blk % 2], xl_refs[blk % 2]
        p0 = j * SUB
        r0 = blk * PROJ_ROWS + p0
        half = SUB // 2
        q = proj_ref[p0:p0 + SUB, 0:HG_WIDTH]
        fl = proj_ref[p0:p0 + SUB, HG_WIDTH:2 * HG_WIDTH]
        v = proj_ref[p0:p0 + SUB, 2 * HG_WIDTH:3 * HG_WIDTH].astype(jnp.bfloat16)
        q = q * jax.nn.sigmoid(q)
        f = lb + (1.0 - lb) * jax.nn.sigmoid(fl)
        k = 1.0 - f
        lf = jnp.log(f) * LOG2E
        lf_hi = lf.astype(jnp.bfloat16)
        lf_lo = (lf - lf_hi.astype(jnp.float32)).astype(jnp.bfloat16)
        slab_start = SEG_PITCH * j * N_SEG + TAIL

        def x_slab(r):
            return jnp.concatenate([xl_ref[ls, pl.ds(slab_start + r, N_SEG, stride=SEG_PITCH), :]
                                    for ls in range(n_ls)], axis=1)

        xs = {r: x_slab(r) for r in range(1 - CONV_WIDTH, SEG)}
        xc_slabs = []
        for r in range(SEG):
            acc = cb_rows
            for tap in range(CONV_WIDTH):
                acc = acc + xs[r - (CONV_WIDTH - 1 - tap)] * cw_rows[tap]
            xc_slabs.append(acc)
        xc = jnp.concatenate(xc_slabs, axis=0)
        xcb = xc.astype(jnp.bfloat16)
        yield
        beta = _dot(tri2, jnp.concatenate([lf_hi, lf_lo], axis=0))
        gates = jnp.concatenate(
            [_dot(xcb[:, gi * LRU_GROUP:(gi + 1) * LRU_GROUP], wg_ref[gi]) for gi in range(LRU_WIDTH // LRU_GROUP)],
            axis=1)
        yield
        r31 = beta[HG_BASE - 1:HG_BASE]
        r63 = beta[2 * HG_BASE - 1:2 * HG_BASE]
        r95 = beta[3 * HG_BASE - 1:3 * HG_BASE]
        r127 = beta[4 * HG_BASE - 1:4 * HG_BASE]
        ref0 = jnp.concatenate([jnp.zeros((HG_BASE, HG_WIDTH), jnp.float32), _bcast_rows(r31, HG_BASE),
                                _bcast_rows(r63, HG_BASE), _bcast_rows(r95, HG_BASE)], axis=0)
        ref1 = jnp.concatenate([_bcast_rows(r31, 2 * HG_BASE), _bcast_rows(r95, 2 * HG_BASE)], axis=0)
        b0 = beta - ref0
        q0 = (q * jnp.exp2(b0)).astype(jnp.bfloat16)
        k0 = (k * jnp.exp2(-b0)).astype(jnp.bfloat16)
        k1 = (k * jnp.exp2(jnp.minimum(ref1 - beta, 0.0))).astype(jnp.bfloat16)
        q2 = (q[half:] * jnp.exp2(beta[half:] - r63)).astype(jnp.bfloat16)
        k2 = jnp.concatenate([(k[:half] * jnp.exp2(r63 - beta[:half])).astype(jnp.bfloat16),
                              jnp.zeros((half, HG_WIDTH), jnp.bfloat16)], axis=0)
        qs = (q * jnp.exp2(beta)).astype(jnp.bfloat16)
        ks = (k * jnp.exp2(r127 - beta)).astype(jnp.bfloat16)
        dec_end = jnp.exp2(r127)
        rr = jnp.concatenate([gates[:, 0:256], gates[:, 512:768]], axis=1)
        ii = jnp.concatenate([gates[:, 256:512], gates[:, 768:1024]], axis=1)
        rr = jax.nn.sigmoid(rr + bg[0:1])
        ii = jax.nn.sigmoid(ii + bg[1:2])
        a = jnp.exp2(rr * neg_c_sp2)
        b_in = jnp.sqrt(1.0 - a * a) * (ii * xc)
        h_loc, a_loc = [b_in[0:N_SEG]], [a[0:N_SEG]]
        for r in range(1, SEG):
            rows = slice(r * N_SEG, (r + 1) * N_SEG)
            h_loc.append(a[rows] * h_loc[-1] + b_in[rows])
            a_loc.append(a[rows] * a_loc[-1])
        carries = [hcar_ref[...]]
        for s in range(N_SEG):
            carries.append(a_loc[-1][s:s + 1] * carries[-1] + h_loc[-1][s:s + 1])
        hcar_ref[...] = carries[-1]
        carry = jnp.concatenate(carries[:-1], axis=0)
        hs0 = j * N_SEG * SEG_PITCH
        for r in range(SEG):
            h_r = h_loc[r] + a_loc[r] * carry
            for ls in range(n_ls):
                hs_ref[ls, pl.ds(hs0 + TAIL + r, N_SEG, stride=SEG_PITCH), :] = h_r[:, ls * LANES:(ls + 1) * LANES]
        hh = jnp.concatenate(
            [jnp.concatenate([hs_ref[ls, hs0 + SEG_PITCH * s + TAIL:hs0 + SEG_PITCH * (s + 1), :]
                              for s in range(N_SEG)], axis=0) for ls in range(n_ls)], axis=1)
        gl = proj_ref[p0:p0 + SUB, n_hg:n_hg + LRU_WIDTH]
        mixed_ref[r0:r0 + SUB, HG_WIDTH:] = (hh * jax.nn.gelu(gl, approximate=True)).astype(jnp.bfloat16)
        yield
        heads = [slice(h * HG_DIM, (h + 1) * HG_DIM) for h in range(HG_HEADS)]
        s01 = [_dot(q0[:, hs], jnp.concatenate([k0[:, hs], k1[:, hs]], axis=0), _NT) for hs in heads]
        s2 = [_dot(q2[:, hs], k2[:, hs], _NT) for hs in heads]
        vt = [v[:, hs].T for hs in heads]
        inc = [_dot(vt[h], ks[:, hs]) for h, hs in enumerate(heads)]
        yield
        g = proj_ref[p0:p0 + SUB, 3 * HG_WIDTH:4 * HG_WIDTH]
        gate = g * jax.nn.sigmoid(g)
        for h, hs in enumerate(heads):
            sc = jnp.where(mask0, s01[h][:, :SUB], 0.0) + jnp.where(mask1, s01[h][:, SUB:], 0.0)
            sc = jnp.concatenate([sc[:half], sc[half:] + s2[h]], axis=0)
            st = state_ref[h]
            o = _dot(jnp.concatenate([sc.astype(jnp.bfloat16), qs[:, hs]], axis=1),
                     jnp.concatenate([vt[h], st.astype(jnp.bfloat16)], axis=1), _NT)
            state_ref[h] = st * dec_end[:, hs] + inc[h]
            o = o * lax.rsqrt(jnp.mean(o * o, axis=-1, keepdims=True) + EPS) * hgw
            mixed_ref[r0:r0 + SUB, hs] = (o * gate[:, hs]).astype(jnp.bfloat16)

    zero_tail = lambda: jnp.zeros((TAIL, LRU_WIDTH), jnp.float32)
    last_rows = slice(SEG_PITCH * (PROJ_ROWS // SEG) - TAIL, SEG_PITCH * (PROJ_ROWS // SEG))
    tail_of = lambda parity: jnp.concatenate([xl_refs[parity][ls, last_rows, :] for ls in range(n_ls)], axis=1)

    @pl.when((bi == 0) & (si == 0))
    def _():
        for step in in_proj(lambda: x_ref[0, 0:PROJ_ROWS, :], 0, zero_tail):
            step()

    next_tail = lambda: jnp.where(si == last_si, 0.0, tail_of(1))
    fills = (
        in_proj(lambda: x_ref[0, PROJ_ROWS:2 * PROJ_ROWS, :], 1, lambda: tail_of(0)),
        in_proj(lambda: xnext_ref[0], 0, next_tail) + out_proj(0),
    )
    for blk, fill in enumerate(fills):
        points = [sub_tile(blk, j) for j in range(PROJ_ROWS // SUB)]
        n_points = len(points) * YIELDS_PER_SUB_TILE
        seen = done = 0
        for gen in points:
            for _ in gen:
                seen += 1
                upto = seen * len(fill) // n_points
                for step in fill[done:upto]:
                    step()
                done = upto
        assert seen == n_points and done == len(fill)
    for step in out_proj(1):
        step()


def _ffn_kernel(h_ref, nw_ref, wgu_ref, wd_ref, fw_ref, o_ref, acc_ref):
    d_ff = wd_ref.shape[0]
    n_ff = d_ff // FF_CHUNK
    h = h_ref[...]
    hn = _rms(h, nw_ref[...]).astype(jnp.bfloat16)

    def act_of(c):
        cols = slice(c * FF_CHUNK, (c + 1) * FF_CHUNK)
        gt = _dot(hn, wgu_ref[:, cols])
        up = _dot(hn, wgu_ref[:, d_ff + c * FF_CHUNK:d_ff + (c + 1) * FF_CHUNK])
        return (gt * jax.nn.sigmoid(gt) * up).astype(jnp.bfloat16)

    act = act_of(0)
    for c in range(n_ff):
        nxt = act_of(c + 1) if c + 1 < n_ff else None
        down = _dot(act, wd_ref[c * FF_CHUNK:(c + 1) * FF_CHUNK, :])
        if c == 0:
            acc_ref[...] = h + down
        else:
            acc_ref[...] += down
        act = nxt
    o_ref[...] = _rms(acc_ref[...], fw_ref[...])


def _block_diag_groups(w):
    per = LRU_GROUP // (LRU_WIDTH // LRU_BLOCKS)
    bd = LRU_WIDTH // LRU_BLOCKS
    w = w.reshape(LRU_WIDTH // LRU_GROUP, per, bd, bd)
    eye = jnp.eye(per, dtype=w.dtype)
    return jnp.einsum('gade,ab->gadbe', w, eye).reshape(LRU_WIDTH // LRU_GROUP, LRU_GROUP, LRU_GROUP)


def kernel(x, mix_norm_w, w_in, hg_lb, hg_norm_w, conv_w, conv_b, lru_wa, lru_ba, lru_wx, lru_bx,
           lru_a, w_out, ffn_norm_w, w_gate_up, w_down, final_norm_w):
    B, S, D = x.shape
    assert D == D_MODEL and w_in.shape == (1, D_MODEL, D_IN) and S % SEQ_TILE == 0 and SEQ_TILE == 2 * PROJ_ROWS
    bf = jnp.bfloat16
    d_ff = w_down.shape[1]
    assert d_ff % FF_CHUNK == 0

    win = w_in[0].astype(bf)
    wout = w_out[0].astype(bf)
    wg = jnp.concatenate([_block_diag_groups(lru_wa[0]), _block_diag_groups(lru_wx[0])], axis=-1).astype(bf)
    bg = jnp.stack([lru_ba[0], lru_bx[0]], axis=0)
    row = lambda a: a.reshape(1, -1)

    full = lambda shape: pl.BlockSpec(shape, lambda b, s: (0,) * len(shape))
    n_s = S // SEQ_TILE

    def next_block(b, s):
        wrap = s == n_s - 1
        return (jnp.minimum(jnp.where(wrap, b + 1, b), B - 1), jnp.where(wrap, 0, s + 1) * (SEQ_TILE // PROJ_ROWS), 0)

    slab_rows = SEG_PITCH * (PROJ_ROWS // SEG)
    h = pl.pallas_call(
        _mixer_kernel,
        grid=(B, n_s),
        in_specs=[
            pl.BlockSpec((1, SEQ_TILE, D), lambda b, s: (b, s, 0)),
            pl.BlockSpec((1, PROJ_ROWS, D), next_block),
            full((1, D)), full((D, D_IN)), full((2, HG_WIDTH)), full((1, HG_DIM)),
            full((CONV_WIDTH, LRU_WIDTH)), full((1, LRU_WIDTH)),
            full((LRU_WIDTH // LRU_GROUP, LRU_GROUP, 2 * LRU_GROUP)), full((2, LRU_WIDTH)),
            full((1, LRU_WIDTH)), full((D_MIX, D)),
        ],
        out_specs=pl.BlockSpec((1, SEQ_TILE, D), lambda b, s: (b, s, 0)),
        out_shape=jax.ShapeDtypeStruct((B, S, D), jnp.float32),
        scratch_shapes=[
            pltpu.VMEM((PROJ_ROWS, 4 * HG_WIDTH + LRU_WIDTH), jnp.float32),
            pltpu.VMEM((PROJ_ROWS, 4 * HG_WIDTH + LRU_WIDTH), jnp.float32),
            pltpu.VMEM((LRU_WIDTH // LANES, slab_rows, LANES), jnp.float32),
            pltpu.VMEM((LRU_WIDTH // LANES, slab_rows, LANES), jnp.float32),
            pltpu.VMEM((LRU_WIDTH // LANES, slab_rows, LANES), jnp.float32),
            pltpu.VMEM((SEQ_TILE, D_MIX), jnp.bfloat16),
            pltpu.VMEM((HG_HEADS, HG_DIM, HG_DIM), jnp.float32),
            pltpu.VMEM((1, LRU_WIDTH), jnp.float32),
        ],
        compiler_params=pltpu.CompilerParams(
            dimension_semantics=("arbitrary", "arbitrary"), vmem_limit_bytes=VMEM_LIMIT),
        name="mixer",
    )(x, x, row(mix_norm_w[0]), win, hg_lb, row(hg_norm_w[0]), conv_w[0], row(conv_b[0]), wg, bg,
      row(lru_a[0]), wout)

    wgu = w_gate_up[0].astype(bf)
    wd = w_down[0].astype(bf)
    T = B * S
    full1 = lambda shape: pl.BlockSpec(shape, lambda i: (0,) * len(shape))
    out = pl.pallas_call(
        _ffn_kernel,
        grid=(T // FFN_TILE,),
        in_specs=[
            pl.BlockSpec((FFN_TILE, D), lambda i: (i, 0)),
            full1((1, D)), full1((D, 2 * d_ff)), full1((d_ff, D)), full1((1, D)),
        ],
        out_specs=pl.BlockSpec((FFN_TILE, D), lambda i: (i, 0)),
        out_shape=jax.ShapeDtypeStruct((T, D), jnp.float32),
        scratch_shapes=[pltpu.VMEM((FFN_TILE, D), jnp.float32)],
        compiler_params=pltpu.CompilerParams(
            dimension_semantics=("arbitrary",), vmem_limit_bytes=VMEM_LIMIT),
        name="ffn",
    )(h.reshape(T, D), row(ffn_norm_w[0]), wgu, wd, row(final_norm_w))
    return out.reshape(B, S, D)
```

```python
import functools

import jax
import jax.numpy as jnp
from jax import lax
from jax.experimental import pallas as pl
from jax.experimental.pallas import tpu as pltpu

D_MODEL = 1024
HG_HEADS = 4
HG_DIM = 128
HG_WIDTH = HG_HEADS * HG_DIM
HG_BASE = 32
LRU_WIDTH = 512
LRU_BLOCKS = 8
LRU_GROUP = 256
CONV_WIDTH = 4
LRU_C = 8.0
D_MIX = HG_WIDTH + LRU_WIDTH
D_IN = 4 * HG_WIDTH + 2 * LRU_WIDTH
EPS = 1e-6
LOG2E = 1.4426950408889634

SUB = 128
PROJ_ROWS = 512
SEQ_TILE = 1024
FFN_TILE = 512
FF_CHUNK = 256
COL_CHUNK = 512
LANES = 128
ROW_TILE = 8
N_SEG = ROW_TILE
SEG = SUB // N_SEG
TAIL = ROW_TILE
SEG_PITCH = TAIL + SEG
SLAB_ROWS = SEG_PITCH * (FFN_TILE // SEG)
N_LS = LRU_WIDTH // LANES
HGRN_YIELDS = 3
VMEM_LIMIT = 56 * 1024 * 1024

_NT = (((1,), (1,)), ((), ()))


def _dot(a, b, dims=None):
    if dims is None:
        return jnp.dot(a, b, preferred_element_type=jnp.float32)
    return lax.dot_general(a, b, dims, preferred_element_type=jnp.float32)


def _rms(x, w):
    return x * lax.rsqrt(jnp.mean(x * x, axis=-1, keepdims=True) + EPS) * w


def _rows(shape):
    return lax.broadcasted_iota(jnp.int32, shape, 0)


def _bcast_rows(r, n):
    return jnp.broadcast_to(r, (n, r.shape[1]))


def _hgrn_kernel(x_ref, xnext_ref, nw_ref, win_ref, lb_ref, hgw_ref, ohg_ref, lru_ref,
                 proj0_ref, proj1_ref, state_ref):
    bi = pl.program_id(0)
    si = pl.program_id(1)
    proj_refs = (proj0_ref, proj1_ref)

    @pl.when(si == 0)
    def _():
        state_ref[...] = jnp.zeros_like(state_ref)

    nw = nw_ref[...]
    n_hg = 4 * HG_WIDTH

    lbp = lb_ref[...]
    lbe = jnp.exp(lbp - jnp.max(lbp, axis=0, keepdims=True))
    lb = lbe[0:1] / jnp.sum(lbe, axis=0, keepdims=True)
    hgw = hgw_ref[...]
    tri_r = _rows((SUB, 2 * SUB))
    tri_c = lax.broadcasted_iota(jnp.int32, (SUB, 2 * SUB), 1) % SUB
    tri2 = jnp.where(tri_c <= tri_r, 1.0, 0.0).astype(jnp.bfloat16)

    row = _rows((SUB, SUB))
    col = lax.broadcasted_iota(jnp.int32, (SUB, SUB), 1)
    mask0 = (row // HG_BASE == col // HG_BASE) & (col <= row)
    mask1 = ((row // (2 * HG_BASE) == col // (2 * HG_BASE))
             & (row % (2 * HG_BASE) >= HG_BASE) & (col % (2 * HG_BASE) < HG_BASE))

    def in_proj(load_x, parity, store_lru):
        proj_ref = proj_refs[parity]
        cell = []

        def norm():
            cell.append(_rms(load_x(), nw).astype(jnp.bfloat16))

        def hg_chunk(c):
            proj_ref[:, c:c + COL_CHUNK] = _dot(cell[0], win_ref[:, c:c + COL_CHUNK])

        def lru_chunk(c):
            store_lru(c - n_hg, _dot(cell[0], win_ref[:, c:c + COL_CHUNK]))

        return ([norm] + [functools.partial(hg_chunk, c) for c in range(0, n_hg, COL_CHUNK)]
                + [functools.partial(lru_chunk, c) for c in range(n_hg, D_IN, COL_CHUNK)])

    def sub_tile(blk, j):
        proj_ref = proj_refs[blk % 2]
        p0 = j * SUB
        r0 = blk * PROJ_ROWS + p0
        half = SUB // 2
        q = proj_ref[p0:p0 + SUB, 0:HG_WIDTH]
        fl = proj_ref[p0:p0 + SUB, HG_WIDTH:2 * HG_WIDTH]
        v = proj_ref[p0:p0 + SUB, 2 * HG_WIDTH:3 * HG_WIDTH].astype(jnp.bfloat16)
        q = q * jax.nn.sigmoid(q)
        f = lb + (1.0 - lb) * jax.nn.sigmoid(fl)
        k = 1.0 - f
        lf = jnp.log(f) * LOG2E
        lf_hi = lf.astype(jnp.bfloat16)
        lf_lo = (lf - lf_hi.astype(jnp.float32)).astype(jnp.bfloat16)
        yield
        beta = _dot(tri2, jnp.concatenate([lf_hi, lf_lo], axis=0))
        r31 = beta[HG_BASE - 1:HG_BASE]
        r63 = beta[2 * HG_BASE - 1:2 * HG_BASE]
        r95 = beta[3 * HG_BASE - 1:3 * HG_BASE]
        r127 = beta[4 * HG_BASE - 1:4 * HG_BASE]
        ref0 = jnp.concatenate([jnp.zeros((HG_BASE, HG_WIDTH), jnp.float32), _bcast_rows(r31, HG_BASE),
                                _bcast_rows(r63, HG_BASE), _bcast_rows(r95, HG_BASE)], axis=0)
        ref1 = jnp.concatenate([_bcast_rows(r31, 2 * HG_BASE), _bcast_rows(r95, 2 * HG_BASE)], axis=0)
        b0 = beta - ref0
        q0 = (q * jnp.exp2(b0)).astype(jnp.bfloat16)
        k0 = (k * jnp.exp2(-b0)).astype(jnp.bfloat16)
        k1 = (k * jnp.exp2(jnp.minimum(ref1 - beta, 0.0))).astype(jnp.bfloat16)
        q2 = (q[half:] * jnp.exp2(beta[half:] - r63)).astype(jnp.bfloat16)
        k2 = jnp.concatenate([(k[:half] * jnp.exp2(r63 - beta[:half])).astype(jnp.bfloat16),
                              jnp.zeros((half, HG_WIDTH), jnp.bfloat16)], axis=0)
        qs = (q * jnp.exp2(beta)).astype(jnp.bfloat16)
        ks = (k * jnp.exp2(r127 - beta)).astype(jnp.bfloat16)
        dec_end = jnp.exp2(r127)
        yield
        heads = [slice(h * HG_DIM, (h + 1) * HG_DIM) for h in range(HG_HEADS)]
        s01 = [_dot(q0[:, hs], jnp.concatenate([k0[:, hs], k1[:, hs]], axis=0), _NT) for hs in heads]
        s2 = [_dot(q2[:, hs], k2[:, hs], _NT) for hs in heads]
        vt = [v[:, hs].T for hs in heads]
        inc = [_dot(vt[h], ks[:, hs]) for h, hs in enumerate(heads)]
        yield
        g = proj_ref[p0:p0 + SUB, 3 * HG_WIDTH:4 * HG_WIDTH]
        gate = g * jax.nn.sigmoid(g)
        for h, hs in enumerate(heads):
            sc = jnp.where(mask0, s01[h][:, :SUB], 0.0) + jnp.where(mask1, s01[h][:, SUB:], 0.0)
            sc = jnp.concatenate([sc[:half], sc[half:] + s2[h]], axis=0)
            st = state_ref[h]
            o = _dot(jnp.concatenate([sc.astype(jnp.bfloat16), qs[:, hs]], axis=1),
                     jnp.concatenate([vt[h], st.astype(jnp.bfloat16)], axis=1), _NT)
            state_ref[h] = st * dec_end[:, hs] + inc[h]
            o = o * lax.rsqrt(jnp.mean(o * o, axis=-1, keepdims=True) + EPS) * hgw
            ohg_ref[0, r0:r0 + SUB, hs] = (o * gate[:, hs]).astype(jnp.bfloat16)

    def lru_to_scratch(off, val):
        proj0_ref[:, n_hg + off:n_hg + off + COL_CHUNK] = val

    def lru_to_output(off, val):
        lru_ref[0, PROJ_ROWS:2 * PROJ_ROWS, off:off + COL_CHUNK] = val

    def flush_first_block():
        lru_ref[0, 0:PROJ_ROWS, :] = proj0_ref[:, n_hg:D_IN]

    @pl.when((bi == 0) & (si == 0))
    def _():
        for step in in_proj(lambda: x_ref[0, 0:PROJ_ROWS, :], 0, lru_to_scratch):
            step()

    fills = (
        [flush_first_block] + in_proj(lambda: x_ref[0, PROJ_ROWS:2 * PROJ_ROWS, :], 1, lru_to_output),
        in_proj(lambda: xnext_ref[0], 0, lru_to_scratch),
    )
    for blk, fill in enumerate(fills):
        points = [sub_tile(blk, j) for j in range(PROJ_ROWS // SUB)]
        n_points = len(points) * HGRN_YIELDS
        seen = done = 0
        for gen in points:
            for _ in gen:
                seen += 1
                upto = seen * len(fill) // n_points
                for step in fill[done:upto]:
                    step()
                done = upto
        assert seen == n_points and done == len(fill)


def _lru_ffn_kernel(x_ref, ohg_ref, lru_ref, cw_ref, cb_ref, wg_ref, bg_ref, ap_ref, wout_ref, nw_ref, wgu_ref,
                    wd_ref, fw_ref, o_ref, xl_ref, hs_ref, mlru_ref, hcar_ref, acc_ref, *, tiles_per_seq):
    t = pl.program_id(0)
    seq_start = (t % tiles_per_seq) == 0
    d_ff = wd_ref.shape[0]
    n_ff = d_ff // FF_CHUNK
    last_rows = slice(SLAB_ROWS - TAIL, SLAB_ROWS)

    @pl.when(t == 0)
    def _():
        mlru_ref[...] = jnp.zeros_like(mlru_ref)
        hcar_ref[...] = jnp.zeros_like(hcar_ref)
        xl_ref[:, last_rows, :] = jnp.zeros((N_LS, TAIL, LANES), jnp.float32)

    mixed = jnp.concatenate([ohg_ref[0], mlru_ref[...]], axis=1)
    h = x_ref[0] + _dot(mixed, wout_ref[...])
    hn = _rms(h, nw_ref[...]).astype(jnp.bfloat16)

    neg_c_sp2 = (-LRU_C * LOG2E) * jax.nn.softplus(-ap_ref[...])
    cw_rows = [jnp.broadcast_to(cw_ref[k:k + 1, :], (N_SEG, LRU_WIDTH)) for k in range(CONV_WIDTH)]
    cb_rows = jnp.broadcast_to(cb_ref[...], (N_SEG, LRU_WIDTH))
    bg = bg_ref[...]

    def store_slabs():
        val = lru_ref[0, :, 0:LRU_WIDTH]
        n_seg = FFN_TILE // SEG
        for ls in range(N_LS):
            lanes = slice(ls * LANES, (ls + 1) * LANES)
            xl_ref[ls, 0:TAIL, :] = jnp.where(seq_start, 0.0, xl_ref[ls, last_rows, :])
            for s in range(n_seg):
                xl_ref[ls, SEG_PITCH * s + TAIL:SEG_PITCH * (s + 1), :] = val[SEG * s:SEG * (s + 1), lanes]
                if s + 1 < n_seg:
                    xl_ref[ls, SEG_PITCH * (s + 1):SEG_PITCH * (s + 1) + TAIL, :] = (
                        val[SEG * (s + 1) - TAIL:SEG * (s + 1), lanes])

    conv_out = {}

    def conv(j):
        slab_start = SEG_PITCH * j * N_SEG + TAIL

        def x_slab(r):
            return jnp.concatenate([xl_ref[ls, pl.ds(slab_start + r, N_SEG, stride=SEG_PITCH), :]
                                    for ls in range(N_LS)], axis=1)

        xs = {r: x_slab(r) for r in range(1 - CONV_WIDTH, SEG)}
        xc_slabs = []
        for r in range(SEG):
            acc = cb_rows
            for tap in range(CONV_WIDTH):
                acc = acc + xs[r - (CONV_WIDTH - 1 - tap)] * cw_rows[tap]
            xc_slabs.append(acc)
        conv_out[j] = jnp.concatenate(xc_slabs, axis=0)

    def recur(j):
        xc = conv_out[j]
        xcb = xc.astype(jnp.bfloat16)
        gates = jnp.concatenate(
            [_dot(xcb[:, gi * LRU_GROUP:(gi + 1) * LRU_GROUP], wg_ref[gi]) for gi in range(LRU_WIDTH // LRU_GROUP)],
            axis=1)
        rr = jnp.concatenate([gates[:, 0:256], gates[:, 512:768]], axis=1)
        ii = jnp.concatenate([gates[:, 256:512], gates[:, 768:1024]], axis=1)
        rr = jax.nn.sigmoid(rr + bg[0:1])
        ii = jax.nn.sigmoid(ii + bg[1:2])
        a = jnp.exp2(rr * neg_c_sp2)
        b_in = jnp.sqrt(1.0 - a * a) * (ii * xc)
        h_loc, a_loc = [b_in[0:N_SEG]], [a[0:N_SEG]]
        for r in range(1, SEG):
            rows = slice(r * N_SEG, (r + 1) * N_SEG)
            h_loc.append(a[rows] * h_loc[-1] + b_in[rows])
            a_loc.append(a[rows] * a_loc[-1])
        h_in = hcar_ref[...]
        if j == 0:
            h_in = jnp.where(seq_start, 0.0, h_in)
        carries = [h_in]
        for s in range(N_SEG):
            carries.append(a_loc[-1][s:s + 1] * carries[-1] + h_loc[-1][s:s + 1])
        hcar_ref[...] = carries[-1]
        carry = jnp.concatenate(carries[:-1], axis=0)
        hs0 = j * N_SEG * SEG_PITCH
        for r in range(SEG):
            h_r = h_loc[r] + a_loc[r] * carry
            for ls in range(N_LS):
                hs_ref[ls, pl.ds(hs0 + TAIL + r, N_SEG, stride=SEG_PITCH), :] = h_r[:, ls * LANES:(ls + 1) * LANES]
        hh = jnp.concatenate(
            [jnp.concatenate([hs_ref[ls, hs0 + SEG_PITCH * s + TAIL:hs0 + SEG_PITCH * (s + 1), :]
                              for s in range(N_SEG)], axis=0) for ls in range(N_LS)], axis=1)
        gl = lru_ref[0, j * SUB:(j + 1) * SUB, LRU_WIDTH:]
        mlru_ref[j * SUB:(j + 1) * SUB, :] = (hh * jax.nn.gelu(gl, approximate=True)).astype(jnp.bfloat16)

    pieces = [store_slabs]
    for j in range(FFN_TILE // SUB):
        pieces += [functools.partial(conv, j), functools.partial(recur, j)]
    assert len(pieces) <= n_ff

    def act_of(c):
        cols = slice(c * FF_CHUNK, (c + 1) * FF_CHUNK)
        gt = _dot(hn, wgu_ref[:, cols])
        up = _dot(hn, wgu_ref[:, d_ff + c * FF_CHUNK:d_ff + (c + 1) * FF_CHUNK])
        return (gt * jax.nn.sigmoid(gt) * up).astype(jnp.bfloat16)

    act = act_of(0)
    for c in range(n_ff):
        if c < len(pieces):
            pieces[c]()
        nxt = act_of(c + 1) if c + 1 < n_ff else None
        down = _dot(act, wd_ref[c * FF_CHUNK:(c + 1) * FF_CHUNK, :])
        if c == 0:
            acc_ref[...] = h + down
        else:
            acc_ref[...] += down
        act = nxt
    o_ref[0] = _rms(acc_ref[...], fw_ref[...])


def _block_diag_groups(w):
    per = LRU_GROUP // (LRU_WIDTH // LRU_BLOCKS)
    bd = LRU_WIDTH // LRU_BLOCKS
    w = w.reshape(LRU_WIDTH // LRU_GROUP, per, bd, bd)
    eye = jnp.eye(per, dtype=w.dtype)
    return jnp.einsum('gade,ab->gadbe', w, eye).reshape(LRU_WIDTH // LRU_GROUP, LRU_GROUP, LRU_GROUP)


def kernel(x, mix_norm_w, w_in, hg_lb, hg_norm_w, conv_w, conv_b, lru_wa, lru_ba, lru_wx, lru_bx,
           lru_a, w_out, ffn_norm_w, w_gate_up, w_down, final_norm_w):
    B, S, D = x.shape
    assert D == D_MODEL and w_in.shape == (1, D_MODEL, D_IN) and S % SEQ_TILE == 0 and SEQ_TILE == 2 * PROJ_ROWS
    bf = jnp.bfloat16
    d_ff = w_down.shape[1]
    assert d_ff % FF_CHUNK == 0

    win = w_in[0].astype(bf)
    wout = w_out[0].astype(bf)
    wg = jnp.concatenate([_block_diag_groups(lru_wa[0]), _block_diag_groups(lru_wx[0])], axis=-1).astype(bf)
    bg = jnp.stack([lru_ba[0], lru_bx[0]], axis=0)
    row = lambda a: a.reshape(1, -1)
    once = pl.Buffered(1)

    full = lambda shape: pl.BlockSpec(shape, lambda b, s: (0,) * len(shape), pipeline_mode=once)
    n_s = S // SEQ_TILE

    def next_block(b, s):
        wrap = s == n_s - 1
        return (jnp.minimum(jnp.where(wrap, b + 1, b), B - 1), jnp.where(wrap, 0, s + 1) * (SEQ_TILE // PROJ_ROWS), 0)

    o_hg, lru_in = pl.pallas_call(
        _hgrn_kernel,
        grid=(B, n_s),
        in_specs=[
            pl.BlockSpec((1, SEQ_TILE, D), lambda b, s: (b, s, 0)),
            pl.BlockSpec((1, PROJ_ROWS, D), next_block),
            full((1, D)), full((D, D_IN)), full((2, HG_WIDTH)), full((1, HG_DIM)),
        ],
        out_specs=[
            pl.BlockSpec((1, SEQ_TILE, HG_WIDTH), lambda b, s: (b, s, 0)),
            pl.BlockSpec((1, SEQ_TILE, 2 * LRU_WIDTH), lambda b, s: (b, s, 0)),
        ],
        out_shape=[
            jax.ShapeDtypeStruct((B, S, HG_WIDTH), bf),
            jax.ShapeDtypeStruct((B, S, 2 * LRU_WIDTH), jnp.float32),
        ],
        scratch_shapes=[
            pltpu.VMEM((PROJ_ROWS, D_IN), jnp.float32),
            pltpu.VMEM((PROJ_ROWS, 4 * HG_WIDTH), jnp.float32),
            pltpu.VMEM((HG_HEADS, HG_DIM, HG_DIM), jnp.float32),
        ],
        compiler_params=pltpu.CompilerParams(
            dimension_semantics=("arbitrary", "arbitrary"), vmem_limit_bytes=VMEM_LIMIT),
        name="hgrn",
    )(x, x, row(mix_norm_w[0]), win, hg_lb, row(hg_norm_w[0]))

    wgu = w_gate_up[0].astype(bf)
    wd = w_down[0].astype(bf)
    tiles_per_seq = S // FFN_TILE
    n_tiles = B * tiles_per_seq

    def cur(t):
        c = jnp.maximum(t - 1, 0)
        return (c // tiles_per_seq, c % tiles_per_seq, 0)

    def nxt(t):
        n = jnp.minimum(t, n_tiles - 1)
        return (n // tiles_per_seq, n % tiles_per_seq, 0)

    full1 = lambda shape: pl.BlockSpec(shape, lambda t: (0,) * len(shape), pipeline_mode=once)
    out = pl.pallas_call(
        functools.partial(_lru_ffn_kernel, tiles_per_seq=tiles_per_seq),
        grid=(n_tiles + 1,),
        in_specs=[
            pl.BlockSpec((1, FFN_TILE, D), cur),
            pl.BlockSpec((1, FFN_TILE, HG_WIDTH), cur),
            pl.BlockSpec((1, FFN_TILE, 2 * LRU_WIDTH), nxt),
            full1((CONV_WIDTH, LRU_WIDTH)), full1((1, LRU_WIDTH)),
            full1((LRU_WIDTH // LRU_GROUP, LRU_GROUP, 2 * LRU_GROUP)), full1((2, LRU_WIDTH)), full1((1, LRU_WIDTH)),
            full1((D_MIX, D)), full1((1, D)), full1((D, 2 * d_ff)), full1((d_ff, D)), full1((1, D)),
        ],
        out_specs=pl.BlockSpec((1, FFN_TILE, D), cur),
        out_shape=jax.ShapeDtypeStruct((B, S, D), jnp.float32),
        scratch_shapes=[
            pltpu.VMEM((N_LS, SLAB_ROWS, LANES), jnp.float32),
            pltpu.VMEM((N_LS, SLAB_ROWS, LANES), jnp.float32),
            pltpu.VMEM((FFN_TILE, LRU_WIDTH), jnp.bfloat16),
            pltpu.VMEM((1, LRU_WIDTH), jnp.float32),
            pltpu.VMEM((FFN_TILE, D), jnp.float32),
        ],
        compiler_params=pltpu.CompilerParams(
            dimension_semantics=("arbitrary",), vmem_limit_bytes=VMEM_LIMIT),
        name="lru_ffn",
    )(x, o_hg, lru_in, conv_w[0], row(conv_b[0]), wg, bg, row(lru_a[0]), wout, row(ffn_norm_w[0]), wgu, wd,
      row(final_norm_w))
    return out
```

```python
import functools

import jax
import jax.numpy as jnp
from jax import lax
from jax.experimental import pallas as pl
from jax.experimental.pallas import tpu as pltpu

D_MODEL = 1024
HG_HEADS = 4
HG_DIM = 128
HG_WIDTH = HG_HEADS * HG_DIM
HG_BASE = 32
LRU_WIDTH = 512
LRU_BLOCKS = 8
LRU_GROUP = 256
CONV_WIDTH = 4
LRU_C = 8.0
D_MIX = HG_WIDTH + LRU_WIDTH
D_IN = 4 * HG_WIDTH + 2 * LRU_WIDTH
EPS = 1e-6
LOG2E = 1.4426950408889634

SUB = 128
PROJ_ROWS = 512
SEQ_TILE = 1024
FFN_TILE = 512
FF_CHUNK = 256
COL_CHUNK = 512
LANES = 128
ROW_TILE = 8
N_SEG = ROW_TILE
SEG = SUB // N_SEG
TAIL = ROW_TILE
SEG_PITCH = TAIL + SEG
SLAB_ROWS = SEG_PITCH * (FFN_TILE // SEG)
N_LS = LRU_WIDTH // LANES
HGRN_YIELDS = 3
VMEM_LIMIT = 56 * 1024 * 1024

_NT = (((1,), (1,)), ((), ()))


def _dot(a, b, dims=None):
    if dims is None:
        return jnp.dot(a, b, preferred_element_type=jnp.float32)
    return lax.dot_general(a, b, dims, preferred_element_type=jnp.float32)


def _rms(x, w):
    return x * lax.rsqrt(jnp.mean(x * x, axis=-1, keepdims=True) + EPS) * w


def _rows(shape):
    return lax.broadcasted_iota(jnp.int32, shape, 0)


def _bcast_rows(r, n):
    return jnp.broadcast_to(r, (n, r.shape[1]))


def _hgrn_kernel(x_ref, xnext_ref, nw_ref, win_ref, lb_ref, hgw_ref, ohg_ref, lru_ref,
                 proj0_ref, proj1_ref, state_ref):
    bi = pl.program_id(0)
    si = pl.program_id(1)
    proj_refs = (proj0_ref, proj1_ref)

    @pl.when(si == 0)
    def _():
        state_ref[...] = jnp.zeros_like(state_ref)

    nw = nw_ref[...]
    n_hg = 4 * HG_WIDTH

    lbp = lb_ref[...]
    lbe = jnp.exp(lbp - jnp.max(lbp, axis=0, keepdims=True))
    lb = lbe[0:1] / jnp.sum(lbe, axis=0, keepdims=True)
    hgw = hgw_ref[...]
    tri_r = _rows((SUB, 2 * SUB))
    tri_c = lax.broadcasted_iota(jnp.int32, (SUB, 2 * SUB), 1) % SUB
    tri2 = jnp.where(tri_c <= tri_r, 1.0, 0.0).astype(jnp.bfloat16)

    row = _rows((SUB, SUB))
    col = lax.broadcasted_iota(jnp.int32, (SUB, SUB), 1)
    mask0 = (row // HG_BASE == col // HG_BASE) & (col <= row)
    mask1 = ((row // (2 * HG_BASE) == col // (2 * HG_BASE))
             & (row % (2 * HG_BASE) >= HG_BASE) & (col % (2 * HG_BASE) < HG_BASE))

    def in_proj(load_x, parity, store_lru):
        proj_ref = proj_refs[parity]
        cell = []

        def norm():
            cell.append(_rms(load_x(), nw).astype(jnp.bfloat16))

        def hg_chunk(c):
            proj_ref[:, c:c + COL_CHUNK] = _dot(cell[0], win_ref[:, c:c + COL_CHUNK])

        def lru_chunk(c):
            store_lru(c - n_hg, _dot(cell[0], win_ref[:, c:c + COL_CHUNK]))

        return ([norm] + [functools.partial(hg_chunk, c) for c in range(0, n_hg, COL_CHUNK)]
                + [functools.partial(lru_chunk, c) for c in range(n_hg, D_IN, COL_CHUNK)])

    def sub_tile(blk, j):
        proj_ref = proj_refs[blk % 2]
        p0 = j * SUB
        r0 = blk * PROJ_ROWS + p0
        half = SUB // 2
        q = proj_ref[p0:p0 + SUB, 0:HG_WIDTH]
        fl = proj_ref[p0:p0 + SUB, HG_WIDTH:2 * HG_WIDTH]
        v = proj_ref[p0:p0 + SUB, 2 * HG_WIDTH:3 * HG_WIDTH].astype(jnp.bfloat16)
        q = q * jax.nn.sigmoid(q)
        f = lb + (1.0 - lb) * jax.nn.sigmoid(fl)
        k = 1.0 - f
        lf = jnp.log(f) * LOG2E
        lf_hi = lf.astype(jnp.bfloat16)
        lf_lo = (lf - lf_hi.astype(jnp.float32)).astype(jnp.bfloat16)
        yield
        beta = _dot(tri2, jnp.concatenate([lf_hi, lf_lo], axis=0))
        r31 = beta[HG_BASE - 1:HG_BASE]
        r63 = beta[2 * HG_BASE - 1:2 * HG_BASE]
        r95 = beta[3 * HG_BASE - 1:3 * HG_BASE]
        r127 = beta[4 * HG_BASE - 1:4 * HG_BASE]
        ref0 = jnp.concatenate([jnp.zeros((HG_BASE, HG_WIDTH), jnp.float32), _bcast_rows(r31, HG_BASE),
                                _bcast_rows(r63, HG_BASE), _bcast_rows(r95, HG_BASE)], axis=0)
        ref1 = jnp.concatenate([_bcast_rows(r31, 2 * HG_BASE), _bcast_rows(r95, 2 * HG_BASE)], axis=0)
        b0 = beta - ref0
        q0 = (q * jnp.exp2(b0)).astype(jnp.bfloat16)
        k0 = (k * jnp.exp2(-b0)).astype(jnp.bfloat16)
        k1 = (k * jnp.exp2(jnp.minimum(ref1 - beta, 0.0))).astype(jnp.bfloat16)
        q2 = (q[half:] * jnp.exp2(beta[half:] - r63)).astype(jnp.bfloat16)
        k2 = jnp.concatenate([(k[:half] * jnp.exp2(r63 - beta[:half])).astype(jnp.bfloat16),
                              jnp.zeros((half, HG_WIDTH), jnp.bfloat16)], axis=0)
        qs = (q * jnp.exp2(beta)).astype(jnp.bfloat16)
        ks = (k * jnp.exp2(r127 - beta)).astype(jnp.bfloat16)
        dec_end = jnp.exp2(r127)
        yield
        heads = [slice(h * HG_DIM, (h + 1) * HG_DIM) for h in range(HG_HEADS)]
        s01 = [_dot(q0[:, hs], jnp.concatenate([k0[:, hs], k1[:, hs]], axis=0), _NT) for hs in heads]
        s2 = [_dot(q2[:, hs], k2[:, hs], _NT) for hs in heads]
        vt = [v[:, hs].T for hs in heads]
        inc = [_dot(vt[h], ks[:, hs]) for h, hs in enumerate(heads)]
        yield
        g = proj_ref[p0:p0 + SUB, 3 * HG_WIDTH:4 * HG_WIDTH]
        gate = g * jax.nn.sigmoid(g)
        for h, hs in enumerate(heads):
            sc = jnp.where(mask0, s01[h][:, :SUB], 0.0) + jnp.where(mask1, s01[h][:, SUB:], 0.0)
            sc = jnp.concatenate([sc[:half], sc[half:] + s2[h]], axis=0)
            st = state_ref[h]
            o = _dot(jnp.concatenate([sc.astype(jnp.bfloat16), qs[:, hs]], axis=1),
                     jnp.concatenate([vt[h], st.astype(jnp.bfloat16)], axis=1), _NT)
            state_ref[h] = st * dec_end[:, hs] + inc[h]
            o = o * lax.rsqrt(jnp.mean(o * o, axis=-1, keepdims=True) + EPS) * hgw
            ohg_ref[0, r0:r0 + SUB, hs] = (o * gate[:, hs]).astype(jnp.bfloat16)

    def lru_to_scratch(off, val):
        proj0_ref[:, n_hg + off:n_hg + off + COL_CHUNK] = val

    def lru_to_output(off, val):
        lru_ref[0, PROJ_ROWS:2 * PROJ_ROWS, off:off + COL_CHUNK] = val

    def flush_first_block():
        lru_ref[0, 0:PROJ_ROWS, :] = proj0_ref[:, n_hg:D_IN]

    @pl.when((bi == 0) & (si == 0))
    def _():
        for step in in_proj(lambda: x_ref[0, 0:PROJ_ROWS, :], 0, lru_to_scratch):
            step()

    fills = (
        [flush_first_block] + in_proj(lambda: x_ref[0, PROJ_ROWS:2 * PROJ_ROWS, :], 1, lru_to_output),
        in_proj(lambda: xnext_ref[0], 0, lru_to_scratch),
    )
    for blk, fill in enumerate(fills):
        points = [sub_tile(blk, j) for j in range(PROJ_ROWS // SUB)]
        n_points = len(points) * HGRN_YIELDS
        seen = done = 0
        for gen in points:
            for _ in gen:
                seen += 1
                upto = seen * len(fill) // n_points
                for step in fill[done:upto]:
                    step()
                done = upto
        assert seen == n_points and done == len(fill)


def _lru_ffn_kernel(x_ref, ohg_ref, lru_ref, cw_ref, cb_ref, wg_ref, bg_ref, ap_ref, wout_ref, nw_ref, wgu_ref,
                    wd_ref, fw_ref, o_ref, xl_ref, hs_ref, mlru_ref, hcar_ref, acc_ref, h_ref, hn_ref, *,
                    tiles_per_seq):
    t = pl.program_id(0)
    seq_start = (t % tiles_per_seq) == 0
    d_ff = wd_ref.shape[0]
    n_ff = d_ff // FF_CHUNK
    last_rows = slice(SLAB_ROWS - TAIL, SLAB_ROWS)

    @pl.when(t == 0)
    def _():
        h_ref[...] = jnp.zeros_like(h_ref)
        hn_ref[...] = jnp.zeros_like(hn_ref)
        hcar_ref[...] = jnp.zeros_like(hcar_ref)
        xl_ref[:, last_rows, :] = jnp.zeros((N_LS, TAIL, LANES), jnp.float32)

    neg_c_sp2 = (-LRU_C * LOG2E) * jax.nn.softplus(-ap_ref[...])
    cw_rows = [jnp.broadcast_to(cw_ref[k:k + 1, :], (N_SEG, LRU_WIDTH)) for k in range(CONV_WIDTH)]
    cb_rows = jnp.broadcast_to(cb_ref[...], (N_SEG, LRU_WIDTH))
    bg = bg_ref[...]

    def store_slabs():
        val = lru_ref[0, :, 0:LRU_WIDTH]
        n_seg = FFN_TILE // SEG
        for ls in range(N_LS):
            lanes = slice(ls * LANES, (ls + 1) * LANES)
            xl_ref[ls, 0:TAIL, :] = jnp.where(seq_start, 0.0, xl_ref[ls, last_rows, :])
            for s in range(n_seg):
                xl_ref[ls, SEG_PITCH * s + TAIL:SEG_PITCH * (s + 1), :] = val[SEG * s:SEG * (s + 1), lanes]
                if s + 1 < n_seg:
                    xl_ref[ls, SEG_PITCH * (s + 1):SEG_PITCH * (s + 1) + TAIL, :] = (
                        val[SEG * (s + 1) - TAIL:SEG * (s + 1), lanes])

    conv_out = {}

    def conv(j):
        slab_start = SEG_PITCH * j * N_SEG + TAIL

        def x_slab(r):
            return jnp.concatenate([xl_ref[ls, pl.ds(slab_start + r, N_SEG, stride=SEG_PITCH), :]
                                    for ls in range(N_LS)], axis=1)

        xs = {r: x_slab(r) for r in range(1 - CONV_WIDTH, SEG)}
        xc_slabs = []
        for r in range(SEG):
            acc = cb_rows
            for tap in range(CONV_WIDTH):
                acc = acc + xs[r - (CONV_WIDTH - 1 - tap)] * cw_rows[tap]
            xc_slabs.append(acc)
        conv_out[j] = jnp.concatenate(xc_slabs, axis=0)

    def recur(j):
        xc = conv_out[j]
        xcb = xc.astype(jnp.bfloat16)
        gates = jnp.concatenate(
            [_dot(xcb[:, gi * LRU_GROUP:(gi + 1) * LRU_GROUP], wg_ref[gi]) for gi in range(LRU_WIDTH // LRU_GROUP)],
            axis=1)
        rr = jnp.concatenate([gates[:, 0:256], gates[:, 512:768]], axis=1)
        ii = jnp.concatenate([gates[:, 256:512], gates[:, 768:1024]], axis=1)
        rr = jax.nn.sigmoid(rr + bg[0:1])
        ii = jax.nn.sigmoid(ii + bg[1:2])
        a = jnp.exp2(rr * neg_c_sp2)
        b_in = jnp.sqrt(1.0 - a * a) * (ii * xc)
        h_loc, a_loc = [b_in[0:N_SEG]], [a[0:N_SEG]]
        for r in range(1, SEG):
            rows = slice(r * N_SEG, (r + 1) * N_SEG)
            h_loc.append(a[rows] * h_loc[-1] + b_in[rows])
            a_loc.append(a[rows] * a_loc[-1])
        h_in = hcar_ref[...]
        if j == 0:
            h_in = jnp.where(seq_start, 0.0, h_in)
        carries = [h_in]
        for s in range(N_SEG):
            carries.append(a_loc[-1][s:s + 1] * carries[-1] + h_loc[-1][s:s + 1])
        hcar_ref[...] = carries[-1]
        carry = jnp.concatenate(carries[:-1], axis=0)
        hs0 = j * N_SEG * SEG_PITCH
        for r in range(SEG):
            h_r = h_loc[r] + a_loc[r] * carry
            for ls in range(N_LS):
                hs_ref[ls, pl.ds(hs0 + TAIL + r, N_SEG, stride=SEG_PITCH), :] = h_r[:, ls * LANES:(ls + 1) * LANES]
        hh = jnp.concatenate(
            [jnp.concatenate([hs_ref[ls, hs0 + SEG_PITCH * s + TAIL:hs0 + SEG_PITCH * (s + 1), :]
                              for s in range(N_SEG)], axis=0) for ls in range(N_LS)], axis=1)
        gl = lru_ref[0, j * SUB:(j + 1) * SUB, LRU_WIDTH:]
        mlru_ref[j * SUB:(j + 1) * SUB, :] = (hh * jax.nn.gelu(gl, approximate=True)).astype(jnp.bfloat16)

    pieces = [store_slabs]
    for j in range(FFN_TILE // SUB):
        pieces += [functools.partial(conv, j), functools.partial(recur, j)]

    def out_proj():
        mixed = jnp.concatenate([ohg_ref[0], mlru_ref[...]], axis=1)
        h_ref[...] = x_ref[0] + _dot(mixed, wout_ref[...])

    def ffn_norm():
        hn_ref[...] = _rms(h_ref[...], nw_ref[...]).astype(jnp.bfloat16)

    pieces += [out_proj, ffn_norm]
    assert len(pieces) == n_ff

    def gate_up(c):
        gt = _dot(hn_ref[...], wgu_ref[:, c * FF_CHUNK:(c + 1) * FF_CHUNK])
        up = _dot(hn_ref[...], wgu_ref[:, d_ff + c * FF_CHUNK:d_ff + (c + 1) * FF_CHUNK])
        return gt, up

    def activate(gt, up):
        return (gt * jax.nn.sigmoid(gt) * up).astype(jnp.bfloat16)

    act = activate(*gate_up(0))
    for c in range(n_ff):
        gu = gate_up(c + 1) if c + 1 < n_ff else None
        down = _dot(act, wd_ref[c * FF_CHUNK:(c + 1) * FF_CHUNK, :])
        if c == 0:
            acc_ref[...] = h_ref[...] + down
        else:
            acc_ref[...] += down
        pieces[c]()
        if gu is not None:
            act = activate(*gu)
    o_ref[0] = _rms(acc_ref[...], fw_ref[...])


def _block_diag_groups(w):
    per = LRU_GROUP // (LRU_WIDTH // LRU_BLOCKS)
    bd = LRU_WIDTH // LRU_BLOCKS
    w = w.reshape(LRU_WIDTH // LRU_GROUP, per, bd, bd)
    eye = jnp.eye(per, dtype=w.dtype)
    return jnp.einsum('gade,ab->gadbe', w, eye).reshape(LRU_WIDTH // LRU_GROUP, LRU_GROUP, LRU_GROUP)


def kernel(x, mix_norm_w, w_in, hg_lb, hg_norm_w, conv_w, conv_b, lru_wa, lru_ba, lru_wx, lru_bx,
           lru_a, w_out, ffn_norm_w, w_gate_up, w_down, final_norm_w):
    B, S, D = x.shape
    assert D == D_MODEL and w_in.shape == (1, D_MODEL, D_IN) and S % SEQ_TILE == 0 and SEQ_TILE == 2 * PROJ_ROWS
    bf = jnp.bfloat16
    d_ff = w_down.shape[1]
    assert d_ff % FF_CHUNK == 0

    win = w_in[0].astype(bf)
    wout = w_out[0].astype(bf)
    wg = jnp.concatenate([_block_diag_groups(lru_wa[0]), _block_diag_groups(lru_wx[0])], axis=-1).astype(bf)
    bg = jnp.stack([lru_ba[0], lru_bx[0]], axis=0)
    row = lambda a: a.reshape(1, -1)
    once = pl.Buffered(1)

    full = lambda shape: pl.BlockSpec(shape, lambda b, s: (0,) * len(shape), pipeline_mode=once)
    n_s = S // SEQ_TILE

    def next_block(b, s):
        wrap = s == n_s - 1
        return (jnp.minimum(jnp.where(wrap, b + 1, b), B - 1), jnp.where(wrap, 0, s + 1) * (SEQ_TILE // PROJ_ROWS), 0)

    o_hg, lru_in = pl.pallas_call(
        _hgrn_kernel,
        grid=(B, n_s),
        in_specs=[
            pl.BlockSpec((1, SEQ_TILE, D), lambda b, s: (b, s, 0)),
            pl.BlockSpec((1, PROJ_ROWS, D), next_block),
            full((1, D)), full((D, D_IN)), full((2, HG_WIDTH)), full((1, HG_DIM)),
        ],
        out_specs=[
            pl.BlockSpec((1, SEQ_TILE, HG_WIDTH), lambda b, s: (b, s, 0)),
            pl.BlockSpec((1, SEQ_TILE, 2 * LRU_WIDTH), lambda b, s: (b, s, 0)),
        ],
        out_shape=[
            jax.ShapeDtypeStruct((B, S, HG_WIDTH), bf),
            jax.ShapeDtypeStruct((B, S, 2 * LRU_WIDTH), jnp.float32),
        ],
        scratch_shapes=[
            pltpu.VMEM((PROJ_ROWS, D_IN), jnp.float32),
            pltpu.VMEM((PROJ_ROWS, 4 * HG_WIDTH), jnp.float32),
            pltpu.VMEM((HG_HEADS, HG_DIM, HG_DIM), jnp.float32),
        ],
        compiler_params=pltpu.CompilerParams(
            dimension_semantics=("arbitrary", "arbitrary"), vmem_limit_bytes=VMEM_LIMIT),
        name="hgrn",
    )(x, x, row(mix_norm_w[0]), win, hg_lb, row(hg_norm_w[0]))

    wgu = w_gate_up[0].astype(bf)
    wd = w_down[0].astype(bf)
    tiles_per_seq = S // FFN_TILE
    n_tiles = B * tiles_per_seq

    def cur(t):
        c = jnp.maximum(t - 1, 0)
        return (c // tiles_per_seq, c % tiles_per_seq, 0)

    def nxt(t):
        n = jnp.minimum(t, n_tiles - 1)
        return (n // tiles_per_seq, n % tiles_per_seq, 0)

    full1 = lambda shape: pl.BlockSpec(shape, lambda t: (0,) * len(shape), pipeline_mode=once)
    out = pl.pallas_call(
        functools.partial(_lru_ffn_kernel, tiles_per_seq=tiles_per_seq),
        grid=(n_tiles + 1,),
        in_specs=[
            pl.BlockSpec((1, FFN_TILE, D), nxt),
            pl.BlockSpec((1, FFN_TILE, HG_WIDTH), nxt),
            pl.BlockSpec((1, FFN_TILE, 2 * LRU_WIDTH), nxt),
            full1((CONV_WIDTH, LRU_WIDTH)), full1((1, LRU_WIDTH)),
            full1((LRU_WIDTH // LRU_GROUP, LRU_GROUP, 2 * LRU_GROUP)), full1((2, LRU_WIDTH)), full1((1, LRU_WIDTH)),
            full1((D_MIX, D)), full1((1, D)), full1((D, 2 * d_ff)), full1((d_ff, D)), full1((1, D)),
        ],
        out_specs=pl.BlockSpec((1, FFN_TILE, D), cur),
        out_shape=jax.ShapeDtypeStruct((B, S, D), jnp.float32),
        scratch_shapes=[
            pltpu.VMEM((N_LS, SLAB_ROWS, LANES), jnp.float32),
            pltpu.VMEM((N_LS, SLAB_ROWS, LANES), jnp.float32),
            pltpu.VMEM((FFN_TILE, LRU_WIDTH), jnp.bfloat16),
            pltpu.VMEM((1, LRU_WIDTH), jnp.float32),
            pltpu.VMEM((FFN_TILE, D), jnp.float32),
            pltpu.VMEM((FFN_TILE, D), jnp.float32),
            pltpu.VMEM((FFN_TILE, D), jnp.bfloat16),
        ],
        compiler_params=pltpu.CompilerParams(
            dimension_semantics=("arbitrary",), vmem_limit_bytes=VMEM_LIMIT),
        name="lru_ffn",
    )(x, o_hg, lru_in, conv_w[0], row(conv_b[0]), wg, bg, row(lru_a[0]), wout, row(ffn_norm_w[0]), wgu, wd,
      row(final_norm_w))
    return out
```

```python
import functools

import jax
import jax.numpy as jnp
from jax import lax
from jax.experimental import pallas as pl
from jax.experimental.pallas import tpu as pltpu

D_MODEL = 1024
HG_HEADS = 4
HG_DIM = 128
HG_WIDTH = HG_HEADS * HG_DIM
HG_BASE = 32
LRU_WIDTH = 512
LRU_BLOCKS = 8
LRU_GROUP = 256
CONV_WIDTH = 4
LRU_C = 8.0
D_MIX = HG_WIDTH + LRU_WIDTH
D_IN = 4 * HG_WIDTH + 2 * LRU_WIDTH
EPS = 1e-6
LOG2E = 1.4426950408889634

SUB = 128
PROJ_ROWS = 512
SEQ_TILE = 1024
FFN_TILE = 512
FF_CHUNK = 256
COL_CHUNK = 256
LANES = 128
ROW_TILE = 8
N_SEG = ROW_TILE
SEG = SUB // N_SEG
TAIL = ROW_TILE
SEG_PITCH = TAIL + SEG
SLAB_ROWS = SEG_PITCH * (FFN_TILE // SEG)
N_LS = LRU_WIDTH // LANES
HGRN_YIELDS = 5
VMEM_LIMIT = 56 * 1024 * 1024

_NT = (((1,), (1,)), ((), ()))


def _dot(a, b, dims=None):
    if dims is None:
        return jnp.dot(a, b, preferred_element_type=jnp.float32)
    return lax.dot_general(a, b, dims, preferred_element_type=jnp.float32)


def _rms(x, w):
    return x * lax.rsqrt(jnp.mean(x * x, axis=-1, keepdims=True) + EPS) * w


def _rows(shape):
    return lax.broadcasted_iota(jnp.int32, shape, 0)


def _bcast_rows(r, n):
    return jnp.broadcast_to(r, (n, r.shape[1]))


def _hgrn_kernel(x_ref, xnext_ref, nw_ref, win_ref, lb_ref, hgw_ref, ohg_ref, lru_ref,
                 proj0_ref, proj1_ref, state_ref):
    bi = pl.program_id(0)
    si = pl.program_id(1)
    proj_refs = (proj0_ref, proj1_ref)

    @pl.when(si == 0)
    def _():
        state_ref[...] = jnp.zeros_like(state_ref)

    nw = nw_ref[...]
    n_hg = 4 * HG_WIDTH

    lbp = lb_ref[...]
    lbe = jnp.exp(lbp - jnp.max(lbp, axis=0, keepdims=True))
    lb = lbe[0:1] / jnp.sum(lbe, axis=0, keepdims=True)
    hgw = hgw_ref[...]
    tri_r = _rows((SUB, 2 * SUB))
    tri_c = lax.broadcasted_iota(jnp.int32, (SUB, 2 * SUB), 1) % SUB
    tri2 = jnp.where(tri_c <= tri_r, 1.0, 0.0).astype(jnp.bfloat16)

    row = _rows((SUB, SUB))
    col = lax.broadcasted_iota(jnp.int32, (SUB, SUB), 1)
    mask0 = (row // HG_BASE == col // HG_BASE) & (col <= row)
    mask1 = ((row // (2 * HG_BASE) == col // (2 * HG_BASE))
             & (row % (2 * HG_BASE) >= HG_BASE) & (col % (2 * HG_BASE) < HG_BASE))

    def in_proj(load_x, parity, store_lru):
        proj_ref = proj_refs[parity]
        cell = []

        def norm():
            cell.append(_rms(load_x(), nw).astype(jnp.bfloat16))

        def hg_chunk(c):
            proj_ref[:, c:c + COL_CHUNK] = _dot(cell[0], win_ref[:, c:c + COL_CHUNK])

        def lru_chunk(c):
            store_lru(c - n_hg, _dot(cell[0], win_ref[:, c:c + COL_CHUNK]))

        return ([norm] + [functools.partial(hg_chunk, c) for c in range(0, n_hg, COL_CHUNK)]
                + [functools.partial(lru_chunk, c) for c in range(n_hg, D_IN, COL_CHUNK)])

    def sub_tile(blk, j):
        proj_ref = proj_refs[blk % 2]
        p0 = j * SUB
        r0 = blk * PROJ_ROWS + p0
        half = SUB // 2
        q = proj_ref[p0:p0 + SUB, 0:HG_WIDTH]
        fl = proj_ref[p0:p0 + SUB, HG_WIDTH:2 * HG_WIDTH]
        v = proj_ref[p0:p0 + SUB, 2 * HG_WIDTH:3 * HG_WIDTH].astype(jnp.bfloat16)
        q = q * jax.nn.sigmoid(q)
        f = lb + (1.0 - lb) * jax.nn.sigmoid(fl)
        k = 1.0 - f
        lf = jnp.log(f) * LOG2E
        lf_hi = lf.astype(jnp.bfloat16)
        lf_lo = (lf - lf_hi.astype(jnp.float32)).astype(jnp.bfloat16)
        yield
        beta = _dot(tri2, jnp.concatenate([lf_hi, lf_lo], axis=0))
        yield
        r31 = beta[HG_BASE - 1:HG_BASE]
        r63 = beta[2 * HG_BASE - 1:2 * HG_BASE]
        r95 = beta[3 * HG_BASE - 1:3 * HG_BASE]
        r127 = beta[4 * HG_BASE - 1:4 * HG_BASE]
        ref0 = jnp.concatenate([jnp.zeros((HG_BASE, HG_WIDTH), jnp.float32), _bcast_rows(r31, HG_BASE),
                                _bcast_rows(r63, HG_BASE), _bcast_rows(r95, HG_BASE)], axis=0)
        ref1 = jnp.concatenate([_bcast_rows(r31, 2 * HG_BASE), _bcast_rows(r95, 2 * HG_BASE)], axis=0)
        b0 = beta - ref0
        q0 = (q * jnp.exp2(b0)).astype(jnp.bfloat16)
        k0 = (k * jnp.exp2(-b0)).astype(jnp.bfloat16)
        k1 = (k * jnp.exp2(jnp.minimum(ref1 - beta, 0.0))).astype(jnp.bfloat16)
        yield
        q2 = (q[half:] * jnp.exp2(beta[half:] - r63)).astype(jnp.bfloat16)
        k2 = jnp.concatenate([(k[:half] * jnp.exp2(r63 - beta[:half])).astype(jnp.bfloat16),
                              jnp.zeros((half, HG_WIDTH), jnp.bfloat16)], axis=0)
        qs = (q * jnp.exp2(beta)).astype(jnp.bfloat16)
        ks = (k * jnp.exp2(r127 - beta)).astype(jnp.bfloat16)
        dec_end = jnp.exp2(r127)
        yield
        heads = [slice(h * HG_DIM, (h + 1) * HG_DIM) for h in range(HG_HEADS)]
        s01 = [_dot(q0[:, hs], jnp.concatenate([k0[:, hs], k1[:, hs]], axis=0), _NT) for hs in heads]
        s2 = [_dot(q2[:, hs], k2[:, hs], _NT) for hs in heads]
        vt = [v[:, hs].T for hs in heads]
        inc = [_dot(vt[h], ks[:, hs]) for h, hs in enumerate(heads)]
        yield
        g = proj_ref[p0:p0 + SUB, 3 * HG_WIDTH:4 * HG_WIDTH]
        gate = g * jax.nn.sigmoid(g)
        for h, hs in enumerate(heads):
            sc = jnp.where(mask0, s01[h][:, :SUB], 0.0) + jnp.where(mask1, s01[h][:, SUB:], 0.0)
            sc = jnp.concatenate([sc[:half], sc[half:] + s2[h]], axis=0)
            st = state_ref[h]
            o = _dot(jnp.concatenate([sc.astype(jnp.bfloat16), qs[:, hs]], axis=1),
                     jnp.concatenate([vt[h], st.astype(jnp.bfloat16)], axis=1), _NT)
            state_ref[h] = st * dec_end[:, hs] + inc[h]
            o = o * lax.rsqrt(jnp.mean(o * o, axis=-1, keepdims=True) + EPS) * hgw
            ohg_ref[0, r0:r0 + SUB, hs] = (o * gate[:, hs]).astype(jnp.bfloat16)

    def lru_to_scratch(off, val):
        proj0_ref[:, n_hg + off:n_hg + off + COL_CHUNK] = val

    def lru_to_output(off, val):
        lru_ref[0, PROJ_ROWS:2 * PROJ_ROWS, off:off + COL_CHUNK] = val

    def flush_first_block():
        lru_ref[0, 0:PROJ_ROWS, :] = proj0_ref[:, n_hg:D_IN]

    @pl.when((bi == 0) & (si == 0))
    def _():
        for step in in_proj(lambda: x_ref[0, 0:PROJ_ROWS, :], 0, lru_to_scratch):
            step()

    fills = (
        [flush_first_block] + in_proj(lambda: x_ref[0, PROJ_ROWS:2 * PROJ_ROWS, :], 1, lru_to_output),
        in_proj(lambda: xnext_ref[0], 0, lru_to_scratch),
    )
    for blk, fill in enumerate(fills):
        points = [sub_tile(blk, j) for j in range(PROJ_ROWS // SUB)]
        n_points = len(points) * HGRN_YIELDS
        seen = done = 0
        for gen in points:
            for _ in gen:
                seen += 1
                upto = seen * len(fill) // n_points
                for step in fill[done:upto]:
                    step()
                done = upto
        assert seen == n_points and done == len(fill)


def _lru_ffn_kernel(x_ref, ohg_ref, lru_ref, cw_ref, cb_ref, wg_ref, bg_ref, ap_ref, wout_ref, nw_ref, wgu_ref,
                    wd_ref, fw_ref, o_ref, xl_ref, hs_ref, mlru_ref, hcar_ref, acc_ref, h_ref, hn_ref, *,
                    tiles_per_seq):
    t = pl.program_id(0)
    seq_start = (t % tiles_per_seq) == 0
    d_ff = wd_ref.shape[0]
    n_ff = d_ff // FF_CHUNK
    last_rows = slice(SLAB_ROWS - TAIL, SLAB_ROWS)

    @pl.when(t == 0)
    def _():
        h_ref[...] = jnp.zeros_like(h_ref)
        hn_ref[...] = jnp.zeros_like(hn_ref)
        hcar_ref[...] = jnp.zeros_like(hcar_ref)
        xl_ref[:, last_rows, :] = jnp.zeros((N_LS, TAIL, LANES), jnp.float32)

    neg_c_sp2 = (-LRU_C * LOG2E) * jax.nn.softplus(-ap_ref[...])
    cw_rows = [jnp.broadcast_to(cw_ref[k:k + 1, :], (N_SEG, LRU_WIDTH)) for k in range(CONV_WIDTH)]
    cb_rows = jnp.broadcast_to(cb_ref[...], (N_SEG, LRU_WIDTH))
    bg = bg_ref[...]

    def store_slabs():
        val = lru_ref[0, :, 0:LRU_WIDTH]
        n_seg = FFN_TILE // SEG
        for ls in range(N_LS):
            lanes = slice(ls * LANES, (ls + 1) * LANES)
            xl_ref[ls, 0:TAIL, :] = jnp.where(seq_start, 0.0, xl_ref[ls, last_rows, :])
            for s in range(n_seg):
                xl_ref[ls, SEG_PITCH * s + TAIL:SEG_PITCH * (s + 1), :] = val[SEG * s:SEG * (s + 1), lanes]
                if s + 1 < n_seg:
                    xl_ref[ls, SEG_PITCH * (s + 1):SEG_PITCH * (s + 1) + TAIL, :] = (
                        val[SEG * (s + 1) - TAIL:SEG * (s + 1), lanes])

    conv_out = {}

    def conv(j):
        slab_start = SEG_PITCH * j * N_SEG + TAIL

        def x_slab(r):
            return jnp.concatenate([xl_ref[ls, pl.ds(slab_start + r, N_SEG, stride=SEG_PITCH), :]
                                    for ls in range(N_LS)], axis=1)

        xs = {r: x_slab(r) for r in range(1 - CONV_WIDTH, SEG)}
        xc_slabs = []
        for r in range(SEG):
            acc = cb_rows
            for tap in range(CONV_WIDTH):
                acc = acc + xs[r - (CONV_WIDTH - 1 - tap)] * cw_rows[tap]
            xc_slabs.append(acc)
        conv_out[j] = jnp.concatenate(xc_slabs, axis=0)

    def recur(j):
        xc = conv_out[j]
        xcb = xc.astype(jnp.bfloat16)
        gates = jnp.concatenate(
            [_dot(xcb[:, gi * LRU_GROUP:(gi + 1) * LRU_GROUP], wg_ref[gi]) for gi in range(LRU_WIDTH // LRU_GROUP)],
            axis=1)
        rr = jnp.concatenate([gates[:, 0:256], gates[:, 512:768]], axis=1)
        ii = jnp.concatenate([gates[:, 256:512], gates[:, 768:1024]], axis=1)
        rr = jax.nn.sigmoid(rr + bg[0:1])
        ii = jax.nn.sigmoid(ii + bg[1:2])
        a = jnp.exp2(rr * neg_c_sp2)
        b_in = jnp.sqrt(1.0 - a * a) * (ii * xc)
        h_loc, a_loc = [b_in[0:N_SEG]], [a[0:N_SEG]]
        for r in range(1, SEG):
            rows = slice(r * N_SEG, (r + 1) * N_SEG)
            h_loc.append(a[rows] * h_loc[-1] + b_in[rows])
            a_loc.append(a[rows] * a_loc[-1])
        h_in = hcar_ref[...]
        if j == 0:
            h_in = jnp.where(seq_start, 0.0, h_in)
        carries = [h_in]
        for s in range(N_SEG):
            carries.append(a_loc[-1][s:s + 1] * carries[-1] + h_loc[-1][s:s + 1])
        hcar_ref[...] = carries[-1]
        carry = jnp.concatenate(carries[:-1], axis=0)
        hs0 = j * N_SEG * SEG_PITCH
        for r in range(SEG):
            h_r = h_loc[r] + a_loc[r] * carry
            for ls in range(N_LS):
                hs_ref[ls, pl.ds(hs0 + TAIL + r, N_SEG, stride=SEG_PITCH), :] = h_r[:, ls * LANES:(ls + 1) * LANES]
        hh = jnp.concatenate(
            [jnp.concatenate([hs_ref[ls, hs0 + SEG_PITCH * s + TAIL:hs0 + SEG_PITCH * (s + 1), :]
                              for s in range(N_SEG)], axis=0) for ls in range(N_LS)], axis=1)
        gl = lru_ref[0, j * SUB:(j + 1) * SUB, LRU_WIDTH:]
        mlru_ref[j * SUB:(j + 1) * SUB, :] = (hh * jax.nn.gelu(gl, approximate=True)).astype(jnp.bfloat16)

    pieces = [store_slabs]
    for j in range(FFN_TILE // SUB):
        pieces += [functools.partial(conv, j), functools.partial(recur, j)]

    def out_proj():
        mixed = jnp.concatenate([ohg_ref[0], mlru_ref[...]], axis=1)
        h_ref[...] = x_ref[0] + _dot(mixed, wout_ref[...])

    def ffn_norm():
        hn_ref[...] = _rms(h_ref[...], nw_ref[...]).astype(jnp.bfloat16)

    pieces += [out_proj, ffn_norm]
    assert len(pieces) == n_ff

    def gate_up(c):
        gt = _dot(hn_ref[...], wgu_ref[:, c * FF_CHUNK:(c + 1) * FF_CHUNK])
        up = _dot(hn_ref[...], wgu_ref[:, d_ff + c * FF_CHUNK:d_ff + (c + 1) * FF_CHUNK])
        return gt, up

    def activate(gt, up):
        return (gt * jax.nn.sigmoid(gt) * up).astype(jnp.bfloat16)

    act = activate(*gate_up(0))
    for c in range(n_ff):
        gu = gate_up(c + 1) if c + 1 < n_ff else None
        down = _dot(act, wd_ref[c * FF_CHUNK:(c + 1) * FF_CHUNK, :])
        if c == 0:
            acc_ref[...] = h_ref[...] + down
        else:
            acc_ref[...] += down
        pieces[c]()
        if gu is not None:
            act = activate(*gu)
    o_ref[0] = _rms(acc_ref[...], fw_ref[...])


def _block_diag_groups(w):
    per = LRU_GROUP // (LRU_WIDTH // LRU_BLOCKS)
    bd = LRU_WIDTH // LRU_BLOCKS
    w = w.reshape(LRU_WIDTH // LRU_GROUP, per, bd, bd)
    eye = jnp.eye(per, dtype=w.dtype)
    return jnp.einsum('gade,ab->gadbe', w, eye).reshape(LRU_WIDTH // LRU_GROUP, LRU_GROUP, LRU_GROUP)


def kernel(x, mix_norm_w, w_in, hg_lb, hg_norm_w, conv_w, conv_b, lru_wa, lru_ba, lru_wx, lru_bx,
           lru_a, w_out, ffn_norm_w, w_gate_up, w_down, final_norm_w):
    B, S, D = x.shape
    assert D == D_MODEL and w_in.shape == (1, D_MODEL, D_IN) and S % SEQ_TILE == 0 and SEQ_TILE == 2 * PROJ_ROWS
    bf = jnp.bfloat16
    d_ff = w_down.shape[1]
    assert d_ff % FF_CHUNK == 0

    win = w_in[0].astype(bf)
    wout = w_out[0].astype(bf)
    wg = jnp.concatenate([_block_diag_groups(lru_wa[0]), _block_diag_groups(lru_wx[0])], axis=-1).astype(bf)
    bg = jnp.stack([lru_ba[0], lru_bx[0]], axis=0)
    row = lambda a: a.reshape(1, -1)
    once = pl.Buffered(1)

    full = lambda shape: pl.BlockSpec(shape, lambda b, s: (0,) * len(shape), pipeline_mode=once)
    n_s = S // SEQ_TILE

    def next_block(b, s):
        wrap = s == n_s - 1
        return (jnp.minimum(jnp.where(wrap, b + 1, b), B - 1), jnp.where(wrap, 0, s + 1) * (SEQ_TILE // PROJ_ROWS), 0)

    o_hg, lru_in = pl.pallas_call(
        _hgrn_kernel,
        grid=(B, n_s),
        in_specs=[
            pl.BlockSpec((1, SEQ_TILE, D), lambda b, s: (b, s, 0)),
            pl.BlockSpec((1, PROJ_ROWS, D), next_block),
            full((1, D)), full((D, D_IN)), full((2, HG_WIDTH)), full((1, HG_DIM)),
        ],
        out_specs=[
            pl.BlockSpec((1, SEQ_TILE, HG_WIDTH), lambda b, s: (b, s, 0)),
            pl.BlockSpec((1, SEQ_TILE, 2 * LRU_WIDTH), lambda b, s: (b, s, 0)),
        ],
        out_shape=[
            jax.ShapeDtypeStruct((B, S, HG_WIDTH), bf),
            jax.ShapeDtypeStruct((B, S, 2 * LRU_WIDTH), jnp.float32),
        ],
        scratch_shapes=[
            pltpu.VMEM((PROJ_ROWS, D_IN), jnp.float32),
            pltpu.VMEM((PROJ_ROWS, 4 * HG_WIDTH), jnp.float32),
            pltpu.VMEM((HG_HEADS, HG_DIM, HG_DIM), jnp.float32),
        ],
        compiler_params=pltpu.CompilerParams(
            dimension_semantics=("arbitrary", "arbitrary"), vmem_limit_bytes=VMEM_LIMIT),
        name="hgrn",
    )(x, x, row(mix_norm_w[0]), win, hg_lb, row(hg_norm_w[0]))

    wgu = w_gate_up[0].astype(bf)
    wd = w_down[0].astype(bf)
    tiles_per_seq = S // FFN_TILE
    n_tiles = B * tiles_per_seq

    def cur(t):
        c = jnp.maximum(t - 1, 0)
        return (c // tiles_per_seq, c % tiles_per_seq, 0)

    def nxt(t):
        n = jnp.minimum(t, n_tiles - 1)
        return (n // tiles_per_seq, n % tiles_per_seq, 0)

    full1 = lambda shape: pl.BlockSpec(shape, lambda t: (0,) * len(shape), pipeline_mode=once)
    out = pl.pallas_call(
        functools.partial(_lru_ffn_kernel, tiles_per_seq=tiles_per_seq),
        grid=(n_tiles + 1,),
        in_specs=[
            pl.BlockSpec((1, FFN_TILE, D), nxt),
            pl.BlockSpec((1, FFN_TILE, HG_WIDTH), nxt),
            pl.BlockSpec((1, FFN_TILE, 2 * LRU_WIDTH), nxt),
            full1((CONV_WIDTH, LRU_WIDTH)), full1((1, LRU_WIDTH)),
            full1((LRU_WIDTH // LRU_GROUP, LRU_GROUP, 2 * LRU_GROUP)), full1((2, LRU_WIDTH)), full1((1, LRU_WIDTH)),
            full1((D_MIX, D)), full1((1, D)), full1((D, 2 * d_ff)), full1((d_ff, D)), full1((1, D)),
        ],
        out_specs=pl.BlockSpec((1, FFN_TILE, D), cur),
        out_shape=jax.ShapeDtypeStruct((B, S, D), jnp.float32),
        scratch_shapes=[
            pltpu.VMEM((N_LS, SLAB_ROWS, LANES), jnp.float32),
            pltpu.VMEM((N_LS, SLAB_ROWS, LANES), jnp.float32),
            pltpu.VMEM((FFN_TILE, LRU_WIDTH), jnp.bfloat16),
            pltpu.VMEM((1, LRU_WIDTH), jnp.float32),
            pltpu.VMEM((FFN_TILE, D), jnp.float32),
            pltpu.VMEM((FFN_TILE, D), jnp.float32),
            pltpu.VMEM((FFN_TILE, D), jnp.bfloat16),
        ],
        compiler_params=pltpu.CompilerParams(
            dimension_semantics=("arbitrary",), vmem_limit_bytes=VMEM_LIMIT),
        name="lru_ffn",
    )(x, o_hg, lru_in, conv_w[0], row(conv_b[0]), wg, bg, row(lru_a[0]), wout, row(ffn_norm_w[0]), wgu, wd,
      row(final_norm_w))
    return out
```

```python
import functools

import jax
import jax.numpy as jnp
from jax import lax
from jax.experimental import pallas as pl
from jax.experimental.pallas import tpu as pltpu

D_MODEL = 1024
HG_HEADS = 4
HG_DIM = 128
HG_WIDTH = HG_HEADS * HG_DIM
HG_BASE = 32
LRU_WIDTH = 512
LRU_BLOCKS = 8
LRU_GROUP = 256
CONV_WIDTH = 4
LRU_C = 8.0
D_MIX = HG_WIDTH + LRU_WIDTH
D_IN = 4 * HG_WIDTH + 2 * LRU_WIDTH
EPS = 1e-6
LOG2E = 1.4426950408889634

SUB = 128
PROJ_ROWS = 512
SEQ_TILE = 1024
FFN_TILE = 512
FF_CHUNK = 256
COL_CHUNK = 256
LANES = 128
ROW_TILE = 8
N_SEG = ROW_TILE
SEG = SUB // N_SEG
TAIL = ROW_TILE
SEG_PITCH = TAIL + SEG
SLAB_ROWS = SEG_PITCH * (FFN_TILE // SEG)
N_LS = LRU_WIDTH // LANES
HGRN_YIELDS = 5
VMEM_LIMIT = 56 * 1024 * 1024

_NT = (((1,), (1,)), ((), ()))


def _dot(a, b, dims=None):
    if dims is None:
        return jnp.dot(a, b, preferred_element_type=jnp.float32)
    return lax.dot_general(a, b, dims, preferred_element_type=jnp.float32)


def _rms(x, w):
    return x * lax.rsqrt(jnp.mean(x * x, axis=-1, keepdims=True) + EPS) * w


def _rows(shape):
    return lax.broadcasted_iota(jnp.int32, shape, 0)


def _bcast_rows(r, n):
    return jnp.broadcast_to(r, (n, r.shape[1]))


def _hgrn_kernel(x_ref, xnext_ref, nw_ref, win_ref, lb_ref, hgw_ref, ohg_ref, lru_ref,
                 proj0_ref, proj1_ref, state_ref):
    bi = pl.program_id(0)
    si = pl.program_id(1)
    proj_refs = (proj0_ref, proj1_ref)

    @pl.when(si == 0)
    def _():
        state_ref[...] = jnp.zeros_like(state_ref)

    nw = nw_ref[...]
    n_hg = 4 * HG_WIDTH

    lbp = lb_ref[...]
    lbe = jnp.exp(lbp - jnp.max(lbp, axis=0, keepdims=True))
    lb = lbe[0:1] / jnp.sum(lbe, axis=0, keepdims=True)
    hgw = hgw_ref[...]
    tri_r = _rows((SUB, 2 * SUB))
    tri_c = lax.broadcasted_iota(jnp.int32, (SUB, 2 * SUB), 1) % SUB
    tri2 = jnp.where(tri_c <= tri_r, 1.0, 0.0).astype(jnp.bfloat16)

    row = _rows((SUB, SUB))
    col = lax.broadcasted_iota(jnp.int32, (SUB, SUB), 1)
    mask0 = (row // HG_BASE == col // HG_BASE) & (col <= row)
    mask1 = ((row // (2 * HG_BASE) == col // (2 * HG_BASE))
             & (row % (2 * HG_BASE) >= HG_BASE) & (col % (2 * HG_BASE) < HG_BASE))

    def in_proj(load_x, parity, store_lru):
        proj_ref = proj_refs[parity]
        cell = []

        def norm():
            cell.append(_rms(load_x(), nw).astype(jnp.bfloat16))

        def hg_chunk(c):
            proj_ref[:, c:c + COL_CHUNK] = _dot(cell[0], win_ref[:, c:c + COL_CHUNK])

        def lru_chunk(c):
            store_lru(c - n_hg, _dot(cell[0], win_ref[:, c:c + COL_CHUNK]))

        return ([norm] + [functools.partial(hg_chunk, c) for c in range(0, n_hg, COL_CHUNK)]
                + [functools.partial(lru_chunk, c) for c in range(n_hg, D_IN, COL_CHUNK)])

    def sub_tile(blk, j):
        proj_ref = proj_refs[blk % 2]
        p0 = j * SUB
        r0 = blk * PROJ_ROWS + p0
        half = SUB // 2
        q = proj_ref[p0:p0 + SUB, 0:HG_WIDTH]
        fl = proj_ref[p0:p0 + SUB, HG_WIDTH:2 * HG_WIDTH]
        v = proj_ref[p0:p0 + SUB, 2 * HG_WIDTH:3 * HG_WIDTH].astype(jnp.bfloat16)
        q = q * jax.nn.sigmoid(q)
        f = lb + (1.0 - lb) * jax.nn.sigmoid(fl)
        k = 1.0 - f
        lf = jnp.log(f) * LOG2E
        lf_hi = lf.astype(jnp.bfloat16)
        lf_lo = (lf - lf_hi.astype(jnp.float32)).astype(jnp.bfloat16)
        yield
        beta = _dot(tri2, jnp.concatenate([lf_hi, lf_lo], axis=0))
        yield
        r31 = beta[HG_BASE - 1:HG_BASE]
        r63 = beta[2 * HG_BASE - 1:2 * HG_BASE]
        r95 = beta[3 * HG_BASE - 1:3 * HG_BASE]
        r127 = beta[4 * HG_BASE - 1:4 * HG_BASE]
        ref0 = jnp.concatenate([jnp.zeros((HG_BASE, HG_WIDTH), jnp.float32), _bcast_rows(r31, HG_BASE),
                                _bcast_rows(r63, HG_BASE), _bcast_rows(r95, HG_BASE)], axis=0)
        ref1 = jnp.concatenate([_bcast_rows(r31, 2 * HG_BASE), _bcast_rows(r95, 2 * HG_BASE)], axis=0)
        b0 = beta - ref0
        q0 = (q * jnp.exp2(b0)).astype(jnp.bfloat16)
        k0 = (k * jnp.exp2(-b0)).astype(jnp.bfloat16)
        k1 = (k * jnp.exp2(jnp.minimum(ref1 - beta, 0.0))).astype(jnp.bfloat16)
        yield
        q2 = (q[half:] * jnp.exp2(beta[half:] - r63)).astype(jnp.bfloat16)
        k2 = jnp.concatenate([(k[:half] * jnp.exp2(r63 - beta[:half])).astype(jnp.bfloat16),
                              jnp.zeros((half, HG_WIDTH), jnp.bfloat16)], axis=0)
        qs = (q * jnp.exp2(beta)).astype(jnp.bfloat16)
        ks = (k * jnp.exp2(r127 - beta)).astype(jnp.bfloat16)
        dec_end = jnp.exp2(r127)
        yield
        heads = [slice(h * HG_DIM, (h + 1) * HG_DIM) for h in range(HG_HEADS)]
        s01 = [_dot(q0[:, hs], jnp.concatenate([k0[:, hs], k1[:, hs]], axis=0), _NT) for hs in heads]
        s2 = [_dot(q2[:, hs], k2[:, hs], _NT) for hs in heads]
        vt = [v[:, hs].T for hs in heads]
        inc = [_dot(vt[h], ks[:, hs]) for h, hs in enumerate(heads)]
        yield
        g = proj_ref[p0:p0 + SUB, 3 * HG_WIDTH:4 * HG_WIDTH]
        gate = g * jax.nn.sigmoid(g)
        for h, hs in enumerate(heads):
            sc = jnp.where(mask0, s01[h][:, :SUB], 0.0) + jnp.where(mask1, s01[h][:, SUB:], 0.0)
            sc = jnp.concatenate([sc[:half], sc[half:] + s2[h]], axis=0)
            st = state_ref[h]
            o = _dot(jnp.concatenate([sc.astype(jnp.bfloat16), qs[:, hs]], axis=1),
                     jnp.concatenate([vt[h], st.astype(jnp.bfloat16)], axis=1), _NT)
            state_ref[h] = st * dec_end[:, hs] + inc[h]
            o = o * lax.rsqrt(jnp.mean(o * o, axis=-1, keepdims=True) + EPS) * hgw
            ohg_ref[0, r0:r0 + SUB, hs] = (o * gate[:, hs]).astype(jnp.bfloat16)

    def lru_to_scratch(off, val):
        proj0_ref[:, n_hg + off:n_hg + off + COL_CHUNK] = val

    def lru_to_output(off, val):
        lru_ref[0, PROJ_ROWS:2 * PROJ_ROWS, off:off + COL_CHUNK] = val

    def flush_first_block():
        lru_ref[0, 0:PROJ_ROWS, :] = proj0_ref[:, n_hg:D_IN]

    @pl.when((bi == 0) & (si == 0))
    def _():
        for step in in_proj(lambda: x_ref[0, 0:PROJ_ROWS, :], 0, lru_to_scratch):
            step()

    fills = (
        [flush_first_block] + in_proj(lambda: x_ref[0, PROJ_ROWS:2 * PROJ_ROWS, :], 1, lru_to_output),
        in_proj(lambda: xnext_ref[0], 0, lru_to_scratch),
    )
    for blk, fill in enumerate(fills):
        points = [sub_tile(blk, j) for j in range(PROJ_ROWS // SUB)]
        n_points = len(points) * HGRN_YIELDS
        seen = done = 0
        for gen in points:
            for _ in gen:
                seen += 1
                upto = -(-seen * len(fill) // n_points)
                for step in fill[done:upto]:
                    step()
                done = upto
        assert seen == n_points and done == len(fill)


def _lru_ffn_kernel(x_ref, ohg_ref, lru_ref, cw_ref, cb_ref, wg_ref, bg_ref, ap_ref, wout_ref, nw_ref, wgu_ref,
                    wd_ref, fw_ref, o_ref, xl_ref, hs_ref, mlru_ref, hcar_ref, acc_ref, h_ref, hn_ref, *,
                    tiles_per_seq):
    t = pl.program_id(0)
    seq_start = (t % tiles_per_seq) == 0
    d_ff = wd_ref.shape[0]
    n_ff = d_ff // FF_CHUNK
    last_rows = slice(SLAB_ROWS - TAIL, SLAB_ROWS)

    @pl.when(t == 0)
    def _():
        h_ref[...] = jnp.zeros_like(h_ref)
        hn_ref[...] = jnp.zeros_like(hn_ref)
        hcar_ref[...] = jnp.zeros_like(hcar_ref)
        xl_ref[:, last_rows, :] = jnp.zeros((N_LS, TAIL, LANES), jnp.float32)

    neg_c_sp2 = (-LRU_C * LOG2E) * jax.nn.softplus(-ap_ref[...])
    cw_rows = [jnp.broadcast_to(cw_ref[k:k + 1, :], (N_SEG, LRU_WIDTH)) for k in range(CONV_WIDTH)]
    cb_rows = jnp.broadcast_to(cb_ref[...], (N_SEG, LRU_WIDTH))
    bg = bg_ref[...]

    def store_slabs():
        val = lru_ref[0, :, 0:LRU_WIDTH]
        n_seg = FFN_TILE // SEG
        for ls in range(N_LS):
            lanes = slice(ls * LANES, (ls + 1) * LANES)
            xl_ref[ls, 0:TAIL, :] = jnp.where(seq_start, 0.0, xl_ref[ls, last_rows, :])
            for s in range(n_seg):
                xl_ref[ls, SEG_PITCH * s + TAIL:SEG_PITCH * (s + 1), :] = val[SEG * s:SEG * (s + 1), lanes]
                if s + 1 < n_seg:
                    xl_ref[ls, SEG_PITCH * (s + 1):SEG_PITCH * (s + 1) + TAIL, :] = (
                        val[SEG * (s + 1) - TAIL:SEG * (s + 1), lanes])

    conv_out = {}

    def conv(j):
        slab_start = SEG_PITCH * j * N_SEG + TAIL

        def x_slab(r):
            return jnp.concatenate([xl_ref[ls, pl.ds(slab_start + r, N_SEG, stride=SEG_PITCH), :]
                                    for ls in range(N_LS)], axis=1)

        xs = {r: x_slab(r) for r in range(1 - CONV_WIDTH, SEG)}
        xc_slabs = []
        for r in range(SEG):
            acc = cb_rows
            for tap in range(CONV_WIDTH):
                acc = acc + xs[r - (CONV_WIDTH - 1 - tap)] * cw_rows[tap]
            xc_slabs.append(acc)
        conv_out[j] = jnp.concatenate(xc_slabs, axis=0)

    def recur(j):
        xc = conv_out[j]
        xcb = xc.astype(jnp.bfloat16)
        gates = jnp.concatenate(
            [_dot(xcb[:, gi * LRU_GROUP:(gi + 1) * LRU_GROUP], wg_ref[gi]) for gi in range(LRU_WIDTH // LRU_GROUP)],
            axis=1)
        rr = jnp.concatenate([gates[:, 0:256], gates[:, 512:768]], axis=1)
        ii = jnp.concatenate([gates[:, 256:512], gates[:, 768:1024]], axis=1)
        rr = jax.nn.sigmoid(rr + bg[0:1])
        ii = jax.nn.sigmoid(ii + bg[1:2])
        a = jnp.exp2(rr * neg_c_sp2)
        b_in = jnp.sqrt(1.0 - a * a) * (ii * xc)
        h_loc, a_loc = [b_in[0:N_SEG]], [a[0:N_SEG]]
        for r in range(1, SEG):
            rows = slice(r * N_SEG, (r + 1) * N_SEG)
            h_loc.append(a[rows] * h_loc[-1] + b_in[rows])
            a_loc.append(a[rows] * a_loc[-1])
        h_in = hcar_ref[...]
        if j == 0:
            h_in = jnp.where(seq_start, 0.0, h_in)
        carries = [h_in]
        for s in range(N_SEG):
            carries.append(a_loc[-1][s:s + 1] * carries[-1] + h_loc[-1][s:s + 1])
        hcar_ref[...] = carries[-1]
        carry = jnp.concatenate(carries[:-1], axis=0)
        hs0 = j * N_SEG * SEG_PITCH
        for r in range(SEG):
            h_r = h_loc[r] + a_loc[r] * carry
            for ls in range(N_LS):
                hs_ref[ls, pl.ds(hs0 + TAIL + r, N_SEG, stride=SEG_PITCH), :] = h_r[:, ls * LANES:(ls + 1) * LANES]
        hh = jnp.concatenate(
            [jnp.concatenate([hs_ref[ls, hs0 + SEG_PITCH * s + TAIL:hs0 + SEG_PITCH * (s + 1), :]
                              for s in range(N_SEG)], axis=0) for ls in range(N_LS)], axis=1)
        gl = lru_ref[0, j * SUB:(j + 1) * SUB, LRU_WIDTH:]
        mlru_ref[j * SUB:(j + 1) * SUB, :] = (hh * jax.nn.gelu(gl, approximate=True)).astype(jnp.bfloat16)

    def first_piece():
        store_slabs()
        conv(0)

    pieces = [first_piece, functools.partial(recur, 0)]
    for j in range(1, FFN_TILE // SUB):
        pieces += [functools.partial(conv, j), functools.partial(recur, j)]

    def out_proj():
        mixed = jnp.concatenate([ohg_ref[0], mlru_ref[...]], axis=1)
        h_ref[...] = x_ref[0] + _dot(mixed, wout_ref[...])

    def ffn_norm():
        hn_ref[...] = _rms(h_ref[...], nw_ref[...]).astype(jnp.bfloat16)

    pieces += [out_proj, ffn_norm]
    assert len(pieces) == n_ff - 1

    def gate_up(c):
        gt = _dot(hn_ref[...], wgu_ref[:, c * FF_CHUNK:(c + 1) * FF_CHUNK])
        up = _dot(hn_ref[...], wgu_ref[:, d_ff + c * FF_CHUNK:d_ff + (c + 1) * FF_CHUNK])
        return gt, up

    def activate(gt, up):
        return (gt * jax.nn.sigmoid(gt) * up).astype(jnp.bfloat16)

    act = activate(*gate_up(0))
    for c in range(n_ff - 1):
        gu = gate_up(c + 1)
        down = _dot(act, wd_ref[c * FF_CHUNK:(c + 1) * FF_CHUNK, :])
        if c == 0:
            acc_ref[...] = h_ref[...] + down
        else:
            acc_ref[...] += down
        pieces[c]()
        act = activate(*gu)
    wd_last = wd_ref[(n_ff - 1) * FF_CHUNK:n_ff * FF_CHUNK, :]
    halves = [slice(i * (FFN_TILE // 2), (i + 1) * (FFN_TILE // 2)) for i in range(2)]
    for rows in halves:
        acc_ref[rows, :] += _dot(act[rows], wd_last)
    for rows in halves:
        o_ref[0, rows, :] = _rms(acc_ref[rows, :], fw_ref[...])


def _block_diag_groups(w):
    per = LRU_GROUP // (LRU_WIDTH // LRU_BLOCKS)
    bd = LRU_WIDTH // LRU_BLOCKS
    w = w.reshape(LRU_WIDTH // LRU_GROUP, per, bd, bd)
    eye = jnp.eye(per, dtype=w.dtype)
    return jnp.einsum('gade,ab->gadbe', w, eye).reshape(LRU_WIDTH // LRU_GROUP, LRU_GROUP, LRU_GROUP)


def kernel(x, mix_norm_w, w_in, hg_lb, hg_norm_w, conv_w, conv_b, lru_wa, lru_ba, lru_wx, lru_bx,
           lru_a, w_out, ffn_norm_w, w_gate_up, w_down, final_norm_w):
    B, S, D = x.shape
    assert D == D_MODEL and w_in.shape == (1, D_MODEL, D_IN) and S % SEQ_TILE == 0 and SEQ_TILE == 2 * PROJ_ROWS
    bf = jnp.bfloat16
    d_ff = w_down.shape[1]
    assert d_ff % FF_CHUNK == 0

    win = w_in[0].astype(bf)
    wout = w_out[0].astype(bf)
    wg = jnp.concatenate([_block_diag_groups(lru_wa[0]), _block_diag_groups(lru_wx[0])], axis=-1).astype(bf)
    bg = jnp.stack([lru_ba[0], lru_bx[0]], axis=0)
    row = lambda a: a.reshape(1, -1)
    once = pl.Buffered(1)

    full = lambda shape: pl.BlockSpec(shape, lambda b, s: (0,) * len(shape), pipeline_mode=once)
    n_s = S // SEQ_TILE

    def next_block(b, s):
        wrap = s == n_s - 1
        return (jnp.minimum(jnp.where(wrap, b + 1, b), B - 1), jnp.where(wrap, 0, s + 1) * (SEQ_TILE // PROJ_ROWS), 0)

    o_hg, lru_in = pl.pallas_call(
        _hgrn_kernel,
        grid=(B, n_s),
        in_specs=[
            pl.BlockSpec((1, SEQ_TILE, D), lambda b, s: (b, s, 0)),
            pl.BlockSpec((1, PROJ_ROWS, D), next_block),
            full((1, D)), full((D, D_IN)), full((2, HG_WIDTH)), full((1, HG_DIM)),
        ],
        out_specs=[
            pl.BlockSpec((1, SEQ_TILE, HG_WIDTH), lambda b, s: (b, s, 0)),
            pl.BlockSpec((1, SEQ_TILE, 2 * LRU_WIDTH), lambda b, s: (b, s, 0)),
        ],
        out_shape=[
            jax.ShapeDtypeStruct((B, S, HG_WIDTH), bf),
            jax.ShapeDtypeStruct((B, S, 2 * LRU_WIDTH), jnp.float32),
        ],
        scratch_shapes=[
            pltpu.VMEM((PROJ_ROWS, D_IN), jnp.float32),
            pltpu.VMEM((PROJ_ROWS, 4 * HG_WIDTH), jnp.float32),
            pltpu.VMEM((HG_HEADS, HG_DIM, HG_DIM), jnp.float32),
        ],
        compiler_params=pltpu.CompilerParams(
            dimension_semantics=("arbitrary", "arbitrary"), vmem_limit_bytes=VMEM_LIMIT),
        name="hgrn",
    )(x, x, row(mix_norm_w[0]), win, hg_lb, row(hg_norm_w[0]))

    wgu = w_gate_up[0].astype(bf)
    wd = w_down[0].astype(bf)
    tiles_per_seq = S // FFN_TILE
    n_tiles = B * tiles_per_seq

    def cur(t):
        c = jnp.maximum(t - 1, 0)
        return (c // tiles_per_seq, c % tiles_per_seq, 0)

    def nxt(t):
        n = jnp.minimum(t, n_tiles - 1)
        return (n // tiles_per_seq, n % tiles_per_seq, 0)

    full1 = lambda shape: pl.BlockSpec(shape, lambda t: (0,) * len(shape), pipeline_mode=once)
    out = pl.pallas_call(
        functools.partial(_lru_ffn_kernel, tiles_per_seq=tiles_per_seq),
        grid=(n_tiles + 1,),
        in_specs=[
            pl.BlockSpec((1, FFN_TILE, D), nxt),
            pl.BlockSpec((1, FFN_TILE, HG_WIDTH), nxt),
            pl.BlockSpec((1, FFN_TILE, 2 * LRU_WIDTH), nxt),
            full1((CONV_WIDTH, LRU_WIDTH)), full1((1, LRU_WIDTH)),
            full1((LRU_WIDTH // LRU_GROUP, LRU_GROUP, 2 * LRU_GROUP)), full1((2, LRU_WIDTH)), full1((1, LRU_WIDTH)),
            full1((D_MIX, D)), full1((1, D)), full1((D, 2 * d_ff)), full1((d_ff, D)), full1((1, D)),
        ],
        out_specs=pl.BlockSpec((1, FFN_TILE, D), cur),
        out_shape=jax.ShapeDtypeStruct((B, S, D), jnp.float32),
        scratch_shapes=[
            pltpu.VMEM((N_LS, SLAB_ROWS, LANES), jnp.float32),
            pltpu.VMEM((N_LS, SLAB_ROWS, LANES), jnp.float32),
            pltpu.VMEM((FFN_TILE, LRU_WIDTH), jnp.bfloat16),
            pltpu.VMEM((1, LRU_WIDTH), jnp.float32),
            pltpu.VMEM((FFN_TILE, D), jnp.float32),
            pltpu.VMEM((FFN_TILE, D), jnp.float32),
            pltpu.VMEM((FFN_TILE, D), jnp.bfloat16),
        ],
        compiler_params=pltpu.CompilerParams(
            dimension_semantics=("arbitrary",), vmem_limit_bytes=VMEM_LIMIT),
        name="lru_ffn",
    )(x, o_hg, lru_in, conv_w[0], row(conv_b[0]), wg, bg, row(lru_a[0]), wout, row(ffn_norm_w[0]), wgu, wd,
      row(final_norm_w))
    return out
```

```python
import functools

import jax
import jax.numpy as jnp
from jax import lax
from jax.experimental import pallas as pl
from jax.experimental.pallas import tpu as pltpu

D_MODEL = 1024
HG_HEADS = 4
HG_DIM = 128
HG_WIDTH = HG_HEADS * HG_DIM
HG_BASE = 32
LRU_WIDTH = 512
LRU_BLOCKS = 8
LRU_GROUP = 256
CONV_WIDTH = 4
LRU_C = 8.0
D_MIX = HG_WIDTH + LRU_WIDTH
D_IN = 4 * HG_WIDTH + 2 * LRU_WIDTH
EPS = 1e-6
LOG2E = 1.4426950408889634

SUB = 128
PROJ_ROWS = 512
SEQ_TILE = 1024
FFN_TILE = 512
FF_CHUNK = 256
COL_CHUNK = 256
LANES = 128
ROW_TILE = 8
N_SEG = ROW_TILE
SEG = SUB // N_SEG
TAIL = ROW_TILE
SEG_PITCH = TAIL + SEG
SLAB_ROWS = SEG_PITCH * (FFN_TILE // SEG)
N_LS = LRU_WIDTH // LANES
HGRN_YIELDS = 5
VMEM_LIMIT = 56 * 1024 * 1024

_NT = (((1,), (1,)), ((), ()))


def _dot(a, b, dims=None):
    if dims is None:
        return jnp.dot(a, b, preferred_element_type=jnp.float32)
    return lax.dot_general(a, b, dims, preferred_element_type=jnp.float32)


def _rms(x, w):
    return x * lax.rsqrt(jnp.mean(x * x, axis=-1, keepdims=True) + EPS) * w


def _rows(shape):
    return lax.broadcasted_iota(jnp.int32, shape, 0)


def _bcast_rows(r, n):
    return jnp.broadcast_to(r, (n, r.shape[1]))


def _hgrn_kernel(x_ref, xnext_ref, nw_ref, win_ref, lb_ref, hgw_ref, ohg_ref, lru_ref,
                 proj0_ref, proj1_ref, state_ref):
    bi = pl.program_id(0)
    si = pl.program_id(1)
    proj_refs = (proj0_ref, proj1_ref)

    @pl.when(si == 0)
    def _():
        state_ref[...] = jnp.zeros_like(state_ref)

    nw = nw_ref[...]
    n_hg = 4 * HG_WIDTH

    lbp = lb_ref[...]
    lbe = jnp.exp(lbp - jnp.max(lbp, axis=0, keepdims=True))
    lb = lbe[0:1] / jnp.sum(lbe, axis=0, keepdims=True)
    hgw = hgw_ref[...]
    tri_r = _rows((SUB, 2 * SUB))
    tri_c = lax.broadcasted_iota(jnp.int32, (SUB, 2 * SUB), 1) % SUB
    tri2 = jnp.where(tri_c <= tri_r, 1.0, 0.0).astype(jnp.bfloat16)

    row = _rows((SUB, SUB))
    col = lax.broadcasted_iota(jnp.int32, (SUB, SUB), 1)
    mask0 = (row // HG_BASE == col // HG_BASE) & (col <= row)
    mask1 = ((row // (2 * HG_BASE) == col // (2 * HG_BASE))
             & (row % (2 * HG_BASE) >= HG_BASE) & (col % (2 * HG_BASE) < HG_BASE))

    def in_proj(load_x, parity, store_lru):
        proj_ref = proj_refs[parity]
        cell = []

        def norm():
            cell.append(_rms(load_x(), nw).astype(jnp.bfloat16))

        def hg_chunk(c):
            proj_ref[:, c:c + COL_CHUNK] = _dot(cell[0], win_ref[:, c:c + COL_CHUNK])

        def lru_chunk(c):
            store_lru(c - n_hg, _dot(cell[0], win_ref[:, c:c + COL_CHUNK]))

        return ([norm] + [functools.partial(hg_chunk, c) for c in range(0, n_hg, COL_CHUNK)]
                + [functools.partial(lru_chunk, c) for c in range(n_hg, D_IN, COL_CHUNK)])

    def sub_tile(blk, j):
        proj_ref = proj_refs[blk % 2]
        p0 = j * SUB
        r0 = blk * PROJ_ROWS + p0
        half = SUB // 2
        q = proj_ref[p0:p0 + SUB, 0:HG_WIDTH]
        fl = proj_ref[p0:p0 + SUB, HG_WIDTH:2 * HG_WIDTH]
        v = proj_ref[p0:p0 + SUB, 2 * HG_WIDTH:3 * HG_WIDTH].astype(jnp.bfloat16)
        q = q * jax.nn.sigmoid(q)
        f = lb + (1.0 - lb) * jax.nn.sigmoid(fl)
        k = 1.0 - f
        lf = jnp.log(f) * LOG2E
        lf_hi = lf.astype(jnp.bfloat16)
        lf_lo = (lf - lf_hi.astype(jnp.float32)).astype(jnp.bfloat16)
        yield
        beta = _dot(tri2, jnp.concatenate([lf_hi, lf_lo], axis=0))
        yield
        r31 = beta[HG_BASE - 1:HG_BASE]
        r63 = beta[2 * HG_BASE - 1:2 * HG_BASE]
        r95 = beta[3 * HG_BASE - 1:3 * HG_BASE]
        r127 = beta[4 * HG_BASE - 1:4 * HG_BASE]
        ref0 = jnp.concatenate([jnp.zeros((HG_BASE, HG_WIDTH), jnp.float32), _bcast_rows(r31, HG_BASE),
                                _bcast_rows(r63, HG_BASE), _bcast_rows(r95, HG_BASE)], axis=0)
        ref1 = jnp.concatenate([_bcast_rows(r31, 2 * HG_BASE), _bcast_rows(r95, 2 * HG_BASE)], axis=0)
        b0 = beta - ref0
        q0 = (q * jnp.exp2(b0)).astype(jnp.bfloat16)
        k0 = (k * jnp.exp2(-b0)).astype(jnp.bfloat16)
        k1 = (k * jnp.exp2(jnp.minimum(ref1 - beta, 0.0))).astype(jnp.bfloat16)
        yield
        q2 = (q[half:] * jnp.exp2(beta[half:] - r63)).astype(jnp.bfloat16)
        k2 = jnp.concatenate([(k[:half] * jnp.exp2(r63 - beta[:half])).astype(jnp.bfloat16),
                              jnp.zeros((half, HG_WIDTH), jnp.bfloat16)], axis=0)
        qs = (q * jnp.exp2(beta)).astype(jnp.bfloat16)
        ks = (k * jnp.exp2(r127 - beta)).astype(jnp.bfloat16)
        dec_end = jnp.exp2(r127)
        yield
        heads = [slice(h * HG_DIM, (h + 1) * HG_DIM) for h in range(HG_HEADS)]
        s01 = [_dot(q0[:, hs], jnp.concatenate([k0[:, hs], k1[:, hs]], axis=0), _NT) for hs in heads]
        s2 = [_dot(q2[:, hs], k2[:, hs], _NT) for hs in heads]
        vt = [v[:, hs].T for hs in heads]
        inc = [_dot(vt[h], ks[:, hs]) for h, hs in enumerate(heads)]
        yield
        g = proj_ref[p0:p0 + SUB, 3 * HG_WIDTH:4 * HG_WIDTH]
        gate = g * jax.nn.sigmoid(g)
        for h, hs in enumerate(heads):
            sc = jnp.where(mask0, s01[h][:, :SUB], 0.0) + jnp.where(mask1, s01[h][:, SUB:], 0.0)
            sc = jnp.concatenate([sc[:half], sc[half:] + s2[h]], axis=0)
            st = state_ref[h]
            o = _dot(jnp.concatenate([sc.astype(jnp.bfloat16), qs[:, hs]], axis=1),
                     jnp.concatenate([vt[h], st.astype(jnp.bfloat16)], axis=1), _NT)
            state_ref[h] = st * dec_end[:, hs] + inc[h]
            o = o * lax.rsqrt(jnp.mean(o * o, axis=-1, keepdims=True) + EPS) * hgw
            ohg_ref[0, r0:r0 + SUB, hs] = (o * gate[:, hs]).astype(jnp.bfloat16)

    def lru_to_scratch(off, val):
        proj0_ref[:, n_hg + off:n_hg + off + COL_CHUNK] = val

    def lru_to_output(off, val):
        lru_ref[0, PROJ_ROWS:2 * PROJ_ROWS, off:off + COL_CHUNK] = val

    def flush_first_block():
        lru_ref[0, 0:PROJ_ROWS, :] = proj0_ref[:, n_hg:D_IN]

    @pl.when((bi == 0) & (si == 0))
    def _():
        for step in in_proj(lambda: x_ref[0, 0:PROJ_ROWS, :], 0, lru_to_scratch):
            step()

    fills = (
        [flush_first_block] + in_proj(lambda: x_ref[0, PROJ_ROWS:2 * PROJ_ROWS, :], 1, lru_to_output),
        in_proj(lambda: xnext_ref[0], 0, lru_to_scratch),
    )
    for blk, fill in enumerate(fills):
        points = [sub_tile(blk, j) for j in range(PROJ_ROWS // SUB)]
        n_points = len(points) * HGRN_YIELDS
        seen = done = 0
        for gen in points:
            for _ in gen:
                seen += 1
                upto = -(-seen * len(fill) // n_points)
                for step in fill[done:upto]:
                    step()
                done = upto
        assert seen == n_points and done == len(fill)


def _lru_ffn_kernel(x_ref, ohg_ref, lru_ref, cw_ref, cb_ref, wg_ref, bg_ref, ap_ref, wout_ref, nw_ref, wgu_ref,
                    wd_ref, fw_ref, o_ref, xl_ref, hs_ref, mlru_ref, hcar_ref, acc_ref, h_ref, hn_ref, *,
                    tiles_per_seq):
    t = pl.program_id(0)
    seq_start = (t % tiles_per_seq) == 0
    d_ff = wd_ref.shape[0]
    n_ff = d_ff // FF_CHUNK
    last_rows = slice(SLAB_ROWS - TAIL, SLAB_ROWS)

    @pl.when(t == 0)
    def _():
        h_ref[...] = jnp.zeros_like(h_ref)
        hn_ref[...] = jnp.zeros_like(hn_ref)
        hcar_ref[...] = jnp.zeros_like(hcar_ref)
        xl_ref[:, last_rows, :] = jnp.zeros((N_LS, TAIL, LANES), jnp.float32)

    neg_c_sp2 = (-LRU_C * LOG2E) * jax.nn.softplus(-ap_ref[...])
    cw_rows = [jnp.broadcast_to(cw_ref[k:k + 1, :], (N_SEG, LRU_WIDTH)) for k in range(CONV_WIDTH)]
    cb_rows = jnp.broadcast_to(cb_ref[...], (N_SEG, LRU_WIDTH))
    bg = bg_ref[...]

    def store_slabs():
        val = lru_ref[0, :, 0:LRU_WIDTH]
        n_seg = FFN_TILE // SEG
        for ls in range(N_LS):
            lanes = slice(ls * LANES, (ls + 1) * LANES)
            xl_ref[ls, 0:TAIL, :] = jnp.where(seq_start, 0.0, xl_ref[ls, last_rows, :])
            for s in range(n_seg):
                xl_ref[ls, SEG_PITCH * s + TAIL:SEG_PITCH * (s + 1), :] = val[SEG * s:SEG * (s + 1), lanes]
                if s + 1 < n_seg:
                    xl_ref[ls, SEG_PITCH * (s + 1):SEG_PITCH * (s + 1) + TAIL, :] = (
                        val[SEG * (s + 1) - TAIL:SEG * (s + 1), lanes])

    conv_out = {}

    def conv(j):
        slab_start = SEG_PITCH * j * N_SEG + TAIL

        def x_slab(r):
            return jnp.concatenate([xl_ref[ls, pl.ds(slab_start + r, N_SEG, stride=SEG_PITCH), :]
                                    for ls in range(N_LS)], axis=1)

        xs = {r: x_slab(r) for r in range(1 - CONV_WIDTH, SEG)}
        xc_slabs = []
        for r in range(SEG):
            acc = cb_rows
            for tap in range(CONV_WIDTH):
                acc = acc + xs[r - (CONV_WIDTH - 1 - tap)] * cw_rows[tap]
            xc_slabs.append(acc)
        conv_out[j] = jnp.concatenate(xc_slabs, axis=0)

    def recur(j):
        xc = conv_out[j]
        xcb = xc.astype(jnp.bfloat16)
        gates = [_dot(xcb[:, gi * LRU_GROUP:(gi + 1) * LRU_GROUP], wg_ref[gi]) for gi in range(LRU_WIDTH // LRU_GROUP)]
        rr = jnp.concatenate([gt[:, :LRU_GROUP] for gt in gates], axis=1)
        ii = jnp.concatenate([gt[:, LRU_GROUP:] for gt in gates], axis=1)
        rr = jax.nn.sigmoid(rr + bg[0:1])
        ii = jax.nn.sigmoid(ii + bg[1:2])
        a = jnp.exp2(rr * neg_c_sp2)
        b_in = jnp.sqrt(1.0 - a * a) * (ii * xc)
        h_loc, a_loc = [b_in[0:N_SEG]], [a[0:N_SEG]]
        for r in range(1, SEG):
            rows = slice(r * N_SEG, (r + 1) * N_SEG)
            h_loc.append(a[rows] * h_loc[-1] + b_in[rows])
            a_loc.append(a[rows] * a_loc[-1])
        h_in = hcar_ref[...]
        if j == 0:
            h_in = jnp.where(seq_start, 0.0, h_in)
        carries = [h_in]
        for s in range(N_SEG):
            carries.append(a_loc[-1][s:s + 1] * carries[-1] + h_loc[-1][s:s + 1])
        hcar_ref[...] = carries[-1]
        carry = jnp.concatenate(carries[:-1], axis=0)
        hs0 = j * N_SEG * SEG_PITCH
        for r in range(SEG):
            h_r = h_loc[r] + a_loc[r] * carry
            for ls in range(N_LS):
                hs_ref[ls, pl.ds(hs0 + TAIL + r, N_SEG, stride=SEG_PITCH), :] = h_r[:, ls * LANES:(ls + 1) * LANES]
        hh = jnp.concatenate(
            [jnp.concatenate([hs_ref[ls, hs0 + SEG_PITCH * s + TAIL:hs0 + SEG_PITCH * (s + 1), :]
                              for s in range(N_SEG)], axis=0) for ls in range(N_LS)], axis=1)
        gl = lru_ref[0, j * SUB:(j + 1) * SUB, LRU_WIDTH:]
        mlru_ref[j * SUB:(j + 1) * SUB, :] = (hh * jax.nn.gelu(gl, approximate=True)).astype(jnp.bfloat16)

    def first_piece():
        store_slabs()
        conv(0)

    pieces = [first_piece, functools.partial(recur, 0)]
    for j in range(1, FFN_TILE // SUB):
        pieces += [functools.partial(conv, j), functools.partial(recur, j)]

    def out_proj():
        mixed = jnp.concatenate([ohg_ref[0], mlru_ref[...]], axis=1)
        h_ref[...] = x_ref[0] + _dot(mixed, wout_ref[...])

    def ffn_norm():
        hn_ref[...] = _rms(h_ref[...], nw_ref[...]).astype(jnp.bfloat16)

    pieces += [out_proj, ffn_norm]
    assert len(pieces) == n_ff - 1

    def gate_up(c):
        gt = _dot(hn_ref[...], wgu_ref[:, c * FF_CHUNK:(c + 1) * FF_CHUNK])
        up = _dot(hn_ref[...], wgu_ref[:, d_ff + c * FF_CHUNK:d_ff + (c + 1) * FF_CHUNK])
        return gt, up

    def activate(gt, up):
        return (gt * jax.nn.sigmoid(gt) * up).astype(jnp.bfloat16)

    act = activate(*gate_up(0))
    for c in range(n_ff - 1):
        gu = gate_up(c + 1)
        down = _dot(act, wd_ref[c * FF_CHUNK:(c + 1) * FF_CHUNK, :])
        if c == 0:
            acc_ref[...] = h_ref[...] + down
        else:
            acc_ref[...] += down
        pieces[c]()
        act = activate(*gu)
    wd_last = wd_ref[(n_ff - 1) * FF_CHUNK:n_ff * FF_CHUNK, :]
    halves = [slice(i * (FFN_TILE // 2), (i + 1) * (FFN_TILE // 2)) for i in range(2)]
    for rows in halves:
        acc_ref[rows, :] += _dot(act[rows], wd_last)
    for rows in halves:
        o_ref[0, rows, :] = _rms(acc_ref[rows, :], fw_ref[...])


def _block_diag_groups(w):
    per = LRU_GROUP // (LRU_WIDTH // LRU_BLOCKS)
    bd = LRU_WIDTH // LRU_BLOCKS
    w = w.reshape(LRU_WIDTH // LRU_GROUP, per, bd, bd)
    eye = jnp.eye(per, dtype=w.dtype)
    return jnp.einsum('gade,ab->gadbe', w, eye).reshape(LRU_WIDTH // LRU_GROUP, LRU_GROUP, LRU_GROUP)


def kernel(x, mix_norm_w, w_in, hg_lb, hg_norm_w, conv_w, conv_b, lru_wa, lru_ba, lru_wx, lru_bx,
           lru_a, w_out, ffn_norm_w, w_gate_up, w_down, final_norm_w):
    B, S, D = x.shape
    assert D == D_MODEL and w_in.shape == (1, D_MODEL, D_IN) and S % SEQ_TILE == 0 and SEQ_TILE == 2 * PROJ_ROWS
    bf = jnp.bfloat16
    d_ff = w_down.shape[1]
    assert d_ff % FF_CHUNK == 0

    win = w_in[0].astype(bf)
    wout = w_out[0].astype(bf)
    wg = jnp.concatenate([_block_diag_groups(lru_wa[0]), _block_diag_groups(lru_wx[0])], axis=-1).astype(bf)
    bg = jnp.stack([lru_ba[0], lru_bx[0]], axis=0)
    row = lambda a: a.reshape(1, -1)
    once = pl.Buffered(1)

    full = lambda shape: pl.BlockSpec(shape, lambda b, s: (0,) * len(shape), pipeline_mode=once)
    n_s = S // SEQ_TILE

    def next_block(b, s):
        wrap = s == n_s - 1
        return (jnp.minimum(jnp.where(wrap, b + 1, b), B - 1), jnp.where(wrap, 0, s + 1) * (SEQ_TILE // PROJ_ROWS), 0)

    o_hg, lru_in = pl.pallas_call(
        _hgrn_kernel,
        grid=(B, n_s),
        in_specs=[
            pl.BlockSpec((1, SEQ_TILE, D), lambda b, s: (b, s, 0)),
            pl.BlockSpec((1, PROJ_ROWS, D), next_block),
            full((1, D)), full((D, D_IN)), full((2, HG_WIDTH)), full((1, HG_DIM)),
        ],
        out_specs=[
            pl.BlockSpec((1, SEQ_TILE, HG_WIDTH), lambda b, s: (b, s, 0)),
            pl.BlockSpec((1, SEQ_TILE, 2 * LRU_WIDTH), lambda b, s: (b, s, 0)),
        ],
        out_shape=[
            jax.ShapeDtypeStruct((B, S, HG_WIDTH), bf),
            jax.ShapeDtypeStruct((B, S, 2 * LRU_WIDTH), jnp.float32),
        ],
        scratch_shapes=[
            pltpu.VMEM((PROJ_ROWS, D_IN), jnp.float32),
            pltpu.VMEM((PROJ_ROWS, 4 * HG_WIDTH), jnp.float32),
            pltpu.VMEM((HG_HEADS, HG_DIM, HG_DIM), jnp.float32),
        ],
        compiler_params=pltpu.CompilerParams(
            dimension_semantics=("arbitrary", "arbitrary"), vmem_limit_bytes=VMEM_LIMIT),
        name="hgrn",
    )(x, x, row(mix_norm_w[0]), win, hg_lb, row(hg_norm_w[0]))

    wgu = w_gate_up[0].astype(bf)
    wd = w_down[0].astype(bf)
    tiles_per_seq = S // FFN_TILE
    n_tiles = B * tiles_per_seq

    def cur(t):
        c = jnp.maximum(t - 1, 0)
        return (c // tiles_per_seq, c % tiles_per_seq, 0)

    def nxt(t):
        n = jnp.minimum(t, n_tiles - 1)
        return (n // tiles_per_seq, n % tiles_per_seq, 0)

    full1 = lambda shape: pl.BlockSpec(shape, lambda t: (0,) * len(shape), pipeline_mode=once)
    out = pl.pallas_call(
        functools.partial(_lru_ffn_kernel, tiles_per_seq=tiles_per_seq),
        grid=(n_tiles + 1,),
        in_specs=[
            pl.BlockSpec((1, FFN_TILE, D), nxt),
            pl.BlockSpec((1, FFN_TILE, HG_WIDTH), nxt),
            pl.BlockSpec((1, FFN_TILE, 2 * LRU_WIDTH), nxt),
            full1((CONV_WIDTH, LRU_WIDTH)), full1((1, LRU_WIDTH)),
            full1((LRU_WIDTH // LRU_GROUP, LRU_GROUP, 2 * LRU_GROUP)), full1((2, LRU_WIDTH)), full1((1, LRU_WIDTH)),
            full1((D_MIX, D)), full1((1, D)), full1((D, 2 * d_ff)), full1((d_ff, D)), full1((1, D)),
        ],
        out_specs=pl.BlockSpec((1, FFN_TILE, D), cur),
        out_shape=jax.ShapeDtypeStruct((B, S, D), jnp.float32),
        scratch_shapes=[
            pltpu.VMEM((N_LS, SLAB_ROWS, LANES), jnp.float32),
            pltpu.VMEM((N_LS, SLAB_ROWS, LANES), jnp.float32),
            pltpu.VMEM((FFN_TILE, LRU_WIDTH), jnp.bfloat16),
            pltpu.VMEM((1, LRU_WIDTH), jnp.float32),
            pltpu.VMEM((FFN_TILE, D), jnp.float32),
            pltpu.VMEM((FFN_TILE, D), jnp.float32),
            pltpu.VMEM((FFN_TILE, D), jnp.bfloat16),
        ],
        compiler_params=pltpu.CompilerParams(
            dimension_semantics=("arbitrary",), vmem_limit_bytes=VMEM_LIMIT),
        name="lru_ffn",
    )(x, o_hg, lru_in, conv_w[0], row(conv_b[0]), wg, bg, row(lru_a[0]), wout, row(ffn_norm_w[0]), wgu, wd,
      row(final_norm_w))
    return out
```

```python
import functools

import jax
import jax.numpy as jnp
from jax import lax
from jax.experimental import pallas as pl
from jax.experimental.pallas import tpu as pltpu

D_MODEL = 1024
HG_HEADS = 4
HG_DIM = 128
HG_WIDTH = HG_HEADS * HG_DIM
HG_BASE = 32
LRU_WIDTH = 512
LRU_BLOCKS = 8
LRU_GROUP = 256
CONV_WIDTH = 4
LRU_C = 8.0
D_MIX = HG_WIDTH + LRU_WIDTH
D_IN = 4 * HG_WIDTH + 2 * LRU_WIDTH
EPS = 1e-6
LOG2E = 1.4426950408889634

SUB = 128
PROJ_ROWS = 512
SEQ_TILE = 1024
FFN_TILE = 512
FF_CHUNK = 256
COL_CHUNK = 256
LANES = 128
ROW_TILE = 8
N_SEG = ROW_TILE
SEG = SUB // N_SEG
TAIL = ROW_TILE
SEG_PITCH = TAIL + SEG
SLAB_ROWS = SEG_PITCH * (FFN_TILE // SEG)
N_LS = LRU_WIDTH // LANES
PHASE_TRAIL = 2
HGRN_YIELDS = 5
VMEM_LIMIT = 56 * 1024 * 1024

_NT = (((1,), (1,)), ((), ()))


def _dot(a, b, dims=None):
    if dims is None:
        return jnp.dot(a, b, preferred_element_type=jnp.float32)
    return lax.dot_general(a, b, dims, preferred_element_type=jnp.float32)


def _rms(x, w):
    return x * lax.rsqrt(jnp.mean(x * x, axis=-1, keepdims=True) + EPS) * w


def _rows(shape):
    return lax.broadcasted_iota(jnp.int32, shape, 0)


def _bcast_rows(r, n):
    return jnp.broadcast_to(r, (n, r.shape[1]))


def _hgrn_kernel(x_ref, xnext_ref, nw_ref, win_ref, lb_ref, hgw_ref, ohg_ref, lru_ref,
                 proj0_ref, proj1_ref, xn1_ref, state_ref):
    bi = pl.program_id(0)
    si = pl.program_id(1)
    proj_refs = (proj0_ref, proj1_ref)

    @pl.when(si == 0)
    def _():
        state_ref[...] = jnp.zeros_like(state_ref)

    nw = nw_ref[...]
    n_hg = 4 * HG_WIDTH

    lbp = lb_ref[...]
    lbe = jnp.exp(lbp - jnp.max(lbp, axis=0, keepdims=True))
    lb = lbe[0:1] / jnp.sum(lbe, axis=0, keepdims=True)
    hgw = hgw_ref[...]
    tri_r = _rows((SUB, 2 * SUB))
    tri_c = lax.broadcasted_iota(jnp.int32, (SUB, 2 * SUB), 1) % SUB
    tri2 = jnp.where(tri_c <= tri_r, 1.0, 0.0).astype(jnp.bfloat16)

    row = _rows((SUB, SUB))
    col = lax.broadcasted_iota(jnp.int32, (SUB, SUB), 1)
    mask0 = (row // HG_BASE == col // HG_BASE) & (col <= row)
    mask1 = ((row // (2 * HG_BASE) == col // (2 * HG_BASE))
             & (row % (2 * HG_BASE) >= HG_BASE) & (col % (2 * HG_BASE) < HG_BASE))

    def normed(rows):
        return _rms(rows, nw).astype(jnp.bfloat16)

    def in_proj(get_xn, parity, store_lru):
        proj_ref = proj_refs[parity]

        def hg_chunk(c):
            proj_ref[:, c:c + COL_CHUNK] = _dot(get_xn(), win_ref[:, c:c + COL_CHUNK])

        def lru_chunk(c):
            store_lru(c - n_hg, _dot(get_xn(), win_ref[:, c:c + COL_CHUNK]))

        return ([functools.partial(hg_chunk, c) for c in range(0, n_hg, COL_CHUNK)]
                + [functools.partial(lru_chunk, c) for c in range(n_hg, D_IN, COL_CHUNK)])

    def sub_tile(blk, j):
        proj_ref = proj_refs[blk % 2]
        p0 = j * SUB
        r0 = blk * PROJ_ROWS + p0
        half = SUB // 2
        q = proj_ref[p0:p0 + SUB, 0:HG_WIDTH]
        fl = proj_ref[p0:p0 + SUB, HG_WIDTH:2 * HG_WIDTH]
        v = proj_ref[p0:p0 + SUB, 2 * HG_WIDTH:3 * HG_WIDTH].astype(jnp.bfloat16)
        q = q * jax.nn.sigmoid(q)
        f = lb + (1.0 - lb) * jax.nn.sigmoid(fl)
        k = 1.0 - f
        lf = jnp.log(f) * LOG2E
        lf_hi = lf.astype(jnp.bfloat16)
        lf_lo = (lf - lf_hi.astype(jnp.float32)).astype(jnp.bfloat16)
        yield
        beta = _dot(tri2, jnp.concatenate([lf_hi, lf_lo], axis=0))
        yield
        r31 = beta[HG_BASE - 1:HG_BASE]
        r63 = beta[2 * HG_BASE - 1:2 * HG_BASE]
        r95 = beta[3 * HG_BASE - 1:3 * HG_BASE]
        r127 = beta[4 * HG_BASE - 1:4 * HG_BASE]
        ref0 = jnp.concatenate([jnp.zeros((HG_BASE, HG_WIDTH), jnp.float32), _bcast_rows(r31, HG_BASE),
                                _bcast_rows(r63, HG_BASE), _bcast_rows(r95, HG_BASE)], axis=0)
        ref1 = jnp.concatenate([_bcast_rows(r31, 2 * HG_BASE), _bcast_rows(r95, 2 * HG_BASE)], axis=0)
        b0 = beta - ref0
        q0 = (q * jnp.exp2(b0)).astype(jnp.bfloat16)
        k0 = (k * jnp.exp2(-b0)).astype(jnp.bfloat16)
        k1 = (k * jnp.exp2(jnp.minimum(ref1 - beta, 0.0))).astype(jnp.bfloat16)
        yield
        q2 = (q[half:] * jnp.exp2(beta[half:] - r63)).astype(jnp.bfloat16)
        k2 = jnp.concatenate([(k[:half] * jnp.exp2(r63 - beta[:half])).astype(jnp.bfloat16),
                              jnp.zeros((half, HG_WIDTH), jnp.bfloat16)], axis=0)
        qs = (q * jnp.exp2(beta)).astype(jnp.bfloat16)
        ks = (k * jnp.exp2(r127 - beta)).astype(jnp.bfloat16)
        dec_end = jnp.exp2(r127)
        yield
        heads = [slice(h * HG_DIM, (h + 1) * HG_DIM) for h in range(HG_HEADS)]
        s01 = [_dot(q0[:, hs], jnp.concatenate([k0[:, hs], k1[:, hs]], axis=0), _NT) for hs in heads]
        s2 = [_dot(q2[:, hs], k2[:, hs], _NT) for hs in heads]
        vt = [v[:, hs].T for hs in heads]
        inc = [_dot(vt[h], ks[:, hs]) for h, hs in enumerate(heads)]
        yield
        g = proj_ref[p0:p0 + SUB, 3 * HG_WIDTH:4 * HG_WIDTH]
        gate = g * jax.nn.sigmoid(g)
        for h, hs in enumerate(heads):
            sc = jnp.where(mask0, s01[h][:, :SUB], 0.0) + jnp.where(mask1, s01[h][:, SUB:], 0.0)
            sc = jnp.concatenate([sc[:half], sc[half:] + s2[h]], axis=0)
            st = state_ref[h]
            o = _dot(jnp.concatenate([sc.astype(jnp.bfloat16), qs[:, hs]], axis=1),
                     jnp.concatenate([vt[h], st.astype(jnp.bfloat16)], axis=1), _NT)
            state_ref[h] = st * dec_end[:, hs] + inc[h]
            o = o * lax.rsqrt(jnp.mean(o * o, axis=-1, keepdims=True) + EPS) * hgw
            ohg_ref[0, r0:r0 + SUB, hs] = (o * gate[:, hs]).astype(jnp.bfloat16)

    def lru_to_scratch(off, val):
        proj0_ref[:, n_hg + off:n_hg + off + COL_CHUNK] = val

    def lru_to_output(off, val):
        lru_ref[0, PROJ_ROWS:2 * PROJ_ROWS, off:off + COL_CHUNK] = val

    def flush_first_block():
        lru_ref[0, 0:PROJ_ROWS, :] = proj0_ref[:, n_hg:D_IN]

    @pl.when((bi == 0) & (si == 0))
    def _():
        xn_first = normed(x_ref[0, 0:PROJ_ROWS, :])
        for step in in_proj(lambda: xn_first, 0, lru_to_scratch):
            step()
        xn1_ref[...] = normed(x_ref[0, PROJ_ROWS:2 * PROJ_ROWS, :])

    next_first = []

    def norm_next_first():
        next_first.append(normed(xnext_ref[0, 0:PROJ_ROWS, :]))

    def norm_next_second():
        xn1_ref[...] = normed(xnext_ref[0, PROJ_ROWS:2 * PROJ_ROWS, :])

    dots_b = in_proj(lambda: next_first[0], 0, lru_to_scratch)
    phases = (
        (in_proj(lambda: xn1_ref[...], 1, lru_to_output) + [flush_first_block, norm_next_first], 1, 0),
        (dots_b[:-PHASE_TRAIL] + [norm_next_second] + dots_b[-PHASE_TRAIL:], 1, PHASE_TRAIL),
    )
    for blk, (fill, lead, trail) in enumerate(phases):
        for step in fill[:lead]:
            step()
        spread = fill[lead:len(fill) - trail]
        points = [sub_tile(blk, j) for j in range(PROJ_ROWS // SUB)]
        n_points = len(points) * HGRN_YIELDS
        seen = done = 0
        for gen in points:
            for _ in gen:
                seen += 1
                upto = -(-seen * len(spread) // n_points)
                for step in spread[done:upto]:
                    step()
                done = upto
        assert seen == n_points and done == len(spread)
        for step in fill[len(fill) - trail:]:
            step()


def _lru_ffn_kernel(x_ref, ohg_ref, lru_ref, cw_ref, cb_ref, wg_ref, bg_ref, ap_ref, wout_ref, nw_ref, wgu_ref,
                    wd_ref, fw_ref, o_ref, xl_ref, hs_ref, mlru_ref, hcar_ref, acc_ref, h_ref, hn_ref, *,
                    tiles_per_seq):
    t = pl.program_id(0)
    seq_start = (t % tiles_per_seq) == 0
    d_ff = wd_ref.shape[0]
    n_ff = d_ff // FF_CHUNK
    last_rows = slice(SLAB_ROWS - TAIL, SLAB_ROWS)

    @pl.when(t == 0)
    def _():
        h_ref[...] = jnp.zeros_like(h_ref)
        hn_ref[...] = jnp.zeros_like(hn_ref)
        hcar_ref[...] = jnp.zeros_like(hcar_ref)
        xl_ref[:, last_rows, :] = jnp.zeros((N_LS, TAIL, LANES), jnp.float32)

    neg_c_sp2 = (-LRU_C * LOG2E) * jax.nn.softplus(-ap_ref[...])
    cw_rows = [jnp.broadcast_to(cw_ref[k:k + 1, :], (N_SEG, LRU_WIDTH)) for k in range(CONV_WIDTH)]
    cb_rows = jnp.broadcast_to(cb_ref[...], (N_SEG, LRU_WIDTH))
    bg = bg_ref[...]

    def store_slabs():
        val = lru_ref[0, :, 0:LRU_WIDTH]
        n_seg = FFN_TILE // SEG
        for ls in range(N_LS):
            lanes = slice(ls * LANES, (ls + 1) * LANES)
            xl_ref[ls, 0:TAIL, :] = jnp.where(seq_start, 0.0, xl_ref[ls, last_rows, :])
            for s in range(n_seg):
                xl_ref[ls, SEG_PITCH * s + TAIL:SEG_PITCH * (s + 1), :] = val[SEG * s:SEG * (s + 1), lanes]
                if s + 1 < n_seg:
                    xl_ref[ls, SEG_PITCH * (s + 1):SEG_PITCH * (s + 1) + TAIL, :] = (
                        val[SEG * (s + 1) - TAIL:SEG * (s + 1), lanes])

    conv_out = {}

    def conv(j):
        slab_start = SEG_PITCH * j * N_SEG + TAIL

        def x_slab(r):
            return jnp.concatenate([xl_ref[ls, pl.ds(slab_start + r, N_SEG, stride=SEG_PITCH), :]
                                    for ls in range(N_LS)], axis=1)

        xs = {r: x_slab(r) for r in range(1 - CONV_WIDTH, SEG)}
        xc_slabs = []
        for r in range(SEG):
            acc = cb_rows
            for tap in range(CONV_WIDTH):
                acc = acc + xs[r - (CONV_WIDTH - 1 - tap)] * cw_rows[tap]
            xc_slabs.append(acc)
        conv_out[j] = jnp.concatenate(xc_slabs, axis=0)

    def recur(j):
        xc = conv_out[j]
        xcb = xc.astype(jnp.bfloat16)
        gates = [_dot(xcb[:, gi * LRU_GROUP:(gi + 1) * LRU_GROUP], wg_ref[gi]) for gi in range(LRU_WIDTH // LRU_GROUP)]
        rr = jnp.concatenate([gt[:, :LRU_GROUP] for gt in gates], axis=1)
        ii = jnp.concatenate([gt[:, LRU_GROUP:] for gt in gates], axis=1)
        rr = jax.nn.sigmoid(rr + bg[0:1])
        ii = jax.nn.sigmoid(ii + bg[1:2])
        a = jnp.exp2(rr * neg_c_sp2)
        b_in = jnp.sqrt(1.0 - a * a) * (ii * xc)
        h_loc, a_loc = [b_in[0:N_SEG]], [a[0:N_SEG]]
        for r in range(1, SEG):
            rows = slice(r * N_SEG, (r + 1) * N_SEG)
            h_loc.append(a[rows] * h_loc[-1] + b_in[rows])
            a_loc.append(a[rows] * a_loc[-1])
        h_in = hcar_ref[...]
        if j == 0:
            h_in = jnp.where(seq_start, 0.0, h_in)
        carries = [h_in]
        for s in range(N_SEG):
            carries.append(a_loc[-1][s:s + 1] * carries[-1] + h_loc[-1][s:s + 1])
        hcar_ref[...] = carries[-1]
        carry = jnp.concatenate(carries[:-1], axis=0)
        hs0 = j * N_SEG * SEG_PITCH
        for r in range(SEG):
            h_r = h_loc[r] + a_loc[r] * carry
            for ls in range(N_LS):
                hs_ref[ls, pl.ds(hs0 + TAIL + r, N_SEG, stride=SEG_PITCH), :] = h_r[:, ls * LANES:(ls + 1) * LANES]
        hh = jnp.concatenate(
            [jnp.concatenate([hs_ref[ls, hs0 + SEG_PITCH * s + TAIL:hs0 + SEG_PITCH * (s + 1), :]
                              for s in range(N_SEG)], axis=0) for ls in range(N_LS)], axis=1)
        gl = lru_ref[0, j * SUB:(j + 1) * SUB, LRU_WIDTH:]
        mlru_ref[j * SUB:(j + 1) * SUB, :] = (hh * jax.nn.gelu(gl, approximate=True)).astype(jnp.bfloat16)

    def first_piece():
        store_slabs()
        conv(0)

    pieces = [first_piece, functools.partial(recur, 0)]
    for j in range(1, FFN_TILE // SUB):
        pieces += [functools.partial(conv, j), functools.partial(recur, j)]

    def out_proj():
        mixed = jnp.concatenate([ohg_ref[0], mlru_ref[...]], axis=1)
        h_ref[...] = x_ref[0] + _dot(mixed, wout_ref[...])

    def ffn_norm():
        hn_ref[...] = _rms(h_ref[...], nw_ref[...]).astype(jnp.bfloat16)

    pieces += [out_proj, ffn_norm]
    assert len(pieces) == n_ff - 1

    def gate_up(c):
        gt = _dot(hn_ref[...], wgu_ref[:, c * FF_CHUNK:(c + 1) * FF_CHUNK])
        up = _dot(hn_ref[...], wgu_ref[:, d_ff + c * FF_CHUNK:d_ff + (c + 1) * FF_CHUNK])
        return gt, up

    def activate(gt, up):
        return (gt * jax.nn.sigmoid(gt) * up).astype(jnp.bfloat16)

    act = activate(*gate_up(0))
    for c in range(n_ff - 1):
        gu = gate_up(c + 1)
        down = _dot(act, wd_ref[c * FF_CHUNK:(c + 1) * FF_CHUNK, :])
        if c == 0:
            acc_ref[...] = h_ref[...] + down
        else:
            acc_ref[...] += down
        pieces[c]()
        act = activate(*gu)
    wd_last = wd_ref[(n_ff - 1) * FF_CHUNK:n_ff * FF_CHUNK, :]
    halves = [slice(i * (FFN_TILE // 2), (i + 1) * (FFN_TILE // 2)) for i in range(2)]
    for rows in halves:
        acc_ref[rows, :] += _dot(act[rows], wd_last)
    for rows in halves:
        o_ref[0, rows, :] = _rms(acc_ref[rows, :], fw_ref[...])


def _block_diag_groups(w):
    per = LRU_GROUP // (LRU_WIDTH // LRU_BLOCKS)
    bd = LRU_WIDTH // LRU_BLOCKS
    w = w.reshape(LRU_WIDTH // LRU_GROUP, per, bd, bd)
    eye = jnp.eye(per, dtype=w.dtype)
    return jnp.einsum('gade,ab->gadbe', w, eye).reshape(LRU_WIDTH // LRU_GROUP, LRU_GROUP, LRU_GROUP)


def kernel(x, mix_norm_w, w_in, hg_lb, hg_norm_w, conv_w, conv_b, lru_wa, lru_ba, lru_wx, lru_bx,
           lru_a, w_out, ffn_norm_w, w_gate_up, w_down, final_norm_w):
    B, S, D = x.shape
    assert D == D_MODEL and w_in.shape == (1, D_MODEL, D_IN) and S % SEQ_TILE == 0 and SEQ_TILE == 2 * PROJ_ROWS
    bf = jnp.bfloat16
    d_ff = w_down.shape[1]
    assert d_ff % FF_CHUNK == 0

    win = w_in[0].astype(bf)
    wout = w_out[0].astype(bf)
    wg = jnp.concatenate([_block_diag_groups(lru_wa[0]), _block_diag_groups(lru_wx[0])], axis=-1).astype(bf)
    bg = jnp.stack([lru_ba[0], lru_bx[0]], axis=0)
    row = lambda a: a.reshape(1, -1)
    once = pl.Buffered(1)

    full = lambda shape: pl.BlockSpec(shape, lambda b, s: (0,) * len(shape), pipeline_mode=once)
    n_s = S // SEQ_TILE

    def next_tile(b, s):
        wrap = s == n_s - 1
        return (jnp.minimum(jnp.where(wrap, b + 1, b), B - 1), jnp.where(wrap, 0, s + 1), 0)

    o_hg, lru_in = pl.pallas_call(
        _hgrn_kernel,
        grid=(B, n_s),
        in_specs=[
            pl.BlockSpec((1, SEQ_TILE, D), lambda b, s: (b, s, 0)),
            pl.BlockSpec((1, SEQ_TILE, D), next_tile),
            full((1, D)), full((D, D_IN)), full((2, HG_WIDTH)), full((1, HG_DIM)),
        ],
        out_specs=[
            pl.BlockSpec((1, SEQ_TILE, HG_WIDTH), lambda b, s: (b, s, 0)),
            pl.BlockSpec((1, SEQ_TILE, 2 * LRU_WIDTH), lambda b, s: (b, s, 0)),
        ],
        out_shape=[
            jax.ShapeDtypeStruct((B, S, HG_WIDTH), bf),
            jax.ShapeDtypeStruct((B, S, 2 * LRU_WIDTH), jnp.float32),
        ],
        scratch_shapes=[
            pltpu.VMEM((PROJ_ROWS, D_IN), jnp.float32),
            pltpu.VMEM((PROJ_ROWS, 4 * HG_WIDTH), jnp.float32),
            pltpu.VMEM((PROJ_ROWS, D), jnp.bfloat16),
            pltpu.VMEM((HG_HEADS, HG_DIM, HG_DIM), jnp.float32),
        ],
        compiler_params=pltpu.CompilerParams(
            dimension_semantics=("arbitrary", "arbitrary"), vmem_limit_bytes=VMEM_LIMIT),
        name="hgrn",
    )(x, x, row(mix_norm_w[0]), win, hg_lb, row(hg_norm_w[0]))

    wgu = w_gate_up[0].astype(bf)
    wd = w_down[0].astype(bf)
    tiles_per_seq = S // FFN_TILE
    n_tiles = B * tiles_per_seq

    def cur(t):
        c = jnp.maximum(t - 1, 0)
        return (c // tiles_per_seq, c % tiles_per_seq, 0)

    def nxt(t):
        n = jnp.minimum(t, n_tiles - 1)
        return (n // tiles_per_seq, n % tiles_per_seq, 0)

    full1 = lambda shape: pl.BlockSpec(shape, lambda t: (0,) * len(shape), pipeline_mode=once)
    out = pl.pallas_call(
        functools.partial(_lru_ffn_kernel, tiles_per_seq=tiles_per_seq),
        grid=(n_tiles + 1,),
        in_specs=[
            pl.BlockSpec((1, FFN_TILE, D), nxt),
            pl.BlockSpec((1, FFN_TILE, HG_WIDTH), nxt),
            pl.BlockSpec((1, FFN_TILE, 2 * LRU_WIDTH), nxt),
            full1((CONV_WIDTH, LRU_WIDTH)), full1((1, LRU_WIDTH)),
            full1((LRU_WIDTH // LRU_GROUP, LRU_GROUP, 2 * LRU_GROUP)), full1((2, LRU_WIDTH)), full1((1, LRU_WIDTH)),
            full1((D_MIX, D)), full1((1, D)), full1((D, 2 * d_ff)), full1((d_ff, D)), full1((1, D)),
        ],
        out_specs=pl.BlockSpec((1, FFN_TILE, D), cur),
        out_shape=jax.ShapeDtypeStruct((B, S, D), jnp.float32),
        scratch_shapes=[
            pltpu.VMEM((N_LS, SLAB_ROWS, LANES), jnp.float32),
            pltpu.VMEM((N_LS, SLAB_ROWS, LANES), jnp.float32),
            pltpu.VMEM((FFN_TILE, LRU_WIDTH), jnp.bfloat16),
            pltpu.VMEM((1, LRU_WIDTH), jnp.float32),
            pltpu.VMEM((FFN_TILE, D), jnp.float32),
            pltpu.VMEM((FFN_TILE, D), jnp.float32),
            pltpu.VMEM((FFN_TILE, D), jnp.bfloat16),
        ],
        compiler_params=pltpu.CompilerParams(
            dimension_semantics=("arbitrary",), vmem_limit_bytes=VMEM_LIMIT),
        name="lru_ffn",
    )(x, o_hg, lru_in, conv_w[0], row(conv_b[0]), wg, bg, row(lru_a[0]), wout, row(ffn_norm_w[0]), wgu, wd,
      row(final_norm_w))
    return out
```

```python
import functools

import jax
import jax.numpy as jnp
from jax import lax
from jax.experimental import pallas as pl
from jax.experimental.pallas import tpu as pltpu

D_MODEL = 1024
HG_HEADS = 4
HG_DIM = 128
HG_WIDTH = HG_HEADS * HG_DIM
HG_BASE = 32
LRU_WIDTH = 512
LRU_BLOCKS = 8
LRU_GROUP = 256
CONV_WIDTH = 4
LRU_C = 8.0
D_MIX = HG_WIDTH + LRU_WIDTH
D_IN = 4 * HG_WIDTH + 2 * LRU_WIDTH
EPS = 1e-6
LOG2E = 1.4426950408889634

SUB = 128
PROJ_ROWS = 512
SEQ_TILE = 1024
FFN_TILE = 512
FF_CHUNK = 256
COL_CHUNK = 256
LANES = 128
ROW_TILE = 8
N_SEG = ROW_TILE
SEG = SUB // N_SEG
TAIL = ROW_TILE
SEG_PITCH = TAIL + SEG
SLAB_ROWS = SEG_PITCH * (FFN_TILE // SEG)
N_LS = LRU_WIDTH // LANES
PHASE_TRAIL = 2
HGRN_YIELDS = 5
VMEM_LIMIT = 56 * 1024 * 1024

_NT = (((1,), (1,)), ((), ()))


def _dot(a, b, dims=None):
    if dims is None:
        return jnp.dot(a, b, preferred_element_type=jnp.float32)
    return lax.dot_general(a, b, dims, preferred_element_type=jnp.float32)


def _rms(x, w):
    return x * lax.rsqrt(jnp.mean(x * x, axis=-1, keepdims=True) + EPS) * w


def _rows(shape):
    return lax.broadcasted_iota(jnp.int32, shape, 0)


def _bcast_rows(r, n):
    return jnp.broadcast_to(r, (n, r.shape[1]))


def _hgrn_kernel(x_ref, xnext_ref, nw_ref, win_ref, lb_ref, hgw_ref, ohg_ref, lru_ref,
                 proj0_ref, proj1_ref, state_ref):
    bi = pl.program_id(0)
    si = pl.program_id(1)
    proj_refs = (proj0_ref, proj1_ref)

    @pl.when(si == 0)
    def _():
        state_ref[...] = jnp.zeros_like(state_ref)

    nw = nw_ref[...]
    n_hg = 4 * HG_WIDTH

    lbp = lb_ref[...]
    lbe = jnp.exp(lbp - jnp.max(lbp, axis=0, keepdims=True))
    lb = lbe[0:1] / jnp.sum(lbe, axis=0, keepdims=True)
    hgw = hgw_ref[...]
    tri_r = _rows((SUB, 2 * SUB))
    tri_c = lax.broadcasted_iota(jnp.int32, (SUB, 2 * SUB), 1) % SUB
    tri2 = jnp.where(tri_c <= tri_r, 1.0, 0.0).astype(jnp.bfloat16)

    row = _rows((SUB, SUB))
    col = lax.broadcasted_iota(jnp.int32, (SUB, SUB), 1)
    mask0 = (row // HG_BASE == col // HG_BASE) & (col <= row)
    mask1 = ((row // (2 * HG_BASE) == col // (2 * HG_BASE))
             & (row % (2 * HG_BASE) >= HG_BASE) & (col % (2 * HG_BASE) < HG_BASE))

    def in_proj(load_x, parity, store_lru):
        proj_ref = proj_refs[parity]
        cell = []

        def norm():
            cell.append(_rms(load_x(), nw).astype(jnp.bfloat16))

        def hg_chunk(c):
            proj_ref[:, c:c + COL_CHUNK] = _dot(cell[0], win_ref[:, c:c + COL_CHUNK])

        def lru_chunk(c):
            store_lru(c - n_hg, _dot(cell[0], win_ref[:, c:c + COL_CHUNK]))

        return ([norm] + [functools.partial(hg_chunk, c) for c in range(0, n_hg, COL_CHUNK)]
                + [functools.partial(lru_chunk, c) for c in range(n_hg, D_IN, COL_CHUNK)])

    def sub_tile(blk, j):
        proj_ref = proj_refs[blk % 2]
        p0 = j * SUB
        r0 = blk * PROJ_ROWS + p0
        half = SUB // 2
        q = proj_ref[p0:p0 + SUB, 0:HG_WIDTH]
        fl = proj_ref[p0:p0 + SUB, HG_WIDTH:2 * HG_WIDTH]
        v = proj_ref[p0:p0 + SUB, 2 * HG_WIDTH:3 * HG_WIDTH].astype(jnp.bfloat16)
        q = q * jax.nn.sigmoid(q)
        f = lb + (1.0 - lb) * jax.nn.sigmoid(fl)
        k = 1.0 - f
        lf = jnp.log(f) * LOG2E
        lf_hi = lf.astype(jnp.bfloat16)
        lf_lo = (lf - lf_hi.astype(jnp.float32)).astype(jnp.bfloat16)
        yield
        beta = _dot(tri2, jnp.concatenate([lf_hi, lf_lo], axis=0))
        yield
        r31 = beta[HG_BASE - 1:HG_BASE]
        r63 = beta[2 * HG_BASE - 1:2 * HG_BASE]
        r95 = beta[3 * HG_BASE - 1:3 * HG_BASE]
        r127 = beta[4 * HG_BASE - 1:4 * HG_BASE]
        ref0 = jnp.concatenate([jnp.zeros((HG_BASE, HG_WIDTH), jnp.float32), _bcast_rows(r31, HG_BASE),
                                _bcast_rows(r63, HG_BASE), _bcast_rows(r95, HG_BASE)], axis=0)
        ref1 = jnp.concatenate([_bcast_rows(r31, 2 * HG_BASE), _bcast_rows(r95, 2 * HG_BASE)], axis=0)
        b0 = beta - ref0
        q0 = (q * jnp.exp2(b0)).astype(jnp.bfloat16)
        k0 = (k * jnp.exp2(-b0)).astype(jnp.bfloat16)
        k1 = (k * jnp.exp2(jnp.minimum(ref1 - beta, 0.0))).astype(jnp.bfloat16)
        yield
        q2 = (q[half:] * jnp.exp2(beta[half:] - r63)).astype(jnp.bfloat16)
        k2 = jnp.concatenate([(k[:half] * jnp.exp2(r63 - beta[:half])).astype(jnp.bfloat16),
                              jnp.zeros((half, HG_WIDTH), jnp.bfloat16)], axis=0)
        qs = (q * jnp.exp2(beta)).astype(jnp.bfloat16)
        ks = (k * jnp.exp2(r127 - beta)).astype(jnp.bfloat16)
        dec_end = jnp.exp2(r127)
        yield
        heads = [slice(h * HG_DIM, (h + 1) * HG_DIM) for h in range(HG_HEADS)]
        s01 = [_dot(q0[:, hs], jnp.concatenate([k0[:, hs], k1[:, hs]], axis=0), _NT) for hs in heads]
        s2 = [_dot(q2[:, hs], k2[:, hs], _NT) for hs in heads]
        vt = [v[:, hs].T for hs in heads]
        inc = [_dot(vt[h], ks[:, hs]) for h, hs in enumerate(heads)]
        yield
        g = proj_ref[p0:p0 + SUB, 3 * HG_WIDTH:4 * HG_WIDTH]
        gate = g * jax.nn.sigmoid(g)
        for h, hs in enumerate(heads):
            sc = jnp.where(mask0, s01[h][:, :SUB], 0.0) + jnp.where(mask1, s01[h][:, SUB:], 0.0)
            sc = jnp.concatenate([sc[:half], sc[half:] + s2[h]], axis=0)
            st = state_ref[h]
            o = _dot(jnp.concatenate([sc.astype(jnp.bfloat16), qs[:, hs]], axis=1),
                     jnp.concatenate([vt[h], st.astype(jnp.bfloat16)], axis=1), _NT)
            state_ref[h] = st * dec_end[:, hs] + inc[h]
            o = o * lax.rsqrt(jnp.mean(o * o, axis=-1, keepdims=True) + EPS) * hgw
            ohg_ref[0, r0:r0 + SUB, hs] = (o * gate[:, hs]).astype(jnp.bfloat16)

    def lru_to_scratch(off, val):
        proj0_ref[:, n_hg + off:n_hg + off + COL_CHUNK] = val

    def lru_to_output(off, val):
        lru_ref[0, PROJ_ROWS:2 * PROJ_ROWS, off:off + COL_CHUNK] = val

    def flush_first_block():
        lru_ref[0, 0:PROJ_ROWS, :] = proj0_ref[:, n_hg:D_IN]

    @pl.when((bi == 0) & (si == 0))
    def _():
        for step in in_proj(lambda: x_ref[0, 0:PROJ_ROWS, :], 0, lru_to_scratch):
            step()

    phases = (
        ([flush_first_block] + in_proj(lambda: x_ref[0, PROJ_ROWS:2 * PROJ_ROWS, :], 1, lru_to_output), 0),
        (in_proj(lambda: xnext_ref[0], 0, lru_to_scratch), PHASE_TRAIL),
    )
    for blk, (fill, trail) in enumerate(phases):
        spread = fill[:len(fill) - trail]
        points = [sub_tile(blk, j) for j in range(PROJ_ROWS // SUB)]
        n_points = len(points) * HGRN_YIELDS
        seen = done = 0
        for gen in points:
            for _ in gen:
                seen += 1
                upto = -(-seen * len(spread) // n_points)
                for step in spread[done:upto]:
                    step()
                done = upto
        assert seen == n_points and done == len(spread)
        for step in fill[len(fill) - trail:]:
            step()


def _lru_ffn_kernel(x_ref, ohg_ref, lru_ref, cw_ref, cb_ref, wg_ref, bg_ref, ap_ref, wout_ref, nw_ref, wgu_ref,
                    wd_ref, fw_ref, o_ref, xl_ref, hs_ref, mlru_ref, hcar_ref, acc_ref, h_ref, hn_ref, *,
                    tiles_per_seq):
    t = pl.program_id(0)
    seq_start = (t % tiles_per_seq) == 0
    d_ff = wd_ref.shape[0]
    n_ff = d_ff // FF_CHUNK
    last_rows = slice(SLAB_ROWS - TAIL, SLAB_ROWS)

    @pl.when(t == 0)
    def _():
        h_ref[...] = jnp.zeros_like(h_ref)
        hn_ref[...] = jnp.zeros_like(hn_ref)
        hcar_ref[...] = jnp.zeros_like(hcar_ref)
        xl_ref[:, last_rows, :] = jnp.zeros((N_LS, TAIL, LANES), jnp.float32)

    neg_c_sp2 = (-LRU_C * LOG2E) * jax.nn.softplus(-ap_ref[...])
    cw_rows = [jnp.broadcast_to(cw_ref[k:k + 1, :], (N_SEG, LRU_WIDTH)) for k in range(CONV_WIDTH)]
    cb_rows = jnp.broadcast_to(cb_ref[...], (N_SEG, LRU_WIDTH))
    bg = bg_ref[...]

    def store_slabs():
        val = lru_ref[0, :, 0:LRU_WIDTH]
        n_seg = FFN_TILE // SEG
        for ls in range(N_LS):
            lanes = slice(ls * LANES, (ls + 1) * LANES)
            xl_ref[ls, 0:TAIL, :] = jnp.where(seq_start, 0.0, xl_ref[ls, last_rows, :])
            for s in range(n_seg):
                xl_ref[ls, SEG_PITCH * s + TAIL:SEG_PITCH * (s + 1), :] = val[SEG * s:SEG * (s + 1), lanes]
                if s + 1 < n_seg:
                    xl_ref[ls, SEG_PITCH * (s + 1):SEG_PITCH * (s + 1) + TAIL, :] = (
                        val[SEG * (s + 1) - TAIL:SEG * (s + 1), lanes])

    conv_out = {}

    def conv(j):
        slab_start = SEG_PITCH * j * N_SEG + TAIL

        def x_slab(r):
            return jnp.concatenate([xl_ref[ls, pl.ds(slab_start + r, N_SEG, stride=SEG_PITCH), :]
                                    for ls in range(N_LS)], axis=1)

        xs = {r: x_slab(r) for r in range(1 - CONV_WIDTH, SEG)}
        xc_slabs = []
        for r in range(SEG):
            acc = cb_rows
            for tap in range(CONV_WIDTH):
                acc = acc + xs[r - (CONV_WIDTH - 1 - tap)] * cw_rows[tap]
            xc_slabs.append(acc)
        conv_out[j] = jnp.concatenate(xc_slabs, axis=0)

    def recur(j):
        xc = conv_out[j]
        xcb = xc.astype(jnp.bfloat16)
        gates = [_dot(xcb[:, gi * LRU_GROUP:(gi + 1) * LRU_GROUP], wg_ref[gi]) for gi in range(LRU_WIDTH // LRU_GROUP)]
        rr = jnp.concatenate([gt[:, :LRU_GROUP] for gt in gates], axis=1)
        ii = jnp.concatenate([gt[:, LRU_GROUP:] for gt in gates], axis=1)
        rr = jax.nn.sigmoid(rr + bg[0:1])
        ii = jax.nn.sigmoid(ii + bg[1:2])
        a = jnp.exp2(rr * neg_c_sp2)
        b_in = jnp.sqrt(1.0 - a * a) * (ii * xc)
        h_loc, a_loc = [b_in[0:N_SEG]], [a[0:N_SEG]]
        for r in range(1, SEG):
            rows = slice(r * N_SEG, (r + 1) * N_SEG)
            h_loc.append(a[rows] * h_loc[-1] + b_in[rows])
            a_loc.append(a[rows] * a_loc[-1])
        h_in = hcar_ref[...]
        if j == 0:
            h_in = jnp.where(seq_start, 0.0, h_in)
        carries = [h_in]
        for s in range(N_SEG):
            carries.append(a_loc[-1][s:s + 1] * carries[-1] + h_loc[-1][s:s + 1])
        hcar_ref[...] = carries[-1]
        carry = jnp.concatenate(carries[:-1], axis=0)
        hs0 = j * N_SEG * SEG_PITCH
        for r in range(SEG):
            h_r = h_loc[r] + a_loc[r] * carry
            for ls in range(N_LS):
                hs_ref[ls, pl.ds(hs0 + TAIL + r, N_SEG, stride=SEG_PITCH), :] = h_r[:, ls * LANES:(ls + 1) * LANES]
        hh = jnp.concatenate(
            [jnp.concatenate([hs_ref[ls, hs0 + SEG_PITCH * s + TAIL:hs0 + SEG_PITCH * (s + 1), :]
                              for s in range(N_SEG)], axis=0) for ls in range(N_LS)], axis=1)
        gl = lru_ref[0, j * SUB:(j + 1) * SUB, LRU_WIDTH:]
        mlru_ref[j * SUB:(j + 1) * SUB, :] = (hh * jax.nn.gelu(gl, approximate=True)).astype(jnp.bfloat16)

    def first_piece():
        store_slabs()
        conv(0)

    pieces = [first_piece, functools.partial(recur, 0)]
    for j in range(1, FFN_TILE // SUB):
        pieces += [functools.partial(conv, j), functools.partial(recur, j)]

    def out_proj():
        mixed = jnp.concatenate([ohg_ref[0], mlru_ref[...]], axis=1)
        h_ref[...] = x_ref[0] + _dot(mixed, wout_ref[...])

    def ffn_norm():
        hn_ref[...] = _rms(h_ref[...], nw_ref[...]).astype(jnp.bfloat16)

    pieces += [out_proj, ffn_norm]
    assert len(pieces) == n_ff - 1

    def gate_up(c):
        gt = _dot(hn_ref[...], wgu_ref[:, c * FF_CHUNK:(c + 1) * FF_CHUNK])
        up = _dot(hn_ref[...], wgu_ref[:, d_ff + c * FF_CHUNK:d_ff + (c + 1) * FF_CHUNK])
        return gt, up

    def activate(gt, up):
        return (gt * jax.nn.sigmoid(gt) * up).astype(jnp.bfloat16)

    act = activate(*gate_up(0))
    for c in range(n_ff - 1):
        gu = gate_up(c + 1)
        down = _dot(act, wd_ref[c * FF_CHUNK:(c + 1) * FF_CHUNK, :])
        if c == 0:
            acc_ref[...] = h_ref[...] + down
        else:
            acc_ref[...] += down
        pieces[c]()
        act = activate(*gu)
    wd_last = wd_ref[(n_ff - 1) * FF_CHUNK:n_ff * FF_CHUNK, :]
    halves = [slice(i * (FFN_TILE // 2), (i + 1) * (FFN_TILE // 2)) for i in range(2)]
    for rows in halves:
        acc_ref[rows, :] += _dot(act[rows], wd_last)
    for rows in halves:
        o_ref[0, rows, :] = _rms(acc_ref[rows, :], fw_ref[...])


def _block_diag_groups(w):
    per = LRU_GROUP // (LRU_WIDTH // LRU_BLOCKS)
    bd = LRU_WIDTH // LRU_BLOCKS
    w = w.reshape(LRU_WIDTH // LRU_GROUP, per, bd, bd)
    eye = jnp.eye(per, dtype=w.dtype)
    return jnp.einsum('gade,ab->gadbe', w, eye).reshape(LRU_WIDTH // LRU_GROUP, LRU_GROUP, LRU_GROUP)


def kernel(x, mix_norm_w, w_in, hg_lb, hg_norm_w, conv_w, conv_b, lru_wa, lru_ba, lru_wx, lru_bx,
           lru_a, w_out, ffn_norm_w, w_gate_up, w_down, final_norm_w):
    B, S, D = x.shape
    assert D == D_MODEL and w_in.shape == (1, D_MODEL, D_IN) and S % SEQ_TILE == 0 and SEQ_TILE == 2 * PROJ_ROWS
    bf = jnp.bfloat16
    d_ff = w_down.shape[1]
    assert d_ff % FF_CHUNK == 0

    win = w_in[0].astype(bf)
    wout = w_out[0].astype(bf)
    wg = jnp.concatenate([_block_diag_groups(lru_wa[0]), _block_diag_groups(lru_wx[0])], axis=-1).astype(bf)
    bg = jnp.stack([lru_ba[0], lru_bx[0]], axis=0)
    row = lambda a: a.reshape(1, -1)
    once = pl.Buffered(1)

    full = lambda shape: pl.BlockSpec(shape, lambda b, s: (0,) * len(shape), pipeline_mode=once)
    n_s = S // SEQ_TILE

    def next_block(b, s):
        wrap = s == n_s - 1
        return (jnp.minimum(jnp.where(wrap, b + 1, b), B - 1), jnp.where(wrap, 0, s + 1) * (SEQ_TILE // PROJ_ROWS), 0)

    o_hg, lru_in = pl.pallas_call(
        _hgrn_kernel,
        grid=(B, n_s),
        in_specs=[
            pl.BlockSpec((1, SEQ_TILE, D), lambda b, s: (b, s, 0)),
            pl.BlockSpec((1, PROJ_ROWS, D), next_block),
            full((1, D)), full((D, D_IN)), full((2, HG_WIDTH)), full((1, HG_DIM)),
        ],
        out_specs=[
            pl.BlockSpec((1, SEQ_TILE, HG_WIDTH), lambda b, s: (b, s, 0)),
            pl.BlockSpec((1, SEQ_TILE, 2 * LRU_WIDTH), lambda b, s: (b, s, 0)),
        ],
        out_shape=[
            jax.ShapeDtypeStruct((B, S, HG_WIDTH), bf),
            jax.ShapeDtypeStruct((B, S, 2 * LRU_WIDTH), jnp.float32),
        ],
        scratch_shapes=[
            pltpu.VMEM((PROJ_ROWS, D_IN), jnp.float32),
            pltpu.VMEM((PROJ_ROWS, 4 * HG_WIDTH), jnp.float32),
            pltpu.VMEM((HG_HEADS, HG_DIM, HG_DIM), jnp.float32),
        ],
        compiler_params=pltpu.CompilerParams(
            dimension_semantics=("arbitrary", "arbitrary"), vmem_limit_bytes=VMEM_LIMIT),
        name="hgrn",
    )(x, x, row(mix_norm_w[0]), win, hg_lb, row(hg_norm_w[0]))

    wgu = w_gate_up[0].astype(bf)
    wd = w_down[0].astype(bf)
    tiles_per_seq = S // FFN_TILE
    n_tiles = B * tiles_per_seq

    def cur(t):
        c = jnp.maximum(t - 1, 0)
        return (c // tiles_per_seq, c % tiles_per_seq, 0)

    def nxt(t):
        n = jnp.minimum(t, n_tiles - 1)
        return (n // tiles_per_seq, n % tiles_per_seq, 0)

    full1 = lambda shape: pl.BlockSpec(shape, lambda t: (0,) * len(shape), pipeline_mode=once)
    out = pl.pallas_call(
        functools.partial(_lru_ffn_kernel, tiles_per_seq=tiles_per_seq),
        grid=(n_tiles + 1,),
        in_specs=[
            pl.BlockSpec((1, FFN_TILE, D), nxt),
            pl.BlockSpec((1, FFN_TILE, HG_WIDTH), nxt),
            pl.BlockSpec((1, FFN_TILE, 2 * LRU_WIDTH), nxt),
            full1((CONV_WIDTH, LRU_WIDTH)), full1((1, LRU_WIDTH)),
            full1((LRU_WIDTH // LRU_GROUP, LRU_GROUP, 2 * LRU_GROUP)), full1((2, LRU_WIDTH)), full1((1, LRU_WIDTH)),
            full1((D_MIX, D)), full1((1, D)), full1((D, 2 * d_ff)), full1((d_ff, D)), full1((1, D)),
        ],
        out_specs=pl.BlockSpec((1, FFN_TILE, D), cur),
        out_shape=jax.ShapeDtypeStruct((B, S, D), jnp.float32),
        scratch_shapes=[
            pltpu.VMEM((N_LS, SLAB_ROWS, LANES), jnp.float32),
            pltpu.VMEM((N_LS, SLAB_ROWS, LANES), jnp.float32),
            pltpu.VMEM((FFN_TILE, LRU_WIDTH), jnp.bfloat16),
            pltpu.VMEM((1, LRU_WIDTH), jnp.float32),
            pltpu.VMEM((FFN_TILE, D), jnp.float32),
            pltpu.VMEM((FFN_TILE, D), jnp.float32),
            pltpu.VMEM((FFN_TILE, D), jnp.bfloat16),
        ],
        compiler_params=pltpu.CompilerParams(
            dimension_semantics=("arbitrary",), vmem_limit_bytes=VMEM_LIMIT),
        name="lru_ffn",
    )(x, o_hg, lru_in, conv_w[0], row(conv_b[0]), wg, bg, row(lru_a[0]), wout, row(ffn_norm_w[0]), wgu, wd,
      row(final_norm_w))
    return out
```

```python
import functools

import jax
import jax.numpy as jnp
from jax import lax
from jax.experimental import pallas as pl
from jax.experimental.pallas import tpu as pltpu

D_MODEL = 1024
HG_HEADS = 4
HG_DIM = 128
HG_WIDTH = HG_HEADS * HG_DIM
HG_BASE = 32
LRU_WIDTH = 512
LRU_BLOCKS = 8
LRU_GROUP = 256
CONV_WIDTH = 4
LRU_C = 8.0
D_MIX = HG_WIDTH + LRU_WIDTH
D_IN = 4 * HG_WIDTH + 2 * LRU_WIDTH
EPS = 1e-6
LOG2E = 1.4426950408889634

SUB = 128
PROJ_ROWS = 512
SEQ_TILE = 1024
FFN_TILE = 512
FF_CHUNK = 256
COL_CHUNK = 256
LANES = 128
ROW_TILE = 8
N_SEG = ROW_TILE
SEG = SUB // N_SEG
TAIL = ROW_TILE
SEG_PITCH = TAIL + SEG
SLAB_ROWS = SEG_PITCH * (FFN_TILE // SEG)
N_LS = LRU_WIDTH // LANES
PHASE_TRAIL = 2
HGRN_YIELDS = 5
VMEM_LIMIT = 56 * 1024 * 1024

_NT = (((1,), (1,)), ((), ()))


def _dot(a, b, dims=None):
    if dims is None:
        return jnp.dot(a, b, preferred_element_type=jnp.float32)
    return lax.dot_general(a, b, dims, preferred_element_type=jnp.float32)


def _rms(x, w):
    return x * lax.rsqrt(jnp.mean(x * x, axis=-1, keepdims=True) + EPS) * w


def _rows(shape):
    return lax.broadcasted_iota(jnp.int32, shape, 0)


def _bcast_rows(r, n):
    return jnp.broadcast_to(r, (n, r.shape[1]))


def _hgrn_kernel(x_ref, xnext_ref, nw_ref, win_ref, lb_ref, hgw_ref, ohg_ref, lru_ref,
                 proj0_ref, proj1_ref, state_ref):
    bi = pl.program_id(0)
    si = pl.program_id(1)
    proj_refs = (proj0_ref, proj1_ref)

    @pl.when(si == 0)
    def _():
        state_ref[...] = jnp.zeros_like(state_ref)

    nw = nw_ref[...]
    n_hg = 4 * HG_WIDTH

    lbp = lb_ref[...]
    lbe = jnp.exp(lbp - jnp.max(lbp, axis=0, keepdims=True))
    lb = lbe[0:1] / jnp.sum(lbe, axis=0, keepdims=True)
    hgw = hgw_ref[...]
    tri_r = _rows((SUB, 2 * SUB))
    tri_c = lax.broadcasted_iota(jnp.int32, (SUB, 2 * SUB), 1) % SUB
    tri2 = jnp.where(tri_c <= tri_r, 1.0, 0.0).astype(jnp.bfloat16)

    row = _rows((SUB, SUB))
    col = lax.broadcasted_iota(jnp.int32, (SUB, SUB), 1)
    mask0 = (row // HG_BASE == col // HG_BASE) & (col <= row)
    mask1 = ((row // (2 * HG_BASE) == col // (2 * HG_BASE))
             & (row % (2 * HG_BASE) >= HG_BASE) & (col % (2 * HG_BASE) < HG_BASE))

    def in_proj(load_x, parity, store_lru):
        proj_ref = proj_refs[parity]
        cell = []

        def norm():
            cell.append(_rms(load_x(), nw).astype(jnp.bfloat16))

        def hg_chunk(c):
            proj_ref[:, c:c + COL_CHUNK] = _dot(cell[0], win_ref[:, c:c + COL_CHUNK])

        def lru_chunk(c):
            store_lru(c - n_hg, _dot(cell[0], win_ref[:, c:c + COL_CHUNK]))

        return ([norm] + [functools.partial(hg_chunk, c) for c in range(0, n_hg, COL_CHUNK)]
                + [functools.partial(lru_chunk, c) for c in range(n_hg, D_IN, COL_CHUNK)])

    def sub_tile(blk, j):
        proj_ref = proj_refs[blk % 2]
        p0 = j * SUB
        r0 = blk * PROJ_ROWS + p0
        half = SUB // 2
        q = proj_ref[p0:p0 + SUB, 0:HG_WIDTH]
        fl = proj_ref[p0:p0 + SUB, HG_WIDTH:2 * HG_WIDTH]
        v = proj_ref[p0:p0 + SUB, 2 * HG_WIDTH:3 * HG_WIDTH].astype(jnp.bfloat16)
        q = q * jax.nn.sigmoid(q)
        f = lb + (1.0 - lb) * jax.nn.sigmoid(fl)
        k = 1.0 - f
        lf = jnp.log(f) * LOG2E
        lf_hi = lf.astype(jnp.bfloat16)
        lf_lo = (lf - lf_hi.astype(jnp.float32)).astype(jnp.bfloat16)
        yield
        beta = _dot(tri2, jnp.concatenate([lf_hi, lf_lo], axis=0))
        yield
        r31 = beta[HG_BASE - 1:HG_BASE]
        r63 = beta[2 * HG_BASE - 1:2 * HG_BASE]
        r95 = beta[3 * HG_BASE - 1:3 * HG_BASE]
        r127 = beta[4 * HG_BASE - 1:4 * HG_BASE]
        ref0 = jnp.concatenate([jnp.zeros((HG_BASE, HG_WIDTH), jnp.float32), _bcast_rows(r31, HG_BASE),
                                _bcast_rows(r63, HG_BASE), _bcast_rows(r95, HG_BASE)], axis=0)
        ref1 = jnp.concatenate([_bcast_rows(r31, 2 * HG_BASE), _bcast_rows(r95, 2 * HG_BASE)], axis=0)
        b0 = beta - ref0
        q0 = (q * jnp.exp2(b0)).astype(jnp.bfloat16)
        k0 = (k * jnp.exp2(-b0)).astype(jnp.bfloat16)
        k1 = (k * jnp.exp2(jnp.minimum(ref1 - beta, 0.0))).astype(jnp.bfloat16)
        yield
        q2 = (q[half:] * jnp.exp2(beta[half:] - r63)).astype(jnp.bfloat16)
        k2 = jnp.concatenate([(k[:half] * jnp.exp2(r63 - beta[:half])).astype(jnp.bfloat16),
                              jnp.zeros((half, HG_WIDTH), jnp.bfloat16)], axis=0)
        qs = (q * jnp.exp2(beta)).astype(jnp.bfloat16)
        ks = (k * jnp.exp2(r127 - beta)).astype(jnp.bfloat16)
        dec_end = jnp.exp2(r127)
        yield
        heads = [slice(h * HG_DIM, (h + 1) * HG_DIM) for h in range(HG_HEADS)]
        s01 = [_dot(q0[:, hs], jnp.concatenate([k0[:, hs], k1[:, hs]], axis=0), _NT) for hs in heads]
        s2 = [_dot(q2[:, hs], k2[:, hs], _NT) for hs in heads]
        vt = [v[:, hs].T for hs in heads]
        inc = [_dot(vt[h], ks[:, hs]) for h, hs in enumerate(heads)]
        yield
        g = proj_ref[p0:p0 + SUB, 3 * HG_WIDTH:4 * HG_WIDTH]
        gate = g * jax.nn.sigmoid(g)
        for h, hs in enumerate(heads):
            sc = jnp.where(mask0, s01[h][:, :SUB], 0.0) + jnp.where(mask1, s01[h][:, SUB:], 0.0)
            sc = jnp.concatenate([sc[:half], sc[half:] + s2[h]], axis=0)
            st = state_ref[h]
            o = _dot(jnp.concatenate([sc.astype(jnp.bfloat16), qs[:, hs]], axis=1),
                     jnp.concatenate([vt[h], st.astype(jnp.bfloat16)], axis=1), _NT)
            state_ref[h] = st * dec_end[:, hs] + inc[h]
            o = o * lax.rsqrt(jnp.mean(o * o, axis=-1, keepdims=True) + EPS) * hgw
            ohg_ref[0, r0:r0 + SUB, hs] = (o * gate[:, hs]).astype(jnp.bfloat16)

    def lru_to_scratch(off, val):
        proj0_ref[:, n_hg + off:n_hg + off + COL_CHUNK] = val

    def lru_to_output(off, val):
        lru_ref[0, PROJ_ROWS:2 * PROJ_ROWS, off:off + COL_CHUNK] = val

    def flush_first_block():
        lru_ref[0, 0:PROJ_ROWS, :] = proj0_ref[:, n_hg:D_IN]

    @pl.when((bi == 0) & (si == 0))
    def _():
        for step in in_proj(lambda: x_ref[0, 0:PROJ_ROWS, :], 0, lru_to_scratch):
            step()

    phases = (
        ([flush_first_block] + in_proj(lambda: x_ref[0, PROJ_ROWS:2 * PROJ_ROWS, :], 1, lru_to_output), 0),
        (in_proj(lambda: xnext_ref[0], 0, lru_to_scratch), PHASE_TRAIL),
    )
    for blk, (fill, trail) in enumerate(phases):
        spread = fill[:len(fill) - trail]
        points = [sub_tile(blk, j) for j in range(PROJ_ROWS // SUB)]
        n_points = len(points) * HGRN_YIELDS
        seen = done = 0
        for gen in points:
            for _ in gen:
                seen += 1
                upto = -(-seen * len(spread) // n_points)
                for step in spread[done:upto]:
                    step()
                done = upto
        assert seen == n_points and done == len(spread)
        for step in fill[len(fill) - trail:]:
            step()


def _lru_ffn_kernel(x_ref, ohg_ref, lru_ref, cw_ref, cb_ref, wg_ref, bg_ref, ap_ref, wout_ref, nw_ref, wgu_ref,
                    wd_ref, fw_ref, o_ref, xl_ref, hs_ref, mlru_ref, hcar_ref, acc_ref, h_ref, hn_ref, *,
                    tiles_per_seq):
    t = pl.program_id(0)
    seq_start = (t % tiles_per_seq) == 0
    d_ff = wd_ref.shape[0]
    last_rows = slice(SLAB_ROWS - TAIL, SLAB_ROWS)

    @pl.when(t == 0)
    def _():
        h_ref[...] = jnp.zeros_like(h_ref)
        hn_ref[...] = jnp.zeros_like(hn_ref)
        hcar_ref[...] = jnp.zeros_like(hcar_ref)
        xl_ref[:, last_rows, :] = jnp.zeros((N_LS, TAIL, LANES), jnp.float32)

    neg_c_sp2 = (-LRU_C * LOG2E) * jax.nn.softplus(-ap_ref[...])
    cw_rows = [jnp.broadcast_to(cw_ref[k:k + 1, :], (N_SEG, LRU_WIDTH)) for k in range(CONV_WIDTH)]
    cb_rows = jnp.broadcast_to(cb_ref[...], (N_SEG, LRU_WIDTH))
    bg = bg_ref[...]

    def store_slabs():
        val = lru_ref[0, :, 0:LRU_WIDTH]
        n_seg = FFN_TILE // SEG
        for ls in range(N_LS):
            lanes = slice(ls * LANES, (ls + 1) * LANES)
            xl_ref[ls, 0:TAIL, :] = jnp.where(seq_start, 0.0, xl_ref[ls, last_rows, :])
            for s in range(n_seg):
                xl_ref[ls, SEG_PITCH * s + TAIL:SEG_PITCH * (s + 1), :] = val[SEG * s:SEG * (s + 1), lanes]
                if s + 1 < n_seg:
                    xl_ref[ls, SEG_PITCH * (s + 1):SEG_PITCH * (s + 1) + TAIL, :] = (
                        val[SEG * (s + 1) - TAIL:SEG * (s + 1), lanes])

    conv_out = {}

    def conv(j):
        slab_start = SEG_PITCH * j * N_SEG + TAIL

        def x_slab(r):
            return jnp.concatenate([xl_ref[ls, pl.ds(slab_start + r, N_SEG, stride=SEG_PITCH), :]
                                    for ls in range(N_LS)], axis=1)

        xs = {r: x_slab(r) for r in range(1 - CONV_WIDTH, SEG)}
        xc_slabs = []
        for r in range(SEG):
            acc = cb_rows
            for tap in range(CONV_WIDTH):
                acc = acc + xs[r - (CONV_WIDTH - 1 - tap)] * cw_rows[tap]
            xc_slabs.append(acc)
        conv_out[j] = jnp.concatenate(xc_slabs, axis=0)

    def recur(j):
        xc = conv_out[j]
        xcb = xc.astype(jnp.bfloat16)
        gates = [_dot(xcb[:, gi * LRU_GROUP:(gi + 1) * LRU_GROUP], wg_ref[gi]) for gi in range(LRU_WIDTH // LRU_GROUP)]
        rr = jnp.concatenate([gt[:, :LRU_GROUP] for gt in gates], axis=1)
        ii = jnp.concatenate([gt[:, LRU_GROUP:] for gt in gates], axis=1)
        rr = jax.nn.sigmoid(rr + bg[0:1])
        ii = jax.nn.sigmoid(ii + bg[1:2])
        a = jnp.exp2(rr * neg_c_sp2)
        b_in = jnp.sqrt(1.0 - a * a) * (ii * xc)
        h_loc, a_loc = [b_in[0:N_SEG]], [a[0:N_SEG]]
        for r in range(1, SEG):
            rows = slice(r * N_SEG, (r + 1) * N_SEG)
            h_loc.append(a[rows] * h_loc[-1] + b_in[rows])
            a_loc.append(a[rows] * a_loc[-1])
        h_in = hcar_ref[...]
        if j == 0:
            h_in = jnp.where(seq_start, 0.0, h_in)
        carries = [h_in]
        for s in range(N_SEG):
            carries.append(a_loc[-1][s:s + 1] * carries[-1] + h_loc[-1][s:s + 1])
        hcar_ref[...] = carries[-1]
        carry = jnp.concatenate(carries[:-1], axis=0)
        hs0 = j * N_SEG * SEG_PITCH
        for r in range(SEG):
            h_r = h_loc[r] + a_loc[r] * carry
            for ls in range(N_LS):
                hs_ref[ls, pl.ds(hs0 + TAIL + r, N_SEG, stride=SEG_PITCH), :] = h_r[:, ls * LANES:(ls + 1) * LANES]
        hh = jnp.concatenate(
            [jnp.concatenate([hs_ref[ls, hs0 + SEG_PITCH * s + TAIL:hs0 + SEG_PITCH * (s + 1), :]
                              for s in range(N_SEG)], axis=0) for ls in range(N_LS)], axis=1)
        gl = lru_ref[0, j * SUB:(j + 1) * SUB, LRU_WIDTH:]
        mlru_ref[j * SUB:(j + 1) * SUB, :] = (hh * jax.nn.gelu(gl, approximate=True)).astype(jnp.bfloat16)

    def first_piece():
        store_slabs()
        conv(0)

    pieces = [first_piece, functools.partial(recur, 0)]
    for j in range(1, FFN_TILE // SUB):
        pieces += [functools.partial(conv, j), functools.partial(recur, j)]

    def out_proj():
        mixed = jnp.concatenate([ohg_ref[0], mlru_ref[...]], axis=1)
        h_ref[...] = x_ref[0] + _dot(mixed, wout_ref[...])

    def ffn_norm():
        hn_ref[...] = _rms(h_ref[...], nw_ref[...]).astype(jnp.bfloat16)

    pieces += [out_proj, ffn_norm]

    bounds = list(range(0, d_ff, FF_CHUNK)) + [d_ff]
    n_chunks = len(bounds) - 1
    per_chunk = -(-len(pieces) // (n_chunks - 1))

    def gate_up(c):
        gt = _dot(hn_ref[...], wgu_ref[:, bounds[c]:bounds[c + 1]])
        up = _dot(hn_ref[...], wgu_ref[:, d_ff + bounds[c]:d_ff + bounds[c + 1]])
        return gt, up

    def activate(gt, up):
        return (gt * jax.nn.sigmoid(gt) * up).astype(jnp.bfloat16)

    act = activate(*gate_up(0))
    for c in range(n_chunks - 1):
        gu = gate_up(c + 1)
        down = _dot(act, wd_ref[bounds[c]:bounds[c + 1], :])
        if c == 0:
            acc_ref[...] = h_ref[...] + down
        else:
            acc_ref[...] += down
        for piece in pieces[c * per_chunk:(c + 1) * per_chunk]:
            piece()
        act = activate(*gu)
    assert (n_chunks - 1) * per_chunk >= len(pieces)
    wd_last = wd_ref[bounds[-2]:bounds[-1], :]
    halves = [slice(i * (FFN_TILE // 2), (i + 1) * (FFN_TILE // 2)) for i in range(2)]
    for rows in halves:
        acc_ref[rows, :] += _dot(act[rows], wd_last)
    for rows in halves:
        o_ref[0, rows, :] = _rms(acc_ref[rows, :], fw_ref[...])


def _block_diag_groups(w):
    per = LRU_GROUP // (LRU_WIDTH // LRU_BLOCKS)
    bd = LRU_WIDTH // LRU_BLOCKS
    w = w.reshape(LRU_WIDTH // LRU_GROUP, per, bd, bd)
    eye = jnp.eye(per, dtype=w.dtype)
    return jnp.einsum('gade,ab->gadbe', w, eye).reshape(LRU_WIDTH // LRU_GROUP, LRU_GROUP, LRU_GROUP)


def kernel(x, mix_norm_w, w_in, hg_lb, hg_norm_w, conv_w, conv_b, lru_wa, lru_ba, lru_wx, lru_bx,
           lru_a, w_out, ffn_norm_w, w_gate_up, w_down, final_norm_w):
    B, S, D = x.shape
    assert D == D_MODEL and w_in.shape == (1, D_MODEL, D_IN) and S % SEQ_TILE == 0 and SEQ_TILE == 2 * PROJ_ROWS
    bf = jnp.bfloat16
    d_ff = w_down.shape[1]
    assert d_ff % (2 * LANES) == 0

    win = w_in[0].astype(bf)
    wout = w_out[0].astype(bf)
    wg = jnp.concatenate([_block_diag_groups(lru_wa[0]), _block_diag_groups(lru_wx[0])], axis=-1).astype(bf)
    bg = jnp.stack([lru_ba[0], lru_bx[0]], axis=0)
    row = lambda a: a.reshape(1, -1)
    once = pl.Buffered(1)

    full = lambda shape: pl.BlockSpec(shape, lambda b, s: (0,) * len(shape), pipeline_mode=once)
    n_s = S // SEQ_TILE

    def next_block(b, s):
        wrap = s == n_s - 1
        return (jnp.minimum(jnp.where(wrap, b + 1, b), B - 1), jnp.where(wrap, 0, s + 1) * (SEQ_TILE // PROJ_ROWS), 0)

    o_hg, lru_in = pl.pallas_call(
        _hgrn_kernel,
        grid=(B, n_s),
        in_specs=[
            pl.BlockSpec((1, SEQ_TILE, D), lambda b, s: (b, s, 0)),
            pl.BlockSpec((1, PROJ_ROWS, D), next_block),
            full((1, D)), full((D, D_IN)), full((2, HG_WIDTH)), full((1, HG_DIM)),
        ],
        out_specs=[
            pl.BlockSpec((1, SEQ_TILE, HG_WIDTH), lambda b, s: (b, s, 0)),
            pl.BlockSpec((1, SEQ_TILE, 2 * LRU_WIDTH), lambda b, s: (b, s, 0)),
        ],
        out_shape=[
            jax.ShapeDtypeStruct((B, S, HG_WIDTH), bf),
            jax.ShapeDtypeStruct((B, S, 2 * LRU_WIDTH), jnp.float32),
        ],
        scratch_shapes=[
            pltpu.VMEM((PROJ_ROWS, D_IN), jnp.float32),
            pltpu.VMEM((PROJ_ROWS, 4 * HG_WIDTH), jnp.float32),
            pltpu.VMEM((HG_HEADS, HG_DIM, HG_DIM), jnp.float32),
        ],
        compiler_params=pltpu.CompilerParams(
            dimension_semantics=("arbitrary", "arbitrary"), vmem_limit_bytes=VMEM_LIMIT),
        name="hgrn",
    )(x, x, row(mix_norm_w[0]), win, hg_lb, row(hg_norm_w[0]))

    wgu = w_gate_up[0].astype(bf)
    wd = w_down[0].astype(bf)
    tiles_per_seq = S // FFN_TILE
    n_tiles = B * tiles_per_seq

    def cur(t):
        c = jnp.maximum(t - 1, 0)
        return (c // tiles_per_seq, c % tiles_per_seq, 0)

    def nxt(t):
        n = jnp.minimum(t, n_tiles - 1)
        return (n // tiles_per_seq, n % tiles_per_seq, 0)

    full1 = lambda shape: pl.BlockSpec(shape, lambda t: (0,) * len(shape), pipeline_mode=once)
    out = pl.pallas_call(
        functools.partial(_lru_ffn_kernel, tiles_per_seq=tiles_per_seq),
        grid=(n_tiles + 1,),
        in_specs=[
            pl.BlockSpec((1, FFN_TILE, D), nxt),
            pl.BlockSpec((1, FFN_TILE, HG_WIDTH), nxt),
            pl.BlockSpec((1, FFN_TILE, 2 * LRU_WIDTH), nxt),
            full1((CONV_WIDTH, LRU_WIDTH)), full1((1, LRU_WIDTH)),
            full1((LRU_WIDTH // LRU_GROUP, LRU_GROUP, 2 * LRU_GROUP)), full1((2, LRU_WIDTH)), full1((1, LRU_WIDTH)),
            full1((D_MIX, D)), full1((1, D)), full1((D, 2 * d_ff)), full1((d_ff, D)), full1((1, D)),
        ],
        out_specs=pl.BlockSpec((1, FFN_TILE, D), cur),
        out_shape=jax.ShapeDtypeStruct((B, S, D), jnp.float32),
        scratch_shapes=[
            pltpu.VMEM((N_LS, SLAB_ROWS, LANES), jnp.float32),
            pltpu.VMEM((N_LS, SLAB_ROWS, LANES), jnp.float32),
            pltpu.VMEM((FFN_TILE, LRU_WIDTH), jnp.bfloat16),
            pltpu.VMEM((1, LRU_WIDTH), jnp.float32),
            pltpu.VMEM((FFN_TILE, D), jnp.float32),
            pltpu.VMEM((FFN_TILE, D), jnp.float32),
            pltpu.VMEM((FFN_TILE, D), jnp.bfloat16),
        ],
        compiler_params=pltpu.CompilerParams(
            dimension_semantics=("arbitrary",), vmem_limit_bytes=VMEM_LIMIT),
        name="lru_ffn",
    )(x, o_hg, lru_in, conv_w[0], row(conv_b[0]), wg, bg, row(lru_a[0]), wout, row(ffn_norm_w[0]), wgu, wd,
      row(final_norm_w))
    return out
```

```python
import functools

import jax
import jax.numpy as jnp
from jax import lax
from jax.experimental import pallas as pl
from jax.experimental.pallas import tpu as pltpu

D_MODEL = 1024
HG_HEADS = 4
HG_DIM = 128
HG_WIDTH = HG_HEADS * HG_DIM
HG_BASE = 32
LRU_WIDTH = 512
LRU_BLOCKS = 8
LRU_GROUP = 256
CONV_WIDTH = 4
LRU_C = 8.0
D_MIX = HG_WIDTH + LRU_WIDTH
D_IN = 4 * HG_WIDTH + 2 * LRU_WIDTH
EPS = 1e-6
LOG2E = 1.4426950408889634

SUB = 128
PROJ_ROWS = 512
SEQ_TILE = 1024
FFN_TILE = 512
FF_CHUNK = 256
COL_CHUNK = 256
LANES = 128
ROW_TILE = 8
N_SEG = ROW_TILE
SEG = SUB // N_SEG
TAIL = ROW_TILE
SEG_PITCH = TAIL + SEG
SLAB_ROWS = SEG_PITCH * (FFN_TILE // SEG)
N_LS = LRU_WIDTH // LANES
PHASE_TRAIL = 2
HGRN_YIELDS = 5
VMEM_LIMIT = 56 * 1024 * 1024

_NT = (((1,), (1,)), ((), ()))


def _dot(a, b, dims=None):
    if dims is None:
        return jnp.dot(a, b, preferred_element_type=jnp.float32)
    return lax.dot_general(a, b, dims, preferred_element_type=jnp.float32)


def _rms(x, w):
    return x * lax.rsqrt(jnp.mean(x * x, axis=-1, keepdims=True) + EPS) * w


def _rows(shape):
    return lax.broadcasted_iota(jnp.int32, shape, 0)


def _bcast_rows(r, n):
    return jnp.broadcast_to(r, (n, r.shape[1]))


def _hgrn_kernel(x_ref, xnext_ref, nw_ref, win_ref, lb_ref, hgw_ref, ohg_ref, lru_ref,
                 proj0_ref, proj1_ref, state_ref):
    bi = pl.program_id(0)
    si = pl.program_id(1)
    proj_refs = (proj0_ref, proj1_ref)

    @pl.when(si == 0)
    def _():
        state_ref[...] = jnp.zeros_like(state_ref)

    nw = nw_ref[...]
    n_hg = 4 * HG_WIDTH

    lbp = lb_ref[...]
    lbe = jnp.exp(lbp - jnp.max(lbp, axis=0, keepdims=True))
    lb = lbe[0:1] / jnp.sum(lbe, axis=0, keepdims=True)
    hgw = hgw_ref[...]
    tri_r = _rows((SUB, 2 * SUB))
    tri_c = lax.broadcasted_iota(jnp.int32, (SUB, 2 * SUB), 1) % SUB
    tri2 = jnp.where(tri_c <= tri_r, 1.0, 0.0).astype(jnp.bfloat16)

    row = _rows((SUB, SUB))
    col = lax.broadcasted_iota(jnp.int32, (SUB, SUB), 1)
    mask0 = (row // HG_BASE == col // HG_BASE) & (col <= row)
    mask1 = ((row // (2 * HG_BASE) == col // (2 * HG_BASE))
             & (row % (2 * HG_BASE) >= HG_BASE) & (col % (2 * HG_BASE) < HG_BASE))

    def in_proj(load_x, parity, store_lru):
        proj_ref = proj_refs[parity]
        cell = []

        def norm():
            cell.append(_rms(load_x(), nw).astype(jnp.bfloat16))

        def hg_chunk(c):
            proj_ref[:, c:c + COL_CHUNK] = _dot(cell[0], win_ref[:, c:c + COL_CHUNK])

        def lru_chunk(c):
            store_lru(c - n_hg, _dot(cell[0], win_ref[:, c:c + COL_CHUNK]))

        return ([norm] + [functools.partial(hg_chunk, c) for c in range(0, n_hg, COL_CHUNK)]
                + [functools.partial(lru_chunk, c) for c in range(n_hg, D_IN, COL_CHUNK)])

    def sub_tile(blk, j):
        proj_ref = proj_refs[blk % 2]
        p0 = j * SUB
        r0 = blk * PROJ_ROWS + p0
        half = SUB // 2
        q = proj_ref[p0:p0 + SUB, 0:HG_WIDTH]
        fl = proj_ref[p0:p0 + SUB, HG_WIDTH:2 * HG_WIDTH]
        v = proj_ref[p0:p0 + SUB, 2 * HG_WIDTH:3 * HG_WIDTH].astype(jnp.bfloat16)
        q = q * jax.nn.sigmoid(q)
        f = lb + (1.0 - lb) * jax.nn.sigmoid(fl)
        k = 1.0 - f
        lf = jnp.log(f) * LOG2E
        lf_hi = lf.astype(jnp.bfloat16)
        lf_lo = (lf - lf_hi.astype(jnp.float32)).astype(jnp.bfloat16)
        yield
        beta = _dot(tri2, jnp.concatenate([lf_hi, lf_lo], axis=0))
        yield
        r31 = beta[HG_BASE - 1:HG_BASE]
        r63 = beta[2 * HG_BASE - 1:2 * HG_BASE]
        r95 = beta[3 * HG_BASE - 1:3 * HG_BASE]
        r127 = beta[4 * HG_BASE - 1:4 * HG_BASE]
        ref0 = jnp.concatenate([jnp.zeros((HG_BASE, HG_WIDTH), jnp.float32), _bcast_rows(r31, HG_BASE),
                                _bcast_rows(r63, HG_BASE), _bcast_rows(r95, HG_BASE)], axis=0)
        ref1 = jnp.concatenate([_bcast_rows(r31, 2 * HG_BASE), _bcast_rows(r95, 2 * HG_BASE)], axis=0)
        b0 = beta - ref0
        q0 = (q * jnp.exp2(b0)).astype(jnp.bfloat16)
        k0 = (k * jnp.exp2(-b0)).astype(jnp.bfloat16)
        k1 = (k * jnp.exp2(jnp.minimum(ref1 - beta, 0.0))).astype(jnp.bfloat16)
        yield
        q2 = (q[half:] * jnp.exp2(beta[half:] - r63)).astype(jnp.bfloat16)
        k2 = jnp.concatenate([(k[:half] * jnp.exp2(r63 - beta[:half])).astype(jnp.bfloat16),
                              jnp.zeros((half, HG_WIDTH), jnp.bfloat16)], axis=0)
        qs = (q * jnp.exp2(beta)).astype(jnp.bfloat16)
        ks = (k * jnp.exp2(r127 - beta)).astype(jnp.bfloat16)
        dec_end = jnp.exp2(r127)
        yield
        heads = [slice(h * HG_DIM, (h + 1) * HG_DIM) for h in range(HG_HEADS)]
        s01 = [_dot(q0[:, hs], jnp.concatenate([k0[:, hs], k1[:, hs]], axis=0), _NT) for hs in heads]
        s2 = [_dot(q2[:, hs], k2[:, hs], _NT) for hs in heads]
        vt = [v[:, hs].T for hs in heads]
        inc = [_dot(vt[h], ks[:, hs]) for h, hs in enumerate(heads)]
        yield
        g = proj_ref[p0:p0 + SUB, 3 * HG_WIDTH:4 * HG_WIDTH]
        gate = g * jax.nn.sigmoid(g)
        for h, hs in enumerate(heads):
            sc = jnp.where(mask0, s01[h][:, :SUB], 0.0) + jnp.where(mask1, s01[h][:, SUB:], 0.0)
            sc = jnp.concatenate([sc[:half], sc[half:] + s2[h]], axis=0)
            st = state_ref[h]
            o = _dot(jnp.concatenate([sc.astype(jnp.bfloat16), qs[:, hs]], axis=1),
                     jnp.concatenate([vt[h], st.astype(jnp.bfloat16)], axis=1), _NT)
            state_ref[h] = st * dec_end[:, hs] + inc[h]
            o = o * lax.rsqrt(jnp.mean(o * o, axis=-1, keepdims=True) + EPS) * hgw
            ohg_ref[0, r0:r0 + SUB, hs] = (o * gate[:, hs]).astype(jnp.bfloat16)

    def lru_to_scratch(off, val):
        proj0_ref[:, n_hg + off:n_hg + off + COL_CHUNK] = val

    def lru_to_output(off, val):
        lru_ref[0, PROJ_ROWS:2 * PROJ_ROWS, off:off + COL_CHUNK] = val

    def flush_first_block():
        lru_ref[0, 0:PROJ_ROWS, :] = proj0_ref[:, n_hg:D_IN]

    @pl.when((bi == 0) & (si == 0))
    def _():
        for step in in_proj(lambda: x_ref[0, 0:PROJ_ROWS, :], 0, lru_to_scratch):
            step()

    phases = (
        ([flush_first_block] + in_proj(lambda: x_ref[0, PROJ_ROWS:2 * PROJ_ROWS, :], 1, lru_to_output), 0),
        (in_proj(lambda: xnext_ref[0], 0, lru_to_scratch), PHASE_TRAIL),
    )
    for blk, (fill, trail) in enumerate(phases):
        spread = fill[:len(fill) - trail]
        points = [sub_tile(blk, j) for j in range(PROJ_ROWS // SUB)]
        n_points = len(points) * HGRN_YIELDS
        seen = done = 0
        for gen in points:
            for _ in gen:
                seen += 1
                upto = -(-seen * len(spread) // n_points)
                for step in spread[done:upto]:
                    step()
                done = upto
        assert seen == n_points and done == len(spread)
        for step in fill[len(fill) - trail:]:
            step()


def _lru_ffn_kernel(x_ref, ohg_ref, lru_ref, cw_ref, cb_ref, wg_ref, bg_ref, ap_ref, wout_ref, nw_ref, wgu_ref,
                    wd_ref, fw_ref, o_ref, xl_ref, hs_ref, mlru_ref, hcar_ref, acc_ref, h_ref, hn_ref, *,
                    tiles_per_seq):
    t = pl.program_id(0)
    seq_start = (t % tiles_per_seq) == 0
    d_ff = wd_ref.shape[0]
    n_ff = d_ff // FF_CHUNK
    last_rows = slice(SLAB_ROWS - TAIL, SLAB_ROWS)

    neg_c_sp2 = (-LRU_C * LOG2E) * jax.nn.softplus(-ap_ref[...])
    cw_rows = [jnp.broadcast_to(cw_ref[k:k + 1, :], (N_SEG, LRU_WIDTH)) for k in range(CONV_WIDTH)]
    cb_rows = jnp.broadcast_to(cb_ref[...], (N_SEG, LRU_WIDTH))
    bg = bg_ref[...]

    def store_slabs():
        val = lru_ref[0, :, 0:LRU_WIDTH]
        n_seg = FFN_TILE // SEG
        for ls in range(N_LS):
            lanes = slice(ls * LANES, (ls + 1) * LANES)
            xl_ref[ls, 0:TAIL, :] = jnp.where(seq_start, 0.0, xl_ref[ls, last_rows, :])
            for s in range(n_seg):
                xl_ref[ls, SEG_PITCH * s + TAIL:SEG_PITCH * (s + 1), :] = val[SEG * s:SEG * (s + 1), lanes]
                if s + 1 < n_seg:
                    xl_ref[ls, SEG_PITCH * (s + 1):SEG_PITCH * (s + 1) + TAIL, :] = (
                        val[SEG * (s + 1) - TAIL:SEG * (s + 1), lanes])

    conv_out = {}

    def conv(j):
        slab_start = SEG_PITCH * j * N_SEG + TAIL

        def x_slab(r):
            return jnp.concatenate([xl_ref[ls, pl.ds(slab_start + r, N_SEG, stride=SEG_PITCH), :]
                                    for ls in range(N_LS)], axis=1)

        xs = {r: x_slab(r) for r in range(1 - CONV_WIDTH, SEG)}
        xc_slabs = []
        for r in range(SEG):
            acc = cb_rows
            for tap in range(CONV_WIDTH):
                acc = acc + xs[r - (CONV_WIDTH - 1 - tap)] * cw_rows[tap]
            xc_slabs.append(acc)
        conv_out[j] = jnp.concatenate(xc_slabs, axis=0)

    def recur(j):
        xc = conv_out[j]
        xcb = xc.astype(jnp.bfloat16)
        gates = [_dot(xcb[:, gi * LRU_GROUP:(gi + 1) * LRU_GROUP], wg_ref[gi]) for gi in range(LRU_WIDTH // LRU_GROUP)]
        rr = jnp.concatenate([gt[:, :LRU_GROUP] for gt in gates], axis=1)
        ii = jnp.concatenate([gt[:, LRU_GROUP:] for gt in gates], axis=1)
        rr = jax.nn.sigmoid(rr + bg[0:1])
        ii = jax.nn.sigmoid(ii + bg[1:2])
        a = jnp.exp2(rr * neg_c_sp2)
        b_in = jnp.sqrt(1.0 - a * a) * (ii * xc)
        h_loc, a_loc = [b_in[0:N_SEG]], [a[0:N_SEG]]
        for r in range(1, SEG):
            rows = slice(r * N_SEG, (r + 1) * N_SEG)
            h_loc.append(a[rows] * h_loc[-1] + b_in[rows])
            a_loc.append(a[rows] * a_loc[-1])
        h_in = hcar_ref[...]
        if j == 0:
            h_in = jnp.where(seq_start, 0.0, h_in)
        carries = [h_in]
        for s in range(N_SEG):
            carries.append(a_loc[-1][s:s + 1] * carries[-1] + h_loc[-1][s:s + 1])
        hcar_ref[...] = carries[-1]
        carry = jnp.concatenate(carries[:-1], axis=0)
        hs0 = j * N_SEG * SEG_PITCH
        for r in range(SEG):
            h_r = h_loc[r] + a_loc[r] * carry
            for ls in range(N_LS):
                hs_ref[ls, pl.ds(hs0 + TAIL + r, N_SEG, stride=SEG_PITCH), :] = h_r[:, ls * LANES:(ls + 1) * LANES]
        hh = jnp.concatenate(
            [jnp.concatenate([hs_ref[ls, hs0 + SEG_PITCH * s + TAIL:hs0 + SEG_PITCH * (s + 1), :]
                              for s in range(N_SEG)], axis=0) for ls in range(N_LS)], axis=1)
        gl = lru_ref[0, j * SUB:(j + 1) * SUB, LRU_WIDTH:]
        mlru_ref[j * SUB:(j + 1) * SUB, :] = (hh * jax.nn.gelu(gl, approximate=True)).astype(jnp.bfloat16)

    def first_piece():
        store_slabs()
        conv(0)

    pieces = [first_piece, functools.partial(recur, 0)]
    for j in range(1, FFN_TILE // SUB):
        pieces += [functools.partial(conv, j), functools.partial(recur, j)]

    def out_proj():
        mixed = jnp.concatenate([ohg_ref[0], mlru_ref[...]], axis=1)
        h_ref[...] = x_ref[0] + _dot(mixed, wout_ref[...])

    def ffn_norm():
        hn_ref[...] = _rms(h_ref[...], nw_ref[...]).astype(jnp.bfloat16)

    pieces += [out_proj, ffn_norm]
    assert len(pieces) == n_ff - 1

    @pl.when(t == 0)
    def _():
        hcar_ref[...] = jnp.zeros_like(hcar_ref)
        xl_ref[:, last_rows, :] = jnp.zeros((N_LS, TAIL, LANES), jnp.float32)
        for piece in pieces:
            piece()

    pl.when(t > 0)(functools.partial(_ffn_with_pieces, pieces, d_ff, n_ff, h_ref, hn_ref, acc_ref, wgu_ref, wd_ref,
                                     fw_ref, o_ref))


def _ffn_with_pieces(pieces, d_ff, n_ff, h_ref, hn_ref, acc_ref, wgu_ref, wd_ref, fw_ref, o_ref):
    def gate_up(c):
        gt = _dot(hn_ref[...], wgu_ref[:, c * FF_CHUNK:(c + 1) * FF_CHUNK])
        up = _dot(hn_ref[...], wgu_ref[:, d_ff + c * FF_CHUNK:d_ff + (c + 1) * FF_CHUNK])
        return gt, up

    def activate(gt, up):
        return (gt * jax.nn.sigmoid(gt) * up).astype(jnp.bfloat16)

    act = activate(*gate_up(0))
    for c in range(n_ff - 1):
        gu = gate_up(c + 1)
        down = _dot(act, wd_ref[c * FF_CHUNK:(c + 1) * FF_CHUNK, :])
        if c == 0:
            acc_ref[...] = h_ref[...] + down
        else:
            acc_ref[...] += down
        pieces[c]()
        act = activate(*gu)
    wd_last = wd_ref[(n_ff - 1) * FF_CHUNK:n_ff * FF_CHUNK, :]
    halves = [slice(i * (FFN_TILE // 2), (i + 1) * (FFN_TILE // 2)) for i in range(2)]
    for rows in halves:
        acc_ref[rows, :] += _dot(act[rows], wd_last)
    for rows in halves:
        o_ref[0, rows, :] = _rms(acc_ref[rows, :], fw_ref[...])


def _block_diag_groups(w):
    per = LRU_GROUP // (LRU_WIDTH // LRU_BLOCKS)
    bd = LRU_WIDTH // LRU_BLOCKS
    w = w.reshape(LRU_WIDTH // LRU_GROUP, per, bd, bd)
    eye = jnp.eye(per, dtype=w.dtype)
    return jnp.einsum('gade,ab->gadbe', w, eye).reshape(LRU_WIDTH // LRU_GROUP, LRU_GROUP, LRU_GROUP)


def kernel(x, mix_norm_w, w_in, hg_lb, hg_norm_w, conv_w, conv_b, lru_wa, lru_ba, lru_wx, lru_bx,
           lru_a, w_out, ffn_norm_w, w_gate_up, w_down, final_norm_w):
    B, S, D = x.shape
    assert D == D_MODEL and w_in.shape == (1, D_MODEL, D_IN) and S % SEQ_TILE == 0 and SEQ_TILE == 2 * PROJ_ROWS
    bf = jnp.bfloat16
    d_ff = w_down.shape[1]
    assert d_ff % FF_CHUNK == 0

    win = w_in[0].astype(bf)
    wout = w_out[0].astype(bf)
    wg = jnp.concatenate([_block_diag_groups(lru_wa[0]), _block_diag_groups(lru_wx[0])], axis=-1).astype(bf)
    bg = jnp.stack([lru_ba[0], lru_bx[0]], axis=0)
    row = lambda a: a.reshape(1, -1)
    once = pl.Buffered(1)

    full = lambda shape: pl.BlockSpec(shape, lambda b, s: (0,) * len(shape), pipeline_mode=once)
    n_s = S // SEQ_TILE

    def next_block(b, s):
        wrap = s == n_s - 1
        return (jnp.minimum(jnp.where(wrap, b + 1, b), B - 1), jnp.where(wrap, 0, s + 1) * (SEQ_TILE // PROJ_ROWS), 0)

    o_hg, lru_in = pl.pallas_call(
        _hgrn_kernel,
        grid=(B, n_s),
        in_specs=[
            pl.BlockSpec((1, SEQ_TILE, D), lambda b, s: (b, s, 0)),
            pl.BlockSpec((1, PROJ_ROWS, D), next_block),
            full((1, D)), full((D, D_IN)), full((2, HG_WIDTH)), full((1, HG_DIM)),
        ],
        out_specs=[
            pl.BlockSpec((1, SEQ_TILE, HG_WIDTH), lambda b, s: (b, s, 0)),
            pl.BlockSpec((1, SEQ_TILE, 2 * LRU_WIDTH), lambda b, s: (b, s, 0)),
        ],
        out_shape=[
            jax.ShapeDtypeStruct((B, S, HG_WIDTH), bf),
            jax.ShapeDtypeStruct((B, S, 2 * LRU_WIDTH), jnp.float32),
        ],
        scratch_shapes=[
            pltpu.VMEM((PROJ_ROWS, D_IN), jnp.float32),
            pltpu.VMEM((PROJ_ROWS, 4 * HG_WIDTH), jnp.float32),
            pltpu.VMEM((HG_HEADS, HG_DIM, HG_DIM), jnp.float32),
        ],
        compiler_params=pltpu.CompilerParams(
            dimension_semantics=("arbitrary", "arbitrary"), vmem_limit_bytes=VMEM_LIMIT),
        name="hgrn",
    )(x, x, row(mix_norm_w[0]), win, hg_lb, row(hg_norm_w[0]))

    wgu = w_gate_up[0].astype(bf)
    wd = w_down[0].astype(bf)
    tiles_per_seq = S // FFN_TILE
    n_tiles = B * tiles_per_seq

    def cur(t):
        c = jnp.maximum(t - 1, 0)
        return (c // tiles_per_seq, c % tiles_per_seq, 0)

    def nxt(t):
        n = jnp.minimum(t, n_tiles - 1)
        return (n // tiles_per_seq, n % tiles_per_seq, 0)

    full1 = lambda shape: pl.BlockSpec(shape, lambda t: (0,) * len(shape), pipeline_mode=once)
    out = pl.pallas_call(
        functools.partial(_lru_ffn_kernel, tiles_per_seq=tiles_per_seq),
        grid=(n_tiles + 1,),
        in_specs=[
            pl.BlockSpec((1, FFN_TILE, D), nxt),
            pl.BlockSpec((1, FFN_TILE, HG_WIDTH), nxt),
            pl.BlockSpec((1, FFN_TILE, 2 * LRU_WIDTH), nxt),
            full1((CONV_WIDTH, LRU_WIDTH)), full1((1, LRU_WIDTH)),
            full1((LRU_WIDTH // LRU_GROUP, LRU_GROUP, 2 * LRU_GROUP)), full1((2, LRU_WIDTH)), full1((1, LRU_WIDTH)),
            full1((D_MIX, D)), full1((1, D)), full1((D, 2 * d_ff)), full1((d_ff, D)), full1((1, D)),
        ],
        out_specs=pl.BlockSpec((1, FFN_TILE, D), cur),
        out_shape=jax.ShapeDtypeStruct((B, S, D), jnp.float32),
        scratch_shapes=[
            pltpu.VMEM((N_LS, SLAB_ROWS, LANES), jnp.float32),
            pltpu.VMEM((N_LS, SLAB_ROWS, LANES), jnp.float32),
            pltpu.VMEM((FFN_TILE, LRU_WIDTH), jnp.bfloat16),
            pltpu.VMEM((1, LRU_WIDTH), jnp.float32),
            pltpu.VMEM((FFN_TILE, D), jnp.float32),
            pltpu.VMEM((FFN_TILE, D), jnp.float32),
            pltpu.VMEM((FFN_TILE, D), jnp.bfloat16),
        ],
        compiler_params=pltpu.CompilerParams(
            dimension_semantics=("arbitrary",), vmem_limit_bytes=VMEM_LIMIT),
        name="lru_ffn",
    )(x, o_hg, lru_in, conv_w[0], row(conv_b[0]), wg, bg, row(lru_a[0]), wout, row(ffn_norm_w[0]), wgu, wd,
      row(final_norm_w))
    return out
```

```python
import functools

import jax
import jax.numpy as jnp
from jax import lax
from jax.experimental import pallas as pl
from jax.experimental.pallas import tpu as pltpu

D_MODEL = 1024
HG_HEADS = 4
HG_DIM = 128
HG_WIDTH = HG_HEADS * HG_DIM
HG_BASE = 32
LRU_WIDTH = 512
LRU_BLOCKS = 8
LRU_GROUP = 256
CONV_WIDTH = 4
LRU_C = 8.0
D_MIX = HG_WIDTH + LRU_WIDTH
D_IN = 4 * HG_WIDTH + 2 * LRU_WIDTH
EPS = 1e-6
LOG2E = 1.4426950408889634

SUB = 128
PROJ_ROWS = 512
SEQ_TILE = 1024
FFN_TILE = 512
FF_CHUNK = 256
COL_CHUNK = 256
LANES = 128
ROW_TILE = 8
N_SEG = ROW_TILE
SEG = SUB // N_SEG
TAIL = ROW_TILE
SEG_PITCH = TAIL + SEG
SLAB_ROWS = SEG_PITCH * (FFN_TILE // SEG)
N_LS = LRU_WIDTH // LANES
PHASE_TRAIL = 2
HGRN_YIELDS = 5
VMEM_LIMIT = 56 * 1024 * 1024

_NT = (((1,), (1,)), ((), ()))


def _dot(a, b, dims=None):
    if dims is None:
        return jnp.dot(a, b, preferred_element_type=jnp.float32)
    return lax.dot_general(a, b, dims, preferred_element_type=jnp.float32)


def _rms(x, w):
    return x * lax.rsqrt(jnp.mean(x * x, axis=-1, keepdims=True) + EPS) * w


def _rows(shape):
    return lax.broadcasted_iota(jnp.int32, shape, 0)


def _cumsum_rows(x):
    n, w = x.shape
    sub = _rows(x.shape) % ROW_TILE
    s = 1
    while s < ROW_TILE:
        shifted = pltpu.roll(x.reshape(n // ROW_TILE, ROW_TILE, w), s, axis=1).reshape(n, w)
        x = x + jnp.where(sub >= s, shifted, 0.0)
        s *= 2
    outs = [x[0:ROW_TILE]]
    for t in range(1, n // ROW_TILE):
        outs.append(x[t * ROW_TILE:(t + 1) * ROW_TILE] + outs[-1][ROW_TILE - 1:ROW_TILE])
    return jnp.concatenate(outs, axis=0)


def _bcast_rows(r, n):
    return jnp.broadcast_to(r, (n, r.shape[1]))


def _hgrn_kernel(x_ref, xnext_ref, nw_ref, win_ref, lb_ref, hgw_ref, ohg_ref, lru_ref,
                 proj0_ref, proj1_ref, state_ref):
    bi = pl.program_id(0)
    si = pl.program_id(1)
    proj_refs = (proj0_ref, proj1_ref)

    @pl.when(si == 0)
    def _():
        state_ref[...] = jnp.zeros_like(state_ref)

    nw = nw_ref[...]
    n_hg = 4 * HG_WIDTH

    lbp = lb_ref[...]
    lbe = jnp.exp(lbp - jnp.max(lbp, axis=0, keepdims=True))
    lb = lbe[0:1] / jnp.sum(lbe, axis=0, keepdims=True)
    hgw = hgw_ref[...]
    tri_r = _rows((SUB, 2 * SUB))
    tri_c = lax.broadcasted_iota(jnp.int32, (SUB, 2 * SUB), 1) % SUB
    tri2 = jnp.where(tri_c <= tri_r, 1.0, 0.0).astype(jnp.bfloat16)

    row = _rows((SUB, SUB))
    col = lax.broadcasted_iota(jnp.int32, (SUB, SUB), 1)
    mask0 = (row // HG_BASE == col // HG_BASE) & (col <= row)
    mask1 = ((row // (2 * HG_BASE) == col // (2 * HG_BASE))
             & (row % (2 * HG_BASE) >= HG_BASE) & (col % (2 * HG_BASE) < HG_BASE))

    def in_proj(load_x, parity, store_lru):
        proj_ref = proj_refs[parity]
        cell = []

        def norm():
            cell.append(_rms(load_x(), nw).astype(jnp.bfloat16))

        def hg_chunk(c):
            proj_ref[:, c:c + COL_CHUNK] = _dot(cell[0], win_ref[:, c:c + COL_CHUNK])

        def lru_chunk(c):
            store_lru(c - n_hg, _dot(cell[0], win_ref[:, c:c + COL_CHUNK]))

        return ([norm] + [functools.partial(hg_chunk, c) for c in range(0, n_hg, COL_CHUNK)]
                + [functools.partial(lru_chunk, c) for c in range(n_hg, D_IN, COL_CHUNK)])

    def sub_tile(blk, j):
        proj_ref = proj_refs[blk % 2]
        p0 = j * SUB
        r0 = blk * PROJ_ROWS + p0
        half = SUB // 2
        q = proj_ref[p0:p0 + SUB, 0:HG_WIDTH]
        fl = proj_ref[p0:p0 + SUB, HG_WIDTH:2 * HG_WIDTH]
        v = proj_ref[p0:p0 + SUB, 2 * HG_WIDTH:3 * HG_WIDTH].astype(jnp.bfloat16)
        q = q * jax.nn.sigmoid(q)
        f = lb + (1.0 - lb) * jax.nn.sigmoid(fl)
        k = 1.0 - f
        lf = jnp.log(f) * LOG2E
        yield
        beta = _cumsum_rows(lf)
        yield
        r31 = beta[HG_BASE - 1:HG_BASE]
        r63 = beta[2 * HG_BASE - 1:2 * HG_BASE]
        r95 = beta[3 * HG_BASE - 1:3 * HG_BASE]
        r127 = beta[4 * HG_BASE - 1:4 * HG_BASE]
        ref0 = jnp.concatenate([jnp.zeros((HG_BASE, HG_WIDTH), jnp.float32), _bcast_rows(r31, HG_BASE),
                                _bcast_rows(r63, HG_BASE), _bcast_rows(r95, HG_BASE)], axis=0)
        ref1 = jnp.concatenate([_bcast_rows(r31, 2 * HG_BASE), _bcast_rows(r95, 2 * HG_BASE)], axis=0)
        b0 = beta - ref0
        q0 = (q * jnp.exp2(b0)).astype(jnp.bfloat16)
        k0 = (k * jnp.exp2(-b0)).astype(jnp.bfloat16)
        k1 = (k * jnp.exp2(jnp.minimum(ref1 - beta, 0.0))).astype(jnp.bfloat16)
        yield
        q2 = (q[half:] * jnp.exp2(beta[half:] - r63)).astype(jnp.bfloat16)
        k2 = jnp.concatenate([(k[:half] * jnp.exp2(r63 - beta[:half])).astype(jnp.bfloat16),
                              jnp.zeros((half, HG_WIDTH), jnp.bfloat16)], axis=0)
        qs = (q * jnp.exp2(beta)).astype(jnp.bfloat16)
        ks = (k * jnp.exp2(r127 - beta)).astype(jnp.bfloat16)
        dec_end = jnp.exp2(r127)
        yield
        heads = [slice(h * HG_DIM, (h + 1) * HG_DIM) for h in range(HG_HEADS)]
        s01 = [_dot(q0[:, hs], jnp.concatenate([k0[:, hs], k1[:, hs]], axis=0), _NT) for hs in heads]
        s2 = [_dot(q2[:, hs], k2[:, hs], _NT) for hs in heads]
        vt = [v[:, hs].T for hs in heads]
        inc = [_dot(vt[h], ks[:, hs]) for h, hs in enumerate(heads)]
        yield
        g = proj_ref[p0:p0 + SUB, 3 * HG_WIDTH:4 * HG_WIDTH]
        gate = g * jax.nn.sigmoid(g)
        for h, hs in enumerate(heads):
            sc = jnp.where(mask0, s01[h][:, :SUB], 0.0) + jnp.where(mask1, s01[h][:, SUB:], 0.0)
            sc = jnp.concatenate([sc[:half], sc[half:] + s2[h]], axis=0)
            st = state_ref[h]
            o = _dot(jnp.concatenate([sc.astype(jnp.bfloat16), qs[:, hs]], axis=1),
                     jnp.concatenate([vt[h], st.astype(jnp.bfloat16)], axis=1), _NT)
            state_ref[h] = st * dec_end[:, hs] + inc[h]
            o = o * lax.rsqrt(jnp.mean(o * o, axis=-1, keepdims=True) + EPS) * hgw
            ohg_ref[0, r0:r0 + SUB, hs] = (o * gate[:, hs]).astype(jnp.bfloat16)

    def lru_to_scratch(off, val):
        proj0_ref[:, n_hg + off:n_hg + off + COL_CHUNK] = val

    def lru_to_output(off, val):
        lru_ref[0, PROJ_ROWS:2 * PROJ_ROWS, off:off + COL_CHUNK] = val

    def flush_first_block():
        lru_ref[0, 0:PROJ_ROWS, :] = proj0_ref[:, n_hg:D_IN]

    @pl.when((bi == 0) & (si == 0))
    def _():
        for step in in_proj(lambda: x_ref[0, 0:PROJ_ROWS, :], 0, lru_to_scratch):
            step()

    phases = (
        ([flush_first_block] + in_proj(lambda: x_ref[0, PROJ_ROWS:2 * PROJ_ROWS, :], 1, lru_to_output), 0),
        (in_proj(lambda: xnext_ref[0], 0, lru_to_scratch), PHASE_TRAIL),
    )
    for blk, (fill, trail) in enumerate(phases):
        spread = fill[:len(fill) - trail]
        points = [sub_tile(blk, j) for j in range(PROJ_ROWS // SUB)]
        n_points = len(points) * HGRN_YIELDS
        seen = done = 0
        for gen in points:
            for _ in gen:
                seen += 1
                upto = -(-seen * len(spread) // n_points)
                for step in spread[done:upto]:
                    step()
                done = upto
        assert seen == n_points and done == len(spread)
        for step in fill[len(fill) - trail:]:
            step()


def _lru_ffn_kernel(x_ref, ohg_ref, lru_ref, cw_ref, cb_ref, wg_ref, bg_ref, ap_ref, wout_ref, nw_ref, wgu_ref,
                    wd_ref, fw_ref, o_ref, xl_ref, hs_ref, mlru_ref, hcar_ref, acc_ref, h_ref, hn_ref, *,
                    tiles_per_seq):
    t = pl.program_id(0)
    seq_start = (t % tiles_per_seq) == 0
    d_ff = wd_ref.shape[0]
    n_ff = d_ff // FF_CHUNK
    last_rows = slice(SLAB_ROWS - TAIL, SLAB_ROWS)

    @pl.when(t == 0)
    def _():
        h_ref[...] = jnp.zeros_like(h_ref)
        hn_ref[...] = jnp.zeros_like(hn_ref)
        hcar_ref[...] = jnp.zeros_like(hcar_ref)
        xl_ref[:, last_rows, :] = jnp.zeros((N_LS, TAIL, LANES), jnp.float32)

    neg_c_sp2 = (-LRU_C * LOG2E) * jax.nn.softplus(-ap_ref[...])
    cw_rows = [jnp.broadcast_to(cw_ref[k:k + 1, :], (N_SEG, LRU_WIDTH)) for k in range(CONV_WIDTH)]
    cb_rows = jnp.broadcast_to(cb_ref[...], (N_SEG, LRU_WIDTH))
    bg = bg_ref[...]

    def store_slabs():
        val = lru_ref[0, :, 0:LRU_WIDTH]
        n_seg = FFN_TILE // SEG
        for ls in range(N_LS):
            lanes = slice(ls * LANES, (ls + 1) * LANES)
            xl_ref[ls, 0:TAIL, :] = jnp.where(seq_start, 0.0, xl_ref[ls, last_rows, :])
            for s in range(n_seg):
                xl_ref[ls, SEG_PITCH * s + TAIL:SEG_PITCH * (s + 1), :] = val[SEG * s:SEG * (s + 1), lanes]
                if s + 1 < n_seg:
                    xl_ref[ls, SEG_PITCH * (s + 1):SEG_PITCH * (s + 1) + TAIL, :] = (
                        val[SEG * (s + 1) - TAIL:SEG * (s + 1), lanes])

    conv_out = {}

    def conv(j):
        slab_start = SEG_PITCH * j * N_SEG + TAIL

        def x_slab(r):
            return jnp.concatenate([xl_ref[ls, pl.ds(slab_start + r, N_SEG, stride=SEG_PITCH), :]
                                    for ls in range(N_LS)], axis=1)

        xs = {r: x_slab(r) for r in range(1 - CONV_WIDTH, SEG)}
        xc_slabs = []
        for r in range(SEG):
            acc = cb_rows
            for tap in range(CONV_WIDTH):
                acc = acc + xs[r - (CONV_WIDTH - 1 - tap)] * cw_rows[tap]
            xc_slabs.append(acc)
        conv_out[j] = jnp.concatenate(xc_slabs, axis=0)

    def recur(j):
        xc = conv_out[j]
        xcb = xc.astype(jnp.bfloat16)
        gates = [_dot(xcb[:, gi * LRU_GROUP:(gi + 1) * LRU_GROUP], wg_ref[gi]) for gi in range(LRU_WIDTH // LRU_GROUP)]
        rr = jnp.concatenate([gt[:, :LRU_GROUP] for gt in gates], axis=1)
        ii = jnp.concatenate([gt[:, LRU_GROUP:] for gt in gates], axis=1)
        rr = jax.nn.sigmoid(rr + bg[0:1])
        ii = jax.nn.sigmoid(ii + bg[1:2])
        a = jnp.exp2(rr * neg_c_sp2)
        b_in = jnp.sqrt(1.0 - a * a) * (ii * xc)
        h_loc, a_loc = [b_in[0:N_SEG]], [a[0:N_SEG]]
        for r in range(1, SEG):
            rows = slice(r * N_SEG, (r + 1) * N_SEG)
            h_loc.append(a[rows] * h_loc[-1] + b_in[rows])
            a_loc.append(a[rows] * a_loc[-1])
        h_in = hcar_ref[...]
        if j == 0:
            h_in = jnp.where(seq_start, 0.0, h_in)
        carries = [h_in]
        for s in range(N_SEG):
            carries.append(a_loc[-1][s:s + 1] * carries[-1] + h_loc[-1][s:s + 1])
        hcar_ref[...] = carries[-1]
        carry = jnp.concatenate(carries[:-1], axis=0)
        hs0 = j * N_SEG * SEG_PITCH
        for r in range(SEG):
            h_r = h_loc[r] + a_loc[r] * carry
            for ls in range(N_LS):
                hs_ref[ls, pl.ds(hs0 + TAIL + r, N_SEG, stride=SEG_PITCH), :] = h_r[:, ls * LANES:(ls + 1) * LANES]
        hh = jnp.concatenate(
            [jnp.concatenate([hs_ref[ls, hs0 + SEG_PITCH * s + TAIL:hs0 + SEG_PITCH * (s + 1), :]
                              for s in range(N_SEG)], axis=0) for ls in range(N_LS)], axis=1)
        gl = lru_ref[0, j * SUB:(j + 1) * SUB, LRU_WIDTH:]
        mlru_ref[j * SUB:(j + 1) * SUB, :] = (hh * jax.nn.gelu(gl, approximate=True)).astype(jnp.bfloat16)

    def first_piece():
        store_slabs()
        conv(0)

    pieces = [first_piece, functools.partial(recur, 0)]
    for j in range(1, FFN_TILE // SUB):
        pieces += [functools.partial(conv, j), functools.partial(recur, j)]

    def out_proj():
        mixed = jnp.concatenate([ohg_ref[0], mlru_ref[...]], axis=1)
        h_ref[...] = x_ref[0] + _dot(mixed, wout_ref[...])

    def ffn_norm():
        hn_ref[...] = _rms(h_ref[...], nw_ref[...]).astype(jnp.bfloat16)

    pieces += [out_proj, ffn_norm]
    assert len(pieces) == n_ff - 1

    def gate_up(c):
        gt = _dot(hn_ref[...], wgu_ref[:, c * FF_CHUNK:(c + 1) * FF_CHUNK])
        up = _dot(hn_ref[...], wgu_ref[:, d_ff + c * FF_CHUNK:d_ff + (c + 1) * FF_CHUNK])
        return gt, up

    def activate(gt, up):
        return (gt * jax.nn.sigmoid(gt) * up).astype(jnp.bfloat16)

    act = activate(*gate_up(0))
    for c in range(n_ff - 1):
        gu = gate_up(c + 1)
        down = _dot(act, wd_ref[c * FF_CHUNK:(c + 1) * FF_CHUNK, :])
        if c == 0:
            acc_ref[...] = h_ref[...] + down
        else:
            acc_ref[...] += down
        pieces[c]()
        act = activate(*gu)
    wd_last = wd_ref[(n_ff - 1) * FF_CHUNK:n_ff * FF_CHUNK, :]
    halves = [slice(i * (FFN_TILE // 2), (i + 1) * (FFN_TILE // 2)) for i in range(2)]
    for rows in halves:
        acc_ref[rows, :] += _dot(act[rows], wd_last)
    for rows in halves:
        o_ref[0, rows, :] = _rms(acc_ref[rows, :], fw_ref[...])


def _block_diag_groups(w):
    per = LRU_GROUP // (LRU_WIDTH // LRU_BLOCKS)
    bd = LRU_WIDTH // LRU_BLOCKS
    w = w.reshape(LRU_WIDTH // LRU_GROUP, per, bd, bd)
    eye = jnp.eye(per, dtype=w.dtype)
    return jnp.einsum('gade,ab->gadbe', w, eye).reshape(LRU_WIDTH // LRU_GROUP, LRU_GROUP, LRU_GROUP)


def kernel(x, mix_norm_w, w_in, hg_lb, hg_norm_w, conv_w, conv_b, lru_wa, lru_ba, lru_wx, lru_bx,
           lru_a, w_out, ffn_norm_w, w_gate_up, w_down, final_norm_w):
    B, S, D = x.shape
    assert D == D_MODEL and w_in.shape == (1, D_MODEL, D_IN) and S % SEQ_TILE == 0 and SEQ_TILE == 2 * PROJ_ROWS
    bf = jnp.bfloat16
    d_ff = w_down.shape[1]
    assert d_ff % FF_CHUNK == 0

    win = w_in[0].astype(bf)
    wout = w_out[0].astype(bf)
    wg = jnp.concatenate([_block_diag_groups(lru_wa[0]), _block_diag_groups(lru_wx[0])], axis=-1).astype(bf)
    bg = jnp.stack([lru_ba[0], lru_bx[0]], axis=0)
    row = lambda a: a.reshape(1, -1)
    once = pl.Buffered(1)

    full = lambda shape: pl.BlockSpec(shape, lambda b, s: (0,) * len(shape), pipeline_mode=once)
    n_s = S // SEQ_TILE

    def next_block(b, s):
        wrap = s == n_s - 1
        return (jnp.minimum(jnp.where(wrap, b + 1, b), B - 1), jnp.where(wrap, 0, s + 1) * (SEQ_TILE // PROJ_ROWS), 0)

    o_hg, lru_in = pl.pallas_call(
        _hgrn_kernel,
        grid=(B, n_s),
        in_specs=[
            pl.BlockSpec((1, SEQ_TILE, D), lambda b, s: (b, s, 0)),
            pl.BlockSpec((1, PROJ_ROWS, D), next_block),
            full((1, D)), full((D, D_IN)), full((2, HG_WIDTH)), full((1, HG_DIM)),
        ],
        out_specs=[
            pl.BlockSpec((1, SEQ_TILE, HG_WIDTH), lambda b, s: (b, s, 0)),
            pl.BlockSpec((1, SEQ_TILE, 2 * LRU_WIDTH), lambda b, s: (b, s, 0)),
        ],
        out_shape=[
            jax.ShapeDtypeStruct((B, S, HG_WIDTH), bf),
            jax.ShapeDtypeStruct((B, S, 2 * LRU_WIDTH), jnp.float32),
        ],
        scratch_shapes=[
            pltpu.VMEM((PROJ_ROWS, D_IN), jnp.float32),
            pltpu.VMEM((PROJ_ROWS, 4 * HG_WIDTH), jnp.float32),
            pltpu.VMEM((HG_HEADS, HG_DIM, HG_DIM), jnp.float32),
        ],
        compiler_params=pltpu.CompilerParams(
            dimension_semantics=("arbitrary", "arbitrary"), vmem_limit_bytes=VMEM_LIMIT),
        name="hgrn",
    )(x, x, row(mix_norm_w[0]), win, hg_lb, row(hg_norm_w[0]))

    wgu = w_gate_up[0].astype(bf)
    wd = w_down[0].astype(bf)
    tiles_per_seq = S // FFN_TILE
    n_tiles = B * tiles_per_seq

    def cur(t):
        c = jnp.maximum(t - 1, 0)
        return (c // tiles_per_seq, c % tiles_per_seq, 0)

    def nxt(t):
        n = jnp.minimum(t, n_tiles - 1)
        return (n // tiles_per_seq, n % tiles_per_seq, 0)

    full1 = lambda shape: pl.BlockSpec(shape, lambda t: (0,) * len(shape), pipeline_mode=once)
    out = pl.pallas_call(
        functools.partial(_lru_ffn_kernel, tiles_per_seq=tiles_per_seq),
        grid=(n_tiles + 1,),
        in_specs=[
            pl.BlockSpec((1, FFN_TILE, D), nxt),
            pl.BlockSpec((1, FFN_TILE, HG_WIDTH), nxt),
            pl.BlockSpec((1, FFN_TILE, 2 * LRU_WIDTH), nxt),
            full1((CONV_WIDTH, LRU_WIDTH)), full1((1, LRU_WIDTH)),
            full1((LRU_WIDTH // LRU_GROUP, LRU_GROUP, 2 * LRU_GROUP)), full1((2, LRU_WIDTH)), full1((1, LRU_WIDTH)),
            full1((D_MIX, D)), full1((1, D)), full1((D, 2 * d_ff)), full1((d_ff, D)), full1((1, D)),
        ],
        out_specs=pl.BlockSpec((1, FFN_TILE, D), cur),
        out_shape=jax.ShapeDtypeStruct((B, S, D), jnp.float32),
        scratch_shapes=[
            pltpu.VMEM((N_LS, SLAB_ROWS, LANES), jnp.float32),
            pltpu.VMEM((N_LS, SLAB_ROWS, LANES), jnp.float32),
            pltpu.VMEM((FFN_TILE, LRU_WIDTH), jnp.bfloat16),
            pltpu.VMEM((1, LRU_WIDTH), jnp.float32),
            pltpu.VMEM((FFN_TILE, D), jnp.float32),
            pltpu.VMEM((FFN_TILE, D), jnp.float32),
            pltpu.VMEM((FFN_TILE, D), jnp.bfloat16),
        ],
        compiler_params=pltpu.CompilerParams(
            dimension_semantics=("arbitrary",), vmem_limit_bytes=VMEM_LIMIT),
        name="lru_ffn",
    )(x, o_hg, lru_in, conv_w[0], row(conv_b[0]), wg, bg, row(lru_a[0]), wout, row(ffn_norm_w[0]), wgu, wd,
      row(final_norm_w))
    return out
```

```python
import functools

import jax
import jax.numpy as jnp
from jax import lax
from jax.experimental import pallas as pl
from jax.experimental.pallas import tpu as pltpu

D_MODEL = 1024
HG_HEADS = 4
HG_DIM = 128
HG_WIDTH = HG_HEADS * HG_DIM
HG_BASE = 32
LRU_WIDTH = 512
LRU_BLOCKS = 8
LRU_GROUP = 256
CONV_WIDTH = 4
LRU_C = 8.0
D_MIX = HG_WIDTH + LRU_WIDTH
D_IN = 4 * HG_WIDTH + 2 * LRU_WIDTH
EPS = 1e-6
LOG2E = 1.4426950408889634

SUB = 128
PROJ_ROWS = 512
SEQ_TILE = 1024
FFN_TILE = 512
FF_CHUNK = 256
COL_CHUNK = 256
LANES = 128
ROW_TILE = 8
N_SEG = ROW_TILE
SEG = SUB // N_SEG
TAIL = ROW_TILE
SEG_PITCH = TAIL + SEG
SLAB_ROWS = SEG_PITCH * (FFN_TILE // SEG)
N_LS = LRU_WIDTH // LANES
PHASE_TRAIL = 2
HGRN_YIELDS = 5
VMEM_LIMIT = 56 * 1024 * 1024

_NT = (((1,), (1,)), ((), ()))


def _dot(a, b, dims=None):
    if dims is None:
        return jnp.dot(a, b, preferred_element_type=jnp.float32)
    return lax.dot_general(a, b, dims, preferred_element_type=jnp.float32)


def _rms(x, w):
    return x * lax.rsqrt(jnp.mean(x * x, axis=-1, keepdims=True) + EPS) * w


def _rows(shape):
    return lax.broadcasted_iota(jnp.int32, shape, 0)


def _cumsum_rows(x):
    n, w = x.shape
    sub = _rows(x.shape) % ROW_TILE
    s = 1
    while s < ROW_TILE:
        shifted = pltpu.roll(x.reshape(n // ROW_TILE, ROW_TILE, w), s, axis=1).reshape(n, w)
        x = x + jnp.where(sub >= s, shifted, 0.0)
        s *= 2
    outs = [x[0:ROW_TILE]]
    for t in range(1, n // ROW_TILE):
        outs.append(x[t * ROW_TILE:(t + 1) * ROW_TILE] + outs[-1][ROW_TILE - 1:ROW_TILE])
    return jnp.concatenate(outs, axis=0)


def _bcast_rows(r, n):
    return jnp.broadcast_to(r, (n, r.shape[1]))


def _hgrn_kernel(x_ref, xnext_ref, nw_ref, win_ref, lb_ref, hgw_ref, ohg_ref, lru_ref,
                 proj0_ref, proj1_ref, state_ref):
    bi = pl.program_id(0)
    si = pl.program_id(1)
    proj_refs = (proj0_ref, proj1_ref)

    @pl.when(si == 0)
    def _():
        state_ref[...] = jnp.zeros_like(state_ref)

    nw = nw_ref[...]
    n_hg = 4 * HG_WIDTH

    lbp = lb_ref[...]
    lbe = jnp.exp(lbp - jnp.max(lbp, axis=0, keepdims=True))
    lb = lbe[0:1] / jnp.sum(lbe, axis=0, keepdims=True)
    hgw = hgw_ref[...]

    row = _rows((SUB, SUB))
    col = lax.broadcasted_iota(jnp.int32, (SUB, SUB), 1)
    mask0 = (row // HG_BASE == col // HG_BASE) & (col <= row)
    mask1 = ((row // (2 * HG_BASE) == col // (2 * HG_BASE))
             & (row % (2 * HG_BASE) >= HG_BASE) & (col % (2 * HG_BASE) < HG_BASE))

    def in_proj(load_x, parity, store_lru):
        proj_ref = proj_refs[parity]
        cell = []

        def norm():
            cell.append(_rms(load_x(), nw).astype(jnp.bfloat16))

        def hg_chunk(c):
            proj_ref[:, c:c + COL_CHUNK] = _dot(cell[0], win_ref[:, c:c + COL_CHUNK])

        def lru_chunk(c):
            store_lru(c - n_hg, _dot(cell[0], win_ref[:, c:c + COL_CHUNK]))

        return ([norm] + [functools.partial(hg_chunk, c) for c in range(0, n_hg, COL_CHUNK)]
                + [functools.partial(lru_chunk, c) for c in range(n_hg, D_IN, COL_CHUNK)])

    def sub_tile(blk, j):
        proj_ref = proj_refs[blk % 2]
        p0 = j * SUB
        r0 = blk * PROJ_ROWS + p0
        half = SUB // 2
        q = proj_ref[p0:p0 + SUB, 0:HG_WIDTH]
        fl = proj_ref[p0:p0 + SUB, HG_WIDTH:2 * HG_WIDTH]
        v = proj_ref[p0:p0 + SUB, 2 * HG_WIDTH:3 * HG_WIDTH].astype(jnp.bfloat16)
        q = q * jax.nn.sigmoid(q)
        f = lb + (1.0 - lb) * jax.nn.sigmoid(fl)
        k = 1.0 - f
        lf = jnp.log(f) * LOG2E
        yield
        beta = _cumsum_rows(lf)
        yield
        r31 = beta[HG_BASE - 1:HG_BASE]
        r63 = beta[2 * HG_BASE - 1:2 * HG_BASE]
        r95 = beta[3 * HG_BASE - 1:3 * HG_BASE]
        r127 = beta[4 * HG_BASE - 1:4 * HG_BASE]
        ref0 = jnp.concatenate([jnp.zeros((HG_BASE, HG_WIDTH), jnp.float32), _bcast_rows(r31, HG_BASE),
                                _bcast_rows(r63, HG_BASE), _bcast_rows(r95, HG_BASE)], axis=0)
        ref1 = jnp.concatenate([_bcast_rows(r31, 2 * HG_BASE), _bcast_rows(r95, 2 * HG_BASE)], axis=0)
        b0 = beta - ref0
        q0 = (q * jnp.exp2(b0)).astype(jnp.bfloat16)
        k0 = (k * jnp.exp2(-b0)).astype(jnp.bfloat16)
        k1 = (k * jnp.exp2(jnp.minimum(ref1 - beta, 0.0))).astype(jnp.bfloat16)
        yield
        q2 = (q[half:] * jnp.exp2(beta[half:] - r63)).astype(jnp.bfloat16)
        k2 = jnp.concatenate([(k[:half] * jnp.exp2(r63 - beta[:half])).astype(jnp.bfloat16),
                              jnp.zeros((half, HG_WIDTH), jnp.bfloat16)], axis=0)
        qs = (q * jnp.exp2(beta)).astype(jnp.bfloat16)
        ks = (k * jnp.exp2(r127 - beta)).astype(jnp.bfloat16)
        dec_end = jnp.exp2(r127)
        yield
        heads = [slice(h * HG_DIM, (h + 1) * HG_DIM) for h in range(HG_HEADS)]
        s01 = [_dot(q0[:, hs], jnp.concatenate([k0[:, hs], k1[:, hs]], axis=0), _NT) for hs in heads]
        s2 = [_dot(q2[:, hs], k2[:, hs], _NT) for hs in heads]
        vt = [v[:, hs].T for hs in heads]
        inc = [_dot(vt[h], ks[:, hs]) for h, hs in enumerate(heads)]
        yield
        g = proj_ref[p0:p0 + SUB, 3 * HG_WIDTH:4 * HG_WIDTH]
        gate = g * jax.nn.sigmoid(g)
        for h, hs in enumerate(heads):
            sc = jnp.where(mask0, s01[h][:, :SUB], 0.0) + jnp.where(mask1, s01[h][:, SUB:], 0.0)
            sc = jnp.concatenate([sc[:half], sc[half:] + s2[h]], axis=0)
            st = state_ref[h]
            o = _dot(jnp.concatenate([sc.astype(jnp.bfloat16), qs[:, hs]], axis=1),
                     jnp.concatenate([vt[h], st.astype(jnp.bfloat16)], axis=1), _NT)
            state_ref[h] = st * dec_end[:, hs] + inc[h]
            o = o * lax.rsqrt(jnp.mean(o * o, axis=-1, keepdims=True) + EPS) * hgw
            ohg_ref[0, r0:r0 + SUB, hs] = (o * gate[:, hs]).astype(jnp.bfloat16)

    def lru_to_scratch(off, val):
        proj0_ref[:, n_hg + off:n_hg + off + COL_CHUNK] = val

    def lru_to_output(off, val):
        lru_ref[0, PROJ_ROWS:2 * PROJ_ROWS, off:off + COL_CHUNK] = val

    def flush_first_block():
        lru_ref[0, 0:PROJ_ROWS, :] = proj0_ref[:, n_hg:D_IN]

    @pl.when((bi == 0) & (si == 0))
    def _():
        for step in in_proj(lambda: x_ref[0, 0:PROJ_ROWS, :], 0, lru_to_scratch):
            step()

    norm_next, *proj_next = in_proj(lambda: xnext_ref[0], 0, lru_to_scratch)
    norm_second, *proj_second = in_proj(lambda: x_ref[0, PROJ_ROWS:2 * PROJ_ROWS, :], 1, lru_to_output)
    phases = (
        ([norm_second, flush_first_block] + proj_second + [norm_next], 0),
        (proj_next, PHASE_TRAIL),
    )
    for blk, (fill, trail) in enumerate(phases):
        spread = fill[:len(fill) - trail]
        points = [sub_tile(blk, j) for j in range(PROJ_ROWS // SUB)]
        n_points = len(points) * HGRN_YIELDS
        seen = done = 0
        for gen in points:
            for _ in gen:
                seen += 1
                upto = -(-seen * len(spread) // n_points)
                for step in spread[done:upto]:
                    step()
                done = upto
        assert seen == n_points and done == len(spread)
        for step in fill[len(fill) - trail:]:
            step()


def _lru_ffn_kernel(x_ref, ohg_ref, lru_ref, cw_ref, cb_ref, wg_ref, bg_ref, ap_ref, wout_ref, nw_ref, wgu_ref,
                    wd_ref, fw_ref, o_ref, xl_ref, hs_ref, mlru_ref, hcar_ref, acc_ref, h_ref, hn_ref, *,
                    tiles_per_seq):
    t = pl.program_id(0)
    seq_start = (t % tiles_per_seq) == 0
    d_ff = wd_ref.shape[0]
    n_ff = d_ff // FF_CHUNK
    last_rows = slice(SLAB_ROWS - TAIL, SLAB_ROWS)

    @pl.when(t == 0)
    def _():
        h_ref[...] = jnp.zeros_like(h_ref)
        hn_ref[...] = jnp.zeros_like(hn_ref)
        hcar_ref[...] = jnp.zeros_like(hcar_ref)
        xl_ref[:, last_rows, :] = jnp.zeros((N_LS, TAIL, LANES), jnp.float32)

    neg_c_sp2 = (-LRU_C * LOG2E) * jax.nn.softplus(-ap_ref[...])
    cw_rows = [jnp.broadcast_to(cw_ref[k:k + 1, :], (N_SEG, LRU_WIDTH)) for k in range(CONV_WIDTH)]
    cb_rows = jnp.broadcast_to(cb_ref[...], (N_SEG, LRU_WIDTH))
    bg = bg_ref[...]

    def store_slabs():
        val = lru_ref[0, :, 0:LRU_WIDTH]
        n_seg = FFN_TILE // SEG
        for ls in range(N_LS):
            lanes = slice(ls * LANES, (ls + 1) * LANES)
            xl_ref[ls, 0:TAIL, :] = jnp.where(seq_start, 0.0, xl_ref[ls, last_rows, :])
            for s in range(n_seg):
                xl_ref[ls, SEG_PITCH * s + TAIL:SEG_PITCH * (s + 1), :] = val[SEG * s:SEG * (s + 1), lanes]
                if s + 1 < n_seg:
                    xl_ref[ls, SEG_PITCH * (s + 1):SEG_PITCH * (s + 1) + TAIL, :] = (
                        val[SEG * (s + 1) - TAIL:SEG * (s + 1), lanes])

    conv_out = {}

    def conv(j):
        slab_start = SEG_PITCH * j * N_SEG + TAIL

        def x_slab(r):
            return jnp.concatenate([xl_ref[ls, pl.ds(slab_start + r, N_SEG, stride=SEG_PITCH), :]
                                    for ls in range(N_LS)], axis=1)

        xs = {r: x_slab(r) for r in range(1 - CONV_WIDTH, SEG)}
        xc_slabs = []
        for r in range(SEG):
            acc = cb_rows
            for tap in range(CONV_WIDTH):
                acc = acc + xs[r - (CONV_WIDTH - 1 - tap)] * cw_rows[tap]
            xc_slabs.append(acc)
        conv_out[j] = jnp.concatenate(xc_slabs, axis=0)

    def recur(j):
        xc = conv_out[j]
        xcb = xc.astype(jnp.bfloat16)
        gates = [_dot(xcb[:, gi * LRU_GROUP:(gi + 1) * LRU_GROUP], wg_ref[gi]) for gi in range(LRU_WIDTH // LRU_GROUP)]
        rr = jnp.concatenate([gt[:, :LRU_GROUP] for gt in gates], axis=1)
        ii = jnp.concatenate([gt[:, LRU_GROUP:] for gt in gates], axis=1)
        rr = jax.nn.sigmoid(rr + bg[0:1])
        ii = jax.nn.sigmoid(ii + bg[1:2])
        a = jnp.exp2(rr * neg_c_sp2)
        b_in = jnp.sqrt(1.0 - a * a) * (ii * xc)
        h_loc, a_loc = [b_in[0:N_SEG]], [a[0:N_SEG]]
        for r in range(1, SEG):
            rows = slice(r * N_SEG, (r + 1) * N_SEG)
            h_loc.append(a[rows] * h_loc[-1] + b_in[rows])
            a_loc.append(a[rows] * a_loc[-1])
        h_in = hcar_ref[...]
        if j == 0:
            h_in = jnp.where(seq_start, 0.0, h_in)
        carries = [h_in]
        for s in range(N_SEG):
            carries.append(a_loc[-1][s:s + 1] * carries[-1] + h_loc[-1][s:s + 1])
        hcar_ref[...] = carries[-1]
        carry = jnp.concatenate(carries[:-1], axis=0)
        hs0 = j * N_SEG * SEG_PITCH
        for r in range(SEG):
            h_r = h_loc[r] + a_loc[r] * carry
            for ls in range(N_LS):
                hs_ref[ls, pl.ds(hs0 + TAIL + r, N_SEG, stride=SEG_PITCH), :] = h_r[:, ls * LANES:(ls + 1) * LANES]
        hh = jnp.concatenate(
            [jnp.concatenate([hs_ref[ls, hs0 + SEG_PITCH * s + TAIL:hs0 + SEG_PITCH * (s + 1), :]
                              for s in range(N_SEG)], axis=0) for ls in range(N_LS)], axis=1)
        gl = lru_ref[0, j * SUB:(j + 1) * SUB, LRU_WIDTH:]
        mlru_ref[j * SUB:(j + 1) * SUB, :] = (hh * jax.nn.gelu(gl, approximate=True)).astype(jnp.bfloat16)

    def first_piece():
        store_slabs()
        conv(0)

    pieces = [first_piece, functools.partial(recur, 0)]
    for j in range(1, FFN_TILE // SUB):
        pieces += [functools.partial(conv, j), functools.partial(recur, j)]

    def out_proj():
        mixed = jnp.concatenate([ohg_ref[0], mlru_ref[...]], axis=1)
        h_ref[...] = x_ref[0] + _dot(mixed, wout_ref[...])

    def ffn_norm():
        hn_ref[...] = _rms(h_ref[...], nw_ref[...]).astype(jnp.bfloat16)

    pieces += [out_proj, ffn_norm]
    assert len(pieces) == n_ff - 1

    def gate_up(c):
        gt = _dot(hn_ref[...], wgu_ref[:, c * FF_CHUNK:(c + 1) * FF_CHUNK])
        up = _dot(hn_ref[...], wgu_ref[:, d_ff + c * FF_CHUNK:d_ff + (c + 1) * FF_CHUNK])
        return gt, up

    def activate(gt, up):
        return (gt * jax.nn.sigmoid(gt) * up).astype(jnp.bfloat16)

    act = activate(*gate_up(0))
    for c in range(n_ff - 1):
        gu = gate_up(c + 1)
        down = _dot(act, wd_ref[c * FF_CHUNK:(c + 1) * FF_CHUNK, :])
        if c == 0:
            acc_ref[...] = h_ref[...] + down
        else:
            acc_ref[...] += down
        pieces[c]()
        act = activate(*gu)
    wd_last = wd_ref[(n_ff - 1) * FF_CHUNK:n_ff * FF_CHUNK, :]
    halves = [slice(i * (FFN_TILE // 2), (i + 1) * (FFN_TILE // 2)) for i in range(2)]
    for rows in halves:
        acc_ref[rows, :] += _dot(act[rows], wd_last)
    for rows in halves:
        o_ref[0, rows, :] = _rms(acc_ref[rows, :], fw_ref[...])


def _block_diag_groups(w):
    per = LRU_GROUP // (LRU_WIDTH // LRU_BLOCKS)
    bd = LRU_WIDTH // LRU_BLOCKS
    w = w.reshape(LRU_WIDTH // LRU_GROUP, per, bd, bd)
    eye = jnp.eye(per, dtype=w.dtype)
    return jnp.einsum('gade,ab->gadbe', w, eye).reshape(LRU_WIDTH // LRU_GROUP, LRU_GROUP, LRU_GROUP)


def kernel(x, mix_norm_w, w_in, hg_lb, hg_norm_w, conv_w, conv_b, lru_wa, lru_ba, lru_wx, lru_bx,
           lru_a, w_out, ffn_norm_w, w_gate_up, w_down, final_norm_w):
    B, S, D = x.shape
    assert D == D_MODEL and w_in.shape == (1, D_MODEL, D_IN) and S % SEQ_TILE == 0 and SEQ_TILE == 2 * PROJ_ROWS
    bf = jnp.bfloat16
    d_ff = w_down.shape[1]
    assert d_ff % FF_CHUNK == 0

    win = w_in[0].astype(bf)
    wout = w_out[0].astype(bf)
    wg = jnp.concatenate([_block_diag_groups(lru_wa[0]), _block_diag_groups(lru_wx[0])], axis=-1).astype(bf)
    bg = jnp.stack([lru_ba[0], lru_bx[0]], axis=0)
    row = lambda a: a.reshape(1, -1)
    once = pl.Buffered(1)

    full = lambda shape: pl.BlockSpec(shape, lambda b, s: (0,) * len(shape), pipeline_mode=once)
    n_s = S // SEQ_TILE

    def next_block(b, s):
        wrap = s == n_s - 1
        return (jnp.minimum(jnp.where(wrap, b + 1, b), B - 1), jnp.where(wrap, 0, s + 1) * (SEQ_TILE // PROJ_ROWS), 0)

    o_hg, lru_in = pl.pallas_call(
        _hgrn_kernel,
        grid=(B, n_s),
        in_specs=[
            pl.BlockSpec((1, SEQ_TILE, D), lambda b, s: (b, s, 0)),
            pl.BlockSpec((1, PROJ_ROWS, D), next_block),
            full((1, D)), full((D, D_IN)), full((2, HG_WIDTH)), full((1, HG_DIM)),
        ],
        out_specs=[
            pl.BlockSpec((1, SEQ_TILE, HG_WIDTH), lambda b, s: (b, s, 0)),
            pl.BlockSpec((1, SEQ_TILE, 2 * LRU_WIDTH), lambda b, s: (b, s, 0)),
        ],
        out_shape=[
            jax.ShapeDtypeStruct((B, S, HG_WIDTH), bf),
            jax.ShapeDtypeStruct((B, S, 2 * LRU_WIDTH), jnp.float32),
        ],
        scratch_shapes=[
            pltpu.VMEM((PROJ_ROWS, D_IN), jnp.float32),
            pltpu.VMEM((PROJ_ROWS, 4 * HG_WIDTH), jnp.float32),
            pltpu.VMEM((HG_HEADS, HG_DIM, HG_DIM), jnp.float32),
        ],
        compiler_params=pltpu.CompilerParams(
            dimension_semantics=("arbitrary", "arbitrary"), vmem_limit_bytes=VMEM_LIMIT),
        name="hgrn",
    )(x, x, row(mix_norm_w[0]), win, hg_lb, row(hg_norm_w[0]))

    wgu = w_gate_up[0].astype(bf)
    wd = w_down[0].astype(bf)
    tiles_per_seq = S // FFN_TILE
    n_tiles = B * tiles_per_seq

    def cur(t):
        c = jnp.maximum(t - 1, 0)
        return (c // tiles_per_seq, c % tiles_per_seq, 0)

    def nxt(t):
        n = jnp.minimum(t, n_tiles - 1)
        return (n // tiles_per_seq, n % tiles_per_seq, 0)

    full1 = lambda shape: pl.BlockSpec(shape, lambda t: (0,) * len(shape), pipeline_mode=once)
    out = pl.pallas_call(
        functools.partial(_lru_ffn_kernel, tiles_per_seq=tiles_per_seq),
        grid=(n_tiles + 1,),
        in_specs=[
            pl.BlockSpec((1, FFN_TILE, D), nxt),
            pl.BlockSpec((1, FFN_TILE, HG_WIDTH), nxt),
            pl.BlockSpec((1, FFN_TILE, 2 * LRU_WIDTH), nxt),
            full1((CONV_WIDTH, LRU_WIDTH)), full1((1, LRU_WIDTH)),
            full1((LRU_WIDTH // LRU_GROUP, LRU_GROUP, 2 * LRU_GROUP)), full1((2, LRU_WIDTH)), full1((1, LRU_WIDTH)),
            full1((D_MIX, D)), full1((1, D)), full1((D, 2 * d_ff)), full1((d_ff, D)), full1((1, D)),
        ],
        out_specs=pl.BlockSpec((1, FFN_TILE, D), cur),
        out_shape=jax.ShapeDtypeStruct((B, S, D), jnp.float32),
        scratch_shapes=[
            pltpu.VMEM((N_LS, SLAB_ROWS, LANES), jnp.float32),
            pltpu.VMEM((N_LS, SLAB_ROWS, LANES), jnp.float32),
            pltpu.VMEM((FFN_TILE, LRU_WIDTH), jnp.bfloat16),
            pltpu.VMEM((1, LRU_WIDTH), jnp.float32),
            pltpu.VMEM((FFN_TILE, D), jnp.float32),
            pltpu.VMEM((FFN_TILE, D), jnp.float32),
            pltpu.VMEM((FFN_TILE, D), jnp.bfloat16),
        ],
        compiler_params=pltpu.CompilerParams(
            dimension_semantics=("arbitrary",), vmem_limit_bytes=VMEM_LIMIT),
        name="lru_ffn",
    )(x, o_hg, lru_in, conv_w[0], row(conv_b[0]), wg, bg, row(lru_a[0]), wout, row(ffn_norm_w[0]), wgu, wd,
      row(final_norm_w))
    return out
```

```python
import functools

import jax
import jax.numpy as jnp
from jax import lax
from jax.experimental import pallas as pl
from jax.experimental.pallas import tpu as pltpu

D_MODEL = 1024
HG_HEADS = 4
HG_DIM = 128
HG_WIDTH = HG_HEADS * HG_DIM
HG_BASE = 32
LRU_WIDTH = 512
LRU_BLOCKS = 8
LRU_GROUP = 256
CONV_WIDTH = 4
LRU_C = 8.0
D_MIX = HG_WIDTH + LRU_WIDTH
D_IN = 4 * HG_WIDTH + 2 * LRU_WIDTH
EPS = 1e-6
LOG2E = 1.4426950408889634

SUB = 128
PROJ_ROWS = 512
SEQ_TILE = 1024
FFN_TILE = 512
FF_CHUNK = 256
COL_CHUNK = 256
LANES = 128
ROW_TILE = 8
N_SEG = ROW_TILE
SEG = SUB // N_SEG
TAIL = ROW_TILE
SEG_PITCH = TAIL + SEG
SLAB_ROWS = SEG_PITCH * (FFN_TILE // SEG)
N_LS = LRU_WIDTH // LANES
PHASE_TRAIL = 2
HGRN_YIELDS = 5
VMEM_LIMIT = 56 * 1024 * 1024

_NT = (((1,), (1,)), ((), ()))


def _dot(a, b, dims=None):
    if dims is None:
        return jnp.dot(a, b, preferred_element_type=jnp.float32)
    return lax.dot_general(a, b, dims, preferred_element_type=jnp.float32)


def _rms(x, w):
    return x * lax.rsqrt(jnp.mean(x * x, axis=-1, keepdims=True) + EPS) * w


def _rows(shape):
    return lax.broadcasted_iota(jnp.int32, shape, 0)


def _cumsum_rows(x):
    n, w = x.shape
    sub = _rows(x.shape) % ROW_TILE
    s = 1
    while s < ROW_TILE:
        shifted = pltpu.roll(x.reshape(n // ROW_TILE, ROW_TILE, w), s, axis=1).reshape(n, w)
        x = x + jnp.where(sub >= s, shifted, 0.0)
        s *= 2
    outs = [x[0:ROW_TILE]]
    for t in range(1, n // ROW_TILE):
        outs.append(x[t * ROW_TILE:(t + 1) * ROW_TILE] + outs[-1][ROW_TILE - 1:ROW_TILE])
    return jnp.concatenate(outs, axis=0)


def _bcast_rows(r, n):
    return jnp.broadcast_to(r, (n, r.shape[1]))


def _hgrn_kernel(x_ref, xnext_ref, nw_ref, win_ref, lb_ref, hgw_ref, ohg_ref, lru_ref,
                 proj0_ref, proj1_ref, xn0_ref, state_ref):
    bi = pl.program_id(0)
    si = pl.program_id(1)
    proj_refs = (proj0_ref, proj1_ref)

    @pl.when(si == 0)
    def _():
        state_ref[...] = jnp.zeros_like(state_ref)

    nw = nw_ref[...]
    n_hg = 4 * HG_WIDTH

    lbp = lb_ref[...]
    lbe = jnp.exp(lbp - jnp.max(lbp, axis=0, keepdims=True))
    lb = lbe[0:1] / jnp.sum(lbe, axis=0, keepdims=True)
    hgw = hgw_ref[...]

    row = _rows((SUB, SUB))
    col = lax.broadcasted_iota(jnp.int32, (SUB, SUB), 1)
    mask0 = (row // HG_BASE == col // HG_BASE) & (col <= row)
    mask1 = ((row // (2 * HG_BASE) == col // (2 * HG_BASE))
             & (row % (2 * HG_BASE) >= HG_BASE) & (col % (2 * HG_BASE) < HG_BASE))

    def in_proj(load_x, parity, store_lru, stash_ref=None):
        proj_ref = proj_refs[parity]
        cell = []

        def norm():
            cell.append(_rms(load_x(), nw).astype(jnp.bfloat16))
            if stash_ref is not None:
                stash_ref[...] = cell[0]

        def hg_chunk(c):
            proj_ref[:, c:c + COL_CHUNK] = _dot(cell[0], win_ref[:, c:c + COL_CHUNK])

        def lru_chunk(c):
            store_lru(c - n_hg, _dot(cell[0], win_ref[:, c:c + COL_CHUNK]))

        return (norm, [functools.partial(hg_chunk, c) for c in range(0, n_hg, COL_CHUNK)],
                [functools.partial(lru_chunk, c) for c in range(n_hg, D_IN, COL_CHUNK)])

    def sub_tile(blk, j):
        proj_ref = proj_refs[blk % 2]
        p0 = j * SUB
        r0 = blk * PROJ_ROWS + p0
        half = SUB // 2
        q = proj_ref[p0:p0 + SUB, 0:HG_WIDTH]
        fl = proj_ref[p0:p0 + SUB, HG_WIDTH:2 * HG_WIDTH]
        v = proj_ref[p0:p0 + SUB, 2 * HG_WIDTH:3 * HG_WIDTH].astype(jnp.bfloat16)
        q = q * jax.nn.sigmoid(q)
        f = lb + (1.0 - lb) * jax.nn.sigmoid(fl)
        k = 1.0 - f
        lf = jnp.log(f) * LOG2E
        yield
        beta = _cumsum_rows(lf)
        yield
        r31 = beta[HG_BASE - 1:HG_BASE]
        r63 = beta[2 * HG_BASE - 1:2 * HG_BASE]
        r95 = beta[3 * HG_BASE - 1:3 * HG_BASE]
        r127 = beta[4 * HG_BASE - 1:4 * HG_BASE]
        ref0 = jnp.concatenate([jnp.zeros((HG_BASE, HG_WIDTH), jnp.float32), _bcast_rows(r31, HG_BASE),
                                _bcast_rows(r63, HG_BASE), _bcast_rows(r95, HG_BASE)], axis=0)
        ref1 = jnp.concatenate([_bcast_rows(r31, 2 * HG_BASE), _bcast_rows(r95, 2 * HG_BASE)], axis=0)
        b0 = beta - ref0
        q0 = (q * jnp.exp2(b0)).astype(jnp.bfloat16)
        k0 = (k * jnp.exp2(-b0)).astype(jnp.bfloat16)
        k1 = (k * jnp.exp2(jnp.minimum(ref1 - beta, 0.0))).astype(jnp.bfloat16)
        yield
        q2 = (q[half:] * jnp.exp2(beta[half:] - r63)).astype(jnp.bfloat16)
        k2 = jnp.concatenate([(k[:half] * jnp.exp2(r63 - beta[:half])).astype(jnp.bfloat16),
                              jnp.zeros((half, HG_WIDTH), jnp.bfloat16)], axis=0)
        qs = (q * jnp.exp2(beta)).astype(jnp.bfloat16)
        ks = (k * jnp.exp2(r127 - beta)).astype(jnp.bfloat16)
        dec_end = jnp.exp2(r127)
        yield
        heads = [slice(h * HG_DIM, (h + 1) * HG_DIM) for h in range(HG_HEADS)]
        s01 = [_dot(q0[:, hs], jnp.concatenate([k0[:, hs], k1[:, hs]], axis=0), _NT) for hs in heads]
        s2 = [_dot(q2[:, hs], k2[:, hs], _NT) for hs in heads]
        vt = [v[:, hs].T for hs in heads]
        inc = [_dot(vt[h], ks[:, hs]) for h, hs in enumerate(heads)]
        yield
        g = proj_ref[p0:p0 + SUB, 3 * HG_WIDTH:4 * HG_WIDTH]
        gate = g * jax.nn.sigmoid(g)
        for h, hs in enumerate(heads):
            sc = jnp.where(mask0, s01[h][:, :SUB], 0.0) + jnp.where(mask1, s01[h][:, SUB:], 0.0)
            sc = jnp.concatenate([sc[:half], sc[half:] + s2[h]], axis=0)
            st = state_ref[h]
            o = _dot(jnp.concatenate([sc.astype(jnp.bfloat16), qs[:, hs]], axis=1),
                     jnp.concatenate([vt[h], st.astype(jnp.bfloat16)], axis=1), _NT)
            state_ref[h] = st * dec_end[:, hs] + inc[h]
            o = o * lax.rsqrt(jnp.mean(o * o, axis=-1, keepdims=True) + EPS) * hgw
            ohg_ref[0, r0:r0 + SUB, hs] = (o * gate[:, hs]).astype(jnp.bfloat16)

    def lru_to_scratch(off, val):
        proj0_ref[:, n_hg + off:n_hg + off + COL_CHUNK] = val

    def lru_to_output(off, val):
        lru_ref[0, PROJ_ROWS:2 * PROJ_ROWS, off:off + COL_CHUNK] = val

    def flush_first_block():
        lru_ref[0, 0:PROJ_ROWS, :] = proj0_ref[:, n_hg:D_IN]

    def first_block_lru(c):
        lru_to_scratch(c - n_hg, _dot(xn0_ref[...], win_ref[:, c:c + COL_CHUNK]))

    @pl.when((bi == 0) & (si == 0))
    def _():
        norm_first, hg_first, _ = in_proj(lambda: x_ref[0, 0:PROJ_ROWS, :], 0, None, stash_ref=xn0_ref)
        for step in [norm_first] + hg_first:
            step()

    norm_next, hg_next, _ = in_proj(lambda: xnext_ref[0], 0, None, stash_ref=xn0_ref)
    norm_second, hg_second, lru_second = in_proj(lambda: x_ref[0, PROJ_ROWS:2 * PROJ_ROWS, :], 1, lru_to_output)
    lead_lru = [functools.partial(first_block_lru, c) for c in range(n_hg, D_IN, COL_CHUNK)]
    phases = (
        (lead_lru + [norm_second, flush_first_block] + hg_second + lru_second + [norm_next], len(lead_lru), 0),
        (hg_next, 0, PHASE_TRAIL),
    )
    for blk, (fill, lead, trail) in enumerate(phases):
        for step in fill[:lead]:
            step()
        spread = fill[lead:len(fill) - trail]
        points = [sub_tile(blk, j) for j in range(PROJ_ROWS // SUB)]
        n_points = len(points) * HGRN_YIELDS
        seen = done = 0
        for gen in points:
            for _ in gen:
                seen += 1
                upto = -(-seen * len(spread) // n_points)
                for step in spread[done:upto]:
                    step()
                done = upto
        assert seen == n_points and done == len(spread)
        for step in fill[len(fill) - trail:]:
            step()


def _lru_ffn_kernel(x_ref, ohg_ref, lru_ref, cw_ref, cb_ref, wg_ref, bg_ref, ap_ref, wout_ref, nw_ref, wgu_ref,
                    wd_ref, fw_ref, o_ref, xl_ref, hs_ref, mlru_ref, hcar_ref, acc_ref, h_ref, hn_ref, *,
                    tiles_per_seq):
    t = pl.program_id(0)
    seq_start = (t % tiles_per_seq) == 0
    d_ff = wd_ref.shape[0]
    n_ff = d_ff // FF_CHUNK
    last_rows = slice(SLAB_ROWS - TAIL, SLAB_ROWS)

    @pl.when(t == 0)
    def _():
        h_ref[...] = jnp.zeros_like(h_ref)
        hn_ref[...] = jnp.zeros_like(hn_ref)
        hcar_ref[...] = jnp.zeros_like(hcar_ref)
        xl_ref[:, last_rows, :] = jnp.zeros((N_LS, TAIL, LANES), jnp.float32)

    neg_c_sp2 = (-LRU_C * LOG2E) * jax.nn.softplus(-ap_ref[...])
    cw_rows = [jnp.broadcast_to(cw_ref[k:k + 1, :], (N_SEG, LRU_WIDTH)) for k in range(CONV_WIDTH)]
    cb_rows = jnp.broadcast_to(cb_ref[...], (N_SEG, LRU_WIDTH))
    bg = bg_ref[...]

    def store_slabs():
        val = lru_ref[0, :, 0:LRU_WIDTH]
        n_seg = FFN_TILE // SEG
        for ls in range(N_LS):
            lanes = slice(ls * LANES, (ls + 1) * LANES)
            xl_ref[ls, 0:TAIL, :] = jnp.where(seq_start, 0.0, xl_ref[ls, last_rows, :])
            for s in range(n_seg):
                xl_ref[ls, SEG_PITCH * s + TAIL:SEG_PITCH * (s + 1), :] = val[SEG * s:SEG * (s + 1), lanes]
                if s + 1 < n_seg:
                    xl_ref[ls, SEG_PITCH * (s + 1):SEG_PITCH * (s + 1) + TAIL, :] = (
                        val[SEG * (s + 1) - TAIL:SEG * (s + 1), lanes])

    conv_out = {}

    def conv(j):
        slab_start = SEG_PITCH * j * N_SEG + TAIL

        def x_slab(r):
            return jnp.concatenate([xl_ref[ls, pl.ds(slab_start + r, N_SEG, stride=SEG_PITCH), :]
                                    for ls in range(N_LS)], axis=1)

        xs = {r: x_slab(r) for r in range(1 - CONV_WIDTH, SEG)}
        xc_slabs = []
        for r in range(SEG):
            acc = cb_rows
            for tap in range(CONV_WIDTH):
                acc = acc + xs[r - (CONV_WIDTH - 1 - tap)] * cw_rows[tap]
            xc_slabs.append(acc)
        conv_out[j] = jnp.concatenate(xc_slabs, axis=0)

    def recur(j):
        xc = conv_out[j]
        xcb = xc.astype(jnp.bfloat16)
        gates = [_dot(xcb[:, gi * LRU_GROUP:(gi + 1) * LRU_GROUP], wg_ref[gi]) for gi in range(LRU_WIDTH // LRU_GROUP)]
        rr = jnp.concatenate([gt[:, :LRU_GROUP] for gt in gates], axis=1)
        ii = jnp.concatenate([gt[:, LRU_GROUP:] for gt in gates], axis=1)
        rr = jax.nn.sigmoid(rr + bg[0:1])
        ii = jax.nn.sigmoid(ii + bg[1:2])
        a = jnp.exp2(rr * neg_c_sp2)
        b_in = jnp.sqrt(1.0 - a * a) * (ii * xc)
        h_loc, a_loc = [b_in[0:N_SEG]], [a[0:N_SEG]]
        for r in range(1, SEG):
            rows = slice(r * N_SEG, (r + 1) * N_SEG)
            h_loc.append(a[rows] * h_loc[-1] + b_in[rows])
            a_loc.append(a[rows] * a_loc[-1])
        h_in = hcar_ref[...]
        if j == 0:
            h_in = jnp.where(seq_start, 0.0, h_in)
        carries = [h_in]
        for s in range(N_SEG):
            carries.append(a_loc[-1][s:s + 1] * carries[-1] + h_loc[-1][s:s + 1])
        hcar_ref[...] = carries[-1]
        carry = jnp.concatenate(carries[:-1], axis=0)
        hs0 = j * N_SEG * SEG_PITCH
        for r in range(SEG):
            h_r = h_loc[r] + a_loc[r] * carry
            for ls in range(N_LS):
                hs_ref[ls, pl.ds(hs0 + TAIL + r, N_SEG, stride=SEG_PITCH), :] = h_r[:, ls * LANES:(ls + 1) * LANES]
        hh = jnp.concatenate(
            [jnp.concatenate([hs_ref[ls, hs0 + SEG_PITCH * s + TAIL:hs0 + SEG_PITCH * (s + 1), :]
                              for s in range(N_SEG)], axis=0) for ls in range(N_LS)], axis=1)
        gl = lru_ref[0, j * SUB:(j + 1) * SUB, LRU_WIDTH:]
        mlru_ref[j * SUB:(j + 1) * SUB, :] = (hh * jax.nn.gelu(gl, approximate=True)).astype(jnp.bfloat16)

    def first_piece():
        store_slabs()
        conv(0)

    pieces = [first_piece, functools.partial(recur, 0)]
    for j in range(1, FFN_TILE // SUB):
        pieces += [functools.partial(conv, j), functools.partial(recur, j)]

    def out_proj():
        mixed = jnp.concatenate([ohg_ref[0], mlru_ref[...]], axis=1)
        h_ref[...] = x_ref[0] + _dot(mixed, wout_ref[...])

    def ffn_norm():
        hn_ref[...] = _rms(h_ref[...], nw_ref[...]).astype(jnp.bfloat16)

    pieces += [out_proj, ffn_norm]
    assert len(pieces) == n_ff - 1

    def gate_up(c):
        gt = _dot(hn_ref[...], wgu_ref[:, c * FF_CHUNK:(c + 1) * FF_CHUNK])
        up = _dot(hn_ref[...], wgu_ref[:, d_ff + c * FF_CHUNK:d_ff + (c + 1) * FF_CHUNK])
        return gt, up

    def activate(gt, up):
        return (gt * jax.nn.sigmoid(gt) * up).astype(jnp.bfloat16)

    act = activate(*gate_up(0))
    for c in range(n_ff - 1):
        gu = gate_up(c + 1)
        down = _dot(act, wd_ref[c * FF_CHUNK:(c + 1) * FF_CHUNK, :])
        if c == 0:
            acc_ref[...] = h_ref[...] + down
        else:
            acc_ref[...] += down
        pieces[c]()
        act = activate(*gu)
    wd_last = wd_ref[(n_ff - 1) * FF_CHUNK:n_ff * FF_CHUNK, :]
    halves = [slice(i * (FFN_TILE // 2), (i + 1) * (FFN_TILE // 2)) for i in range(2)]
    for rows in halves:
        acc_ref[rows, :] += _dot(act[rows], wd_last)
    for rows in halves:
        o_ref[0, rows, :] = _rms(acc_ref[rows, :], fw_ref[...])


def _block_diag_groups(w):
    per = LRU_GROUP // (LRU_WIDTH // LRU_BLOCKS)
    bd = LRU_WIDTH // LRU_BLOCKS
    w = w.reshape(LRU_WIDTH // LRU_GROUP, per, bd, bd)
    eye = jnp.eye(per, dtype=w.dtype)
    return jnp.einsum('gade,ab->gadbe', w, eye).reshape(LRU_WIDTH // LRU_GROUP, LRU_GROUP, LRU_GROUP)


def kernel(x, mix_norm_w, w_in, hg_lb, hg_norm_w, conv_w, conv_b, lru_wa, lru_ba, lru_wx, lru_bx,
           lru_a, w_out, ffn_norm_w, w_gate_up, w_down, final_norm_w):
    B, S, D = x.shape
    assert D == D_MODEL and w_in.shape == (1, D_MODEL, D_IN) and S % SEQ_TILE == 0 and SEQ_TILE == 2 * PROJ_ROWS
    bf = jnp.bfloat16
    d_ff = w_down.shape[1]
    assert d_ff % FF_CHUNK == 0

    win = w_in[0].astype(bf)
    wout = w_out[0].astype(bf)
    wg = jnp.concatenate([_block_diag_groups(lru_wa[0]), _block_diag_groups(lru_wx[0])], axis=-1).astype(bf)
    bg = jnp.stack([lru_ba[0], lru_bx[0]], axis=0)
    row = lambda a: a.reshape(1, -1)
    once = pl.Buffered(1)

    full = lambda shape: pl.BlockSpec(shape, lambda b, s: (0,) * len(shape), pipeline_mode=once)
    n_s = S // SEQ_TILE

    def next_block(b, s):
        wrap = s == n_s - 1
        return (jnp.minimum(jnp.where(wrap, b + 1, b), B - 1), jnp.where(wrap, 0, s + 1) * (SEQ_TILE // PROJ_ROWS), 0)

    o_hg, lru_in = pl.pallas_call(
        _hgrn_kernel,
        grid=(B, n_s),
        in_specs=[
            pl.BlockSpec((1, SEQ_TILE, D), lambda b, s: (b, s, 0)),
            pl.BlockSpec((1, PROJ_ROWS, D), next_block),
            full((1, D)), full((D, D_IN)), full((2, HG_WIDTH)), full((1, HG_DIM)),
        ],
        out_specs=[
            pl.BlockSpec((1, SEQ_TILE, HG_WIDTH), lambda b, s: (b, s, 0)),
            pl.BlockSpec((1, SEQ_TILE, 2 * LRU_WIDTH), lambda b, s: (b, s, 0)),
        ],
        out_shape=[
            jax.ShapeDtypeStruct((B, S, HG_WIDTH), bf),
            jax.ShapeDtypeStruct((B, S, 2 * LRU_WIDTH), jnp.float32),
        ],
        scratch_shapes=[
            pltpu.VMEM((PROJ_ROWS, D_IN), jnp.float32),
            pltpu.VMEM((PROJ_ROWS, 4 * HG_WIDTH), jnp.float32),
            pltpu.VMEM((PROJ_ROWS, D), jnp.bfloat16),
            pltpu.VMEM((HG_HEADS, HG_DIM, HG_DIM), jnp.float32),
        ],
        compiler_params=pltpu.CompilerParams(
            dimension_semantics=("arbitrary", "arbitrary"), vmem_limit_bytes=VMEM_LIMIT),
        name="hgrn",
    )(x, x, row(mix_norm_w[0]), win, hg_lb, row(hg_norm_w[0]))

    wgu = w_gate_up[0].astype(bf)
    wd = w_down[0].astype(bf)
    tiles_per_seq = S // FFN_TILE
    n_tiles = B * tiles_per_seq

    def cur(t):
        c = jnp.maximum(t - 1, 0)
        return (c // tiles_per_seq, c % tiles_per_seq, 0)

    def nxt(t):
        n = jnp.minimum(t, n_tiles - 1)
        return (n // tiles_per_seq, n % tiles_per_seq, 0)

    full1 = lambda shape: pl.BlockSpec(shape, lambda t: (0,) * len(shape), pipeline_mode=once)
    out = pl.pallas_call(
        functools.partial(_lru_ffn_kernel, tiles_per_seq=tiles_per_seq),
        grid=(n_tiles + 1,),
        in_specs=[
            pl.BlockSpec((1, FFN_TILE, D), nxt),
            pl.BlockSpec((1, FFN_TILE, HG_WIDTH), nxt),
            pl.BlockSpec((1, FFN_TILE, 2 * LRU_WIDTH), nxt),
            full1((CONV_WIDTH, LRU_WIDTH)), full1((1, LRU_WIDTH)),
            full1((LRU_WIDTH // LRU_GROUP, LRU_GROUP, 2 * LRU_GROUP)), full1((2, LRU_WIDTH)), full1((1, LRU_WIDTH)),
            full1((D_MIX, D)), full1((1, D)), full1((D, 2 * d_ff)), full1((d_ff, D)), full1((1, D)),
        ],
        out_specs=pl.BlockSpec((1, FFN_TILE, D), cur),
        out_shape=jax.ShapeDtypeStruct((B, S, D), jnp.float32),
        scratch_shapes=[
            pltpu.VMEM((N_LS, SLAB_ROWS, LANES), jnp.float32),
            pltpu.VMEM((N_LS, SLAB_ROWS, LANES), jnp.float32),
            pltpu.VMEM((FFN_TILE, LRU_WIDTH), jnp.bfloat16),
            pltpu.VMEM((1, LRU_WIDTH), jnp.float32),
            pltpu.VMEM((FFN_TILE, D), jnp.float32),
            pltpu.VMEM((FFN_TILE, D), jnp.float32),
            pltpu.VMEM((FFN_TILE, D), jnp.bfloat16),
        ],
        compiler_params=pltpu.CompilerParams(
            dimension_semantics=("arbitrary",), vmem_limit_bytes=VMEM_LIMIT),
        name="lru_ffn",
    )(x, o_hg, lru_in, conv_w[0], row(conv_b[0]), wg, bg, row(lru_a[0]), wout, row(ffn_norm_w[0]), wgu, wd,
      row(final_norm_w))
    return out
```

```python
import functools

import jax
import jax.numpy as jnp
from jax import lax
from jax.experimental import pallas as pl
from jax.experimental.pallas import tpu as pltpu

D_MODEL = 1024
HG_HEADS = 4
HG_DIM = 128
HG_WIDTH = HG_HEADS * HG_DIM
HG_BASE = 32
LRU_WIDTH = 512
LRU_BLOCKS = 8
LRU_GROUP = 256
CONV_WIDTH = 4
LRU_C = 8.0
D_MIX = HG_WIDTH + LRU_WIDTH
D_IN = 4 * HG_WIDTH + 2 * LRU_WIDTH
EPS = 1e-6
LOG2E = 1.4426950408889634

SUB = 128
PROJ_ROWS = 512
SEQ_TILE = 1024
FFN_TILE = 512
FF_CHUNK = 256
FFN_SPLIT = 6
COL_CHUNK = 256
LANES = 128
ROW_TILE = 8
N_SEG = ROW_TILE
SEG = SUB // N_SEG
TAIL = ROW_TILE
SEG_PITCH = TAIL + SEG
SLAB_ROWS = SEG_PITCH * (FFN_TILE // SEG)
N_LS = LRU_WIDTH // LANES
PHASE_TRAIL = 2
HGRN_YIELDS = 5
VMEM_LIMIT = 56 * 1024 * 1024

_NT = (((1,), (1,)), ((), ()))


def _dot(a, b, dims=None):
    if dims is None:
        return jnp.dot(a, b, preferred_element_type=jnp.float32)
    return lax.dot_general(a, b, dims, preferred_element_type=jnp.float32)


def _rms(x, w):
    return x * lax.rsqrt(jnp.mean(x * x, axis=-1, keepdims=True) + EPS) * w


def _rows(shape):
    return lax.broadcasted_iota(jnp.int32, shape, 0)


def _cumsum_rows(x):
    n, w = x.shape
    sub = _rows(x.shape) % ROW_TILE
    s = 1
    while s < ROW_TILE:
        shifted = pltpu.roll(x.reshape(n // ROW_TILE, ROW_TILE, w), s, axis=1).reshape(n, w)
        x = x + jnp.where(sub >= s, shifted, 0.0)
        s *= 2
    outs = [x[0:ROW_TILE]]
    for t in range(1, n // ROW_TILE):
        outs.append(x[t * ROW_TILE:(t + 1) * ROW_TILE] + outs[-1][ROW_TILE - 1:ROW_TILE])
    return jnp.concatenate(outs, axis=0)


def _bcast_rows(r, n):
    return jnp.broadcast_to(r, (n, r.shape[1]))


def _hgrn_kernel(x_ref, xnext_ref, nw_ref, win_ref, lb_ref, hgw_ref, ohg_ref, lru_ref,
                 proj0_ref, proj1_ref, state_ref):
    bi = pl.program_id(0)
    si = pl.program_id(1)
    proj_refs = (proj0_ref, proj1_ref)

    @pl.when(si == 0)
    def _():
        state_ref[...] = jnp.zeros_like(state_ref)

    nw = nw_ref[...]
    n_hg = 4 * HG_WIDTH

    lbp = lb_ref[...]
    lbe = jnp.exp(lbp - jnp.max(lbp, axis=0, keepdims=True))
    lb = lbe[0:1] / jnp.sum(lbe, axis=0, keepdims=True)
    hgw = hgw_ref[...]

    row = _rows((SUB, SUB))
    col = lax.broadcasted_iota(jnp.int32, (SUB, SUB), 1)
    mask0 = (row // HG_BASE == col // HG_BASE) & (col <= row)
    mask1 = ((row // (2 * HG_BASE) == col // (2 * HG_BASE))
             & (row % (2 * HG_BASE) >= HG_BASE) & (col % (2 * HG_BASE) < HG_BASE))

    def in_proj(load_x, parity, store_lru):
        proj_ref = proj_refs[parity]
        cell = []

        def norm():
            cell.append(_rms(load_x(), nw).astype(jnp.bfloat16))

        def hg_chunk(c):
            proj_ref[:, c:c + COL_CHUNK] = _dot(cell[0], win_ref[:, c:c + COL_CHUNK])

        def lru_chunk(c):
            store_lru(c - n_hg, _dot(cell[0], win_ref[:, c:c + COL_CHUNK]))

        return ([norm] + [functools.partial(hg_chunk, c) for c in range(0, n_hg, COL_CHUNK)]
                + [functools.partial(lru_chunk, c) for c in range(n_hg, D_IN, COL_CHUNK)])

    def sub_tile(blk, j):
        proj_ref = proj_refs[blk % 2]
        p0 = j * SUB
        r0 = blk * PROJ_ROWS + p0
        half = SUB // 2
        q = proj_ref[p0:p0 + SUB, 0:HG_WIDTH]
        fl = proj_ref[p0:p0 + SUB, HG_WIDTH:2 * HG_WIDTH]
        v = proj_ref[p0:p0 + SUB, 2 * HG_WIDTH:3 * HG_WIDTH].astype(jnp.bfloat16)
        q = q * jax.nn.sigmoid(q)
        f = lb + (1.0 - lb) * jax.nn.sigmoid(fl)
        k = 1.0 - f
        lf = jnp.log(f) * LOG2E
        yield
        beta = _cumsum_rows(lf)
        yield
        r31 = beta[HG_BASE - 1:HG_BASE]
        r63 = beta[2 * HG_BASE - 1:2 * HG_BASE]
        r95 = beta[3 * HG_BASE - 1:3 * HG_BASE]
        r127 = beta[4 * HG_BASE - 1:4 * HG_BASE]
        ref0 = jnp.concatenate([jnp.zeros((HG_BASE, HG_WIDTH), jnp.float32), _bcast_rows(r31, HG_BASE),
                                _bcast_rows(r63, HG_BASE), _bcast_rows(r95, HG_BASE)], axis=0)
        ref1 = jnp.concatenate([_bcast_rows(r31, 2 * HG_BASE), _bcast_rows(r95, 2 * HG_BASE)], axis=0)
        b0 = beta - ref0
        q0 = (q * jnp.exp2(b0)).astype(jnp.bfloat16)
        k0 = (k * jnp.exp2(-b0)).astype(jnp.bfloat16)
        k1 = (k * jnp.exp2(jnp.minimum(ref1 - beta, 0.0))).astype(jnp.bfloat16)
        yield
        q2 = (q[half:] * jnp.exp2(beta[half:] - r63)).astype(jnp.bfloat16)
        k2 = jnp.concatenate([(k[:half] * jnp.exp2(r63 - beta[:half])).astype(jnp.bfloat16),
                              jnp.zeros((half, HG_WIDTH), jnp.bfloat16)], axis=0)
        qs = (q * jnp.exp2(beta)).astype(jnp.bfloat16)
        ks = (k * jnp.exp2(r127 - beta)).astype(jnp.bfloat16)
        dec_end = jnp.exp2(r127)
        yield
        heads = [slice(h * HG_DIM, (h + 1) * HG_DIM) for h in range(HG_HEADS)]
        s01 = [_dot(q0[:, hs], jnp.concatenate([k0[:, hs], k1[:, hs]], axis=0), _NT) for hs in heads]
        s2 = [_dot(q2[:, hs], k2[:, hs], _NT) for hs in heads]
        vt = [v[:, hs].T for hs in heads]
        inc = [_dot(vt[h], ks[:, hs]) for h, hs in enumerate(heads)]
        yield
        g = proj_ref[p0:p0 + SUB, 3 * HG_WIDTH:4 * HG_WIDTH]
        gate = g * jax.nn.sigmoid(g)
        for h, hs in enumerate(heads):
            sc = jnp.where(mask0, s01[h][:, :SUB], 0.0) + jnp.where(mask1, s01[h][:, SUB:], 0.0)
            sc = jnp.concatenate([sc[:half], sc[half:] + s2[h]], axis=0)
            st = state_ref[h]
            o = _dot(jnp.concatenate([sc.astype(jnp.bfloat16), qs[:, hs]], axis=1),
                     jnp.concatenate([vt[h], st.astype(jnp.bfloat16)], axis=1), _NT)
            state_ref[h] = st * dec_end[:, hs] + inc[h]
            o = o * lax.rsqrt(jnp.mean(o * o, axis=-1, keepdims=True) + EPS) * hgw
            ohg_ref[0, r0:r0 + SUB, hs] = (o * gate[:, hs]).astype(jnp.bfloat16)

    def lru_to_scratch(off, val):
        proj0_ref[:, n_hg + off:n_hg + off + COL_CHUNK] = val

    def lru_to_output(off, val):
        lru_ref[0, PROJ_ROWS:2 * PROJ_ROWS, off:off + COL_CHUNK] = val

    def flush_first_block():
        lru_ref[0, 0:PROJ_ROWS, :] = proj0_ref[:, n_hg:D_IN]

    @pl.when((bi == 0) & (si == 0))
    def _():
        for step in in_proj(lambda: x_ref[0, 0:PROJ_ROWS, :], 0, lru_to_scratch):
            step()

    norm_next, *proj_next = in_proj(lambda: xnext_ref[0], 0, lru_to_scratch)
    norm_second, *proj_second = in_proj(lambda: x_ref[0, PROJ_ROWS:2 * PROJ_ROWS, :], 1, lru_to_output)
    phases = (
        ([norm_second, flush_first_block] + proj_second + [norm_next], 0),
        (proj_next, PHASE_TRAIL),
    )
    for blk, (fill, trail) in enumerate(phases):
        spread = fill[:len(fill) - trail]
        points = [sub_tile(blk, j) for j in range(PROJ_ROWS // SUB)]
        n_points = len(points) * HGRN_YIELDS
        seen = done = 0
        for gen in points:
            for _ in gen:
                seen += 1
                upto = -(-seen * len(spread) // n_points)
                for step in spread[done:upto]:
                    step()
                done = upto
        assert seen == n_points and done == len(spread)
        for step in fill[len(fill) - trail:]:
            step()


def _lru_ffn_kernel(x_ref, ohg_ref, lru_ref, lru0_ref, cw_ref, cb_ref, wg_ref, bg_ref, ap_ref, wout_ref, nw_ref, wgu_ref,
                    wd_ref, fw_ref, o_ref, xl_ref, hs_ref, mlru_ref, hcar_ref, acc_ref, h_ref, hn_ref, *,
                    tiles_per_seq):
    t = pl.program_id(0)
    d_ff = wd_ref.shape[0]
    n_ff = d_ff // FF_CHUNK
    last_rows = slice(SLAB_ROWS - TAIL, SLAB_ROWS)
    next_starts = ((t + 1) % tiles_per_seq) == 0
    ctx = {"src": lru_ref, "start": next_starts}

    neg_c_sp2 = (-LRU_C * LOG2E) * jax.nn.softplus(-ap_ref[...])
    cw_rows = [jnp.broadcast_to(cw_ref[k:k + 1, :], (N_SEG, LRU_WIDTH)) for k in range(CONV_WIDTH)]
    cb_rows = jnp.broadcast_to(cb_ref[...], (N_SEG, LRU_WIDTH))
    bg = bg_ref[...]

    def store_slabs():
        val = ctx["src"][0, :, 0:LRU_WIDTH]
        n_seg = FFN_TILE // SEG
        for ls in range(N_LS):
            lanes = slice(ls * LANES, (ls + 1) * LANES)
            xl_ref[ls, 0:TAIL, :] = jnp.where(ctx["start"], 0.0, xl_ref[ls, last_rows, :])
            for s in range(n_seg):
                xl_ref[ls, SEG_PITCH * s + TAIL:SEG_PITCH * (s + 1), :] = val[SEG * s:SEG * (s + 1), lanes]
                if s + 1 < n_seg:
                    xl_ref[ls, SEG_PITCH * (s + 1):SEG_PITCH * (s + 1) + TAIL, :] = (
                        val[SEG * (s + 1) - TAIL:SEG * (s + 1), lanes])

    conv_out = {}

    def conv(j):
        slab_start = SEG_PITCH * j * N_SEG + TAIL

        def x_slab(r):
            return jnp.concatenate([xl_ref[ls, pl.ds(slab_start + r, N_SEG, stride=SEG_PITCH), :]
                                    for ls in range(N_LS)], axis=1)

        xs = {r: x_slab(r) for r in range(1 - CONV_WIDTH, SEG)}
        xc_slabs = []
        for r in range(SEG):
            acc = cb_rows
            for tap in range(CONV_WIDTH):
                acc = acc + xs[r - (CONV_WIDTH - 1 - tap)] * cw_rows[tap]
            xc_slabs.append(acc)
        conv_out[j] = jnp.concatenate(xc_slabs, axis=0)

    def recur(j):
        xc = conv_out[j]
        xcb = xc.astype(jnp.bfloat16)
        gates = [_dot(xcb[:, gi * LRU_GROUP:(gi + 1) * LRU_GROUP], wg_ref[gi]) for gi in range(LRU_WIDTH // LRU_GROUP)]
        rr = jnp.concatenate([gt[:, :LRU_GROUP] for gt in gates], axis=1)
        ii = jnp.concatenate([gt[:, LRU_GROUP:] for gt in gates], axis=1)
        rr = jax.nn.sigmoid(rr + bg[0:1])
        ii = jax.nn.sigmoid(ii + bg[1:2])
        a = jnp.exp2(rr * neg_c_sp2)
        b_in = jnp.sqrt(1.0 - a * a) * (ii * xc)
        h_loc, a_loc = [b_in[0:N_SEG]], [a[0:N_SEG]]
        for r in range(1, SEG):
            rows = slice(r * N_SEG, (r + 1) * N_SEG)
            h_loc.append(a[rows] * h_loc[-1] + b_in[rows])
            a_loc.append(a[rows] * a_loc[-1])
        h_in = hcar_ref[...]
        if j == 0:
            h_in = jnp.where(ctx["start"], 0.0, h_in)
        carries = [h_in]
        for s in range(N_SEG):
            carries.append(a_loc[-1][s:s + 1] * carries[-1] + h_loc[-1][s:s + 1])
        hcar_ref[...] = carries[-1]
        carry = jnp.concatenate(carries[:-1], axis=0)
        hs0 = j * N_SEG * SEG_PITCH
        for r in range(SEG):
            h_r = h_loc[r] + a_loc[r] * carry
            for ls in range(N_LS):
                hs_ref[ls, pl.ds(hs0 + TAIL + r, N_SEG, stride=SEG_PITCH), :] = h_r[:, ls * LANES:(ls + 1) * LANES]
        hh = jnp.concatenate(
            [jnp.concatenate([hs_ref[ls, hs0 + SEG_PITCH * s + TAIL:hs0 + SEG_PITCH * (s + 1), :]
                              for s in range(N_SEG)], axis=0) for ls in range(N_LS)], axis=1)
        gl = ctx["src"][0, j * SUB:(j + 1) * SUB, LRU_WIDTH:]
        mlru_ref[j * SUB:(j + 1) * SUB, :] = (hh * jax.nn.gelu(gl, approximate=True)).astype(jnp.bfloat16)

    def first_piece():
        store_slabs()
        conv(0)

    pieces = [first_piece, functools.partial(recur, 0)]
    for j in range(1, FFN_TILE // SUB):
        pieces += [functools.partial(conv, j), functools.partial(recur, j)]

    def out_proj():
        mixed = jnp.concatenate([ohg_ref[0], mlru_ref[...]], axis=1)
        h_ref[...] = x_ref[0] + _dot(mixed, wout_ref[...])

    def ffn_norm():
        hn_ref[...] = _rms(h_ref[...], nw_ref[...]).astype(jnp.bfloat16)

    lru_pieces = list(pieces)
    assert len(lru_pieces) == 2 * (FFN_TILE // SUB)

    @pl.when(t == 0)
    def _():
        h_ref[...] = jnp.zeros_like(h_ref)
        hn_ref[...] = jnp.zeros_like(hn_ref)
        acc_ref[...] = jnp.zeros_like(acc_ref)
        hcar_ref[...] = jnp.zeros_like(hcar_ref)
        xl_ref[:, last_rows, :] = jnp.zeros((N_LS, TAIL, LANES), jnp.float32)
        ctx.update(src=lru0_ref, start=True)
        for piece in lru_pieces:
            piece()
        ctx.update(src=lru_ref, start=next_starts)

    def gate_up(c):
        gt = _dot(hn_ref[...], wgu_ref[:, c * FF_CHUNK:(c + 1) * FF_CHUNK])
        up = _dot(hn_ref[...], wgu_ref[:, d_ff + c * FF_CHUNK:d_ff + (c + 1) * FF_CHUNK])
        return gt, up

    def activate(gt, up):
        return (gt * jax.nn.sigmoid(gt) * up).astype(jnp.bfloat16)

    def down(c, act):
        return _dot(act, wd_ref[c * FF_CHUNK:(c + 1) * FF_CHUNK, :])

    first_part = {FFN_SPLIT: out_proj, n_ff - 2: ffn_norm}
    act = activate(*gate_up(FFN_SPLIT))
    for c in range(FFN_SPLIT, n_ff - 1):
        gu = gate_up(c + 1)
        acc_ref[...] += down(c, act)
        if c in first_part:
            first_part[c]()
        act = activate(*gu)
    wd_last = wd_ref[(n_ff - 1) * FF_CHUNK:n_ff * FF_CHUNK, :]
    halves = [slice(i * (FFN_TILE // 2), (i + 1) * (FFN_TILE // 2)) for i in range(2)]
    for rows in halves:
        acc_ref[rows, :] += _dot(act[rows], wd_last)
    for rows in halves:
        o_ref[0, rows, :] = _rms(acc_ref[rows, :], fw_ref[...])
    act = activate(*gate_up(0))
    per_chunk = -(-len(lru_pieces) // FFN_SPLIT)
    for c in range(FFN_SPLIT):
        gu = gate_up(c + 1) if c + 1 < FFN_SPLIT else None
        if c == 0:
            acc_ref[...] = h_ref[...] + down(c, act)
        else:
            acc_ref[...] += down(c, act)
        for piece in lru_pieces[c * per_chunk:(c + 1) * per_chunk]:
            piece()
        if gu is not None:
            act = activate(*gu)


def _block_diag_groups(w):
    per = LRU_GROUP // (LRU_WIDTH // LRU_BLOCKS)
    bd = LRU_WIDTH // LRU_BLOCKS
    w = w.reshape(LRU_WIDTH // LRU_GROUP, per, bd, bd)
    eye = jnp.eye(per, dtype=w.dtype)
    return jnp.einsum('gade,ab->gadbe', w, eye).reshape(LRU_WIDTH // LRU_GROUP, LRU_GROUP, LRU_GROUP)


def kernel(x, mix_norm_w, w_in, hg_lb, hg_norm_w, conv_w, conv_b, lru_wa, lru_ba, lru_wx, lru_bx,
           lru_a, w_out, ffn_norm_w, w_gate_up, w_down, final_norm_w):
    B, S, D = x.shape
    assert D == D_MODEL and w_in.shape == (1, D_MODEL, D_IN) and S % SEQ_TILE == 0 and SEQ_TILE == 2 * PROJ_ROWS
    bf = jnp.bfloat16
    d_ff = w_down.shape[1]
    assert d_ff % FF_CHUNK == 0

    win = w_in[0].astype(bf)
    wout = w_out[0].astype(bf)
    wg = jnp.concatenate([_block_diag_groups(lru_wa[0]), _block_diag_groups(lru_wx[0])], axis=-1).astype(bf)
    bg = jnp.stack([lru_ba[0], lru_bx[0]], axis=0)
    row = lambda a: a.reshape(1, -1)
    once = pl.Buffered(1)

    full = lambda shape: pl.BlockSpec(shape, lambda b, s: (0,) * len(shape), pipeline_mode=once)
    n_s = S // SEQ_TILE

    def next_block(b, s):
        wrap = s == n_s - 1
        return (jnp.minimum(jnp.where(wrap, b + 1, b), B - 1), jnp.where(wrap, 0, s + 1) * (SEQ_TILE // PROJ_ROWS), 0)

    o_hg, lru_in = pl.pallas_call(
        _hgrn_kernel,
        grid=(B, n_s),
        in_specs=[
            pl.BlockSpec((1, SEQ_TILE, D), lambda b, s: (b, s, 0)),
            pl.BlockSpec((1, PROJ_ROWS, D), next_block),
            full((1, D)), full((D, D_IN)), full((2, HG_WIDTH)), full((1, HG_DIM)),
        ],
        out_specs=[
            pl.BlockSpec((1, SEQ_TILE, HG_WIDTH), lambda b, s: (b, s, 0)),
            pl.BlockSpec((1, SEQ_TILE, 2 * LRU_WIDTH), lambda b, s: (b, s, 0)),
        ],
        out_shape=[
            jax.ShapeDtypeStruct((B, S, HG_WIDTH), bf),
            jax.ShapeDtypeStruct((B, S, 2 * LRU_WIDTH), jnp.float32),
        ],
        scratch_shapes=[
            pltpu.VMEM((PROJ_ROWS, D_IN), jnp.float32),
            pltpu.VMEM((PROJ_ROWS, 4 * HG_WIDTH), jnp.float32),
            pltpu.VMEM((HG_HEADS, HG_DIM, HG_DIM), jnp.float32),
        ],
        compiler_params=pltpu.CompilerParams(
            dimension_semantics=("arbitrary", "arbitrary"), vmem_limit_bytes=VMEM_LIMIT),
        name="hgrn",
    )(x, x, row(mix_norm_w[0]), win, hg_lb, row(hg_norm_w[0]))

    wgu = w_gate_up[0].astype(bf)
    wd = w_down[0].astype(bf)
    tiles_per_seq = S // FFN_TILE
    n_tiles = B * tiles_per_seq

    def tile(n):
        n = jnp.clip(n, 0, n_tiles - 1)
        return (n // tiles_per_seq, n % tiles_per_seq, 0)

    prev = lambda t: tile(t - 1)
    cur = lambda t: tile(t)
    nxt = lambda t: tile(t + 1)

    full1 = lambda shape: pl.BlockSpec(shape, lambda t: (0,) * len(shape), pipeline_mode=once)
    out = pl.pallas_call(
        functools.partial(_lru_ffn_kernel, tiles_per_seq=tiles_per_seq),
        grid=(n_tiles + 1,),
        in_specs=[
            pl.BlockSpec((1, FFN_TILE, D), cur),
            pl.BlockSpec((1, FFN_TILE, HG_WIDTH), cur),
            pl.BlockSpec((1, FFN_TILE, 2 * LRU_WIDTH), nxt),
            full1((1, FFN_TILE, 2 * LRU_WIDTH)),
            full1((CONV_WIDTH, LRU_WIDTH)), full1((1, LRU_WIDTH)),
            full1((LRU_WIDTH // LRU_GROUP, LRU_GROUP, 2 * LRU_GROUP)), full1((2, LRU_WIDTH)), full1((1, LRU_WIDTH)),
            full1((D_MIX, D)), full1((1, D)), full1((D, 2 * d_ff)), full1((d_ff, D)), full1((1, D)),
        ],
        out_specs=pl.BlockSpec((1, FFN_TILE, D), prev),
        out_shape=jax.ShapeDtypeStruct((B, S, D), jnp.float32),
        scratch_shapes=[
            pltpu.VMEM((N_LS, SLAB_ROWS, LANES), jnp.float32),
            pltpu.VMEM((N_LS, SLAB_ROWS, LANES), jnp.float32),
            pltpu.VMEM((FFN_TILE, LRU_WIDTH), jnp.bfloat16),
            pltpu.VMEM((1, LRU_WIDTH), jnp.float32),
            pltpu.VMEM((FFN_TILE, D), jnp.float32),
            pltpu.VMEM((FFN_TILE, D), jnp.float32),
            pltpu.VMEM((FFN_TILE, D), jnp.bfloat16),
        ],
        compiler_params=pltpu.CompilerParams(
            dimension_semantics=("arbitrary",), vmem_limit_bytes=VMEM_LIMIT),
        name="lru_ffn",
    )(x, o_hg, lru_in, lru_in, conv_w[0], row(conv_b[0]), wg, bg, row(lru_a[0]), wout, row(ffn_norm_w[0]), wgu, wd,
      row(final_norm_w))
    return out
```

```python
import functools

import jax
import jax.numpy as jnp
from jax import lax
from jax.experimental import pallas as pl
from jax.experimental.pallas import tpu as pltpu

D_MODEL = 1024
HG_HEADS = 4
HG_DIM = 128
HG_WIDTH = HG_HEADS * HG_DIM
HG_BASE = 32
LRU_WIDTH = 512
LRU_BLOCKS = 8
LRU_GROUP = 256
CONV_WIDTH = 4
LRU_C = 8.0
D_MIX = HG_WIDTH + LRU_WIDTH
D_IN = 4 * HG_WIDTH + 2 * LRU_WIDTH
EPS = 1e-6
LOG2E = 1.4426950408889634

SUB = 128
PROJ_ROWS = 512
SEQ_TILE = 1024
FFN_TILE = 512
FF_CHUNK = 256
COL_CHUNK = 256
LANES = 128
ROW_TILE = 8
N_SEG = ROW_TILE
SEG = SUB // N_SEG
TAIL = ROW_TILE
SEG_PITCH = TAIL + SEG
SLAB_ROWS = SEG_PITCH * (FFN_TILE // SEG)
N_LS = LRU_WIDTH // LANES
PHASE_TRAIL = 2
HGRN_YIELDS = 5
VMEM_LIMIT = 56 * 1024 * 1024

_NT = (((1,), (1,)), ((), ()))


def _dot(a, b, dims=None):
    if dims is None:
        return jnp.dot(a, b, preferred_element_type=jnp.float32)
    return lax.dot_general(a, b, dims, preferred_element_type=jnp.float32)


def _rms(x, w):
    return x * lax.rsqrt(jnp.mean(x * x, axis=-1, keepdims=True) + EPS) * w


def _rows(shape):
    return lax.broadcasted_iota(jnp.int32, shape, 0)


def _cumsum_rows(x):
    n, w = x.shape
    sub = _rows(x.shape) % ROW_TILE
    s = 1
    while s < ROW_TILE:
        shifted = pltpu.roll(x.reshape(n // ROW_TILE, ROW_TILE, w), s, axis=1).reshape(n, w)
        x = x + jnp.where(sub >= s, shifted, 0.0)
        s *= 2
    outs = [x[0:ROW_TILE]]
    for t in range(1, n // ROW_TILE):
        outs.append(x[t * ROW_TILE:(t + 1) * ROW_TILE] + outs[-1][ROW_TILE - 1:ROW_TILE])
    return jnp.concatenate(outs, axis=0)


def _bcast_rows(r, n):
    return jnp.broadcast_to(r, (n, r.shape[1]))


def _hgrn_kernel(x_ref, xnext_ref, nw_ref, win_ref, lb_ref, hgw_ref, ohg_ref, lru_ref,
                 proj0_ref, proj1_ref, state_ref):
    bi = pl.program_id(0)
    si = pl.program_id(1)
    proj_refs = (proj0_ref, proj1_ref)

    @pl.when(si == 0)
    def _():
        state_ref[...] = jnp.zeros_like(state_ref)

    nw = nw_ref[...]
    n_hg = 4 * HG_WIDTH

    lbp = lb_ref[...]
    lbe = jnp.exp(lbp - jnp.max(lbp, axis=0, keepdims=True))
    lb = lbe[0:1] / jnp.sum(lbe, axis=0, keepdims=True)
    hgw = hgw_ref[...]

    row = _rows((SUB, SUB))
    col = lax.broadcasted_iota(jnp.int32, (SUB, SUB), 1)
    mask0 = (row // HG_BASE == col // HG_BASE) & (col <= row)
    mask1 = ((row // (2 * HG_BASE) == col // (2 * HG_BASE))
             & (row % (2 * HG_BASE) >= HG_BASE) & (col % (2 * HG_BASE) < HG_BASE))

    def in_proj(load_x, parity, store_lru):
        proj_ref = proj_refs[parity]
        cell = []

        def norm():
            cell.append(_rms(load_x(), nw).astype(jnp.bfloat16))

        def hg_chunk(c):
            proj_ref[:, c:c + COL_CHUNK] = _dot(cell[0], win_ref[:, c:c + COL_CHUNK])

        def lru_chunk(c):
            store_lru(c - n_hg, _dot(cell[0], win_ref[:, c:c + COL_CHUNK]))

        return ([norm] + [functools.partial(hg_chunk, c) for c in range(0, n_hg, COL_CHUNK)]
                + [functools.partial(lru_chunk, c) for c in range(n_hg, D_IN, COL_CHUNK)])

    def sub_tile(blk, j):
        proj_ref = proj_refs[blk % 2]
        p0 = j * SUB
        r0 = blk * PROJ_ROWS + p0
        half = SUB // 2
        q = proj_ref[p0:p0 + SUB, 0:HG_WIDTH]
        fl = proj_ref[p0:p0 + SUB, HG_WIDTH:2 * HG_WIDTH]
        v = proj_ref[p0:p0 + SUB, 2 * HG_WIDTH:3 * HG_WIDTH].astype(jnp.bfloat16)
        q = q * jax.nn.sigmoid(q)
        f = lb + (1.0 - lb) * jax.nn.sigmoid(fl)
        k = 1.0 - f
        lf = jnp.log(f) * LOG2E
        yield
        beta = _cumsum_rows(lf)
        yield
        r31 = beta[HG_BASE - 1:HG_BASE]
        r63 = beta[2 * HG_BASE - 1:2 * HG_BASE]
        r95 = beta[3 * HG_BASE - 1:3 * HG_BASE]
        r127 = beta[4 * HG_BASE - 1:4 * HG_BASE]
        ref0 = jnp.concatenate([jnp.zeros((HG_BASE, HG_WIDTH), jnp.float32), _bcast_rows(r31, HG_BASE),
                                _bcast_rows(r63, HG_BASE), _bcast_rows(r95, HG_BASE)], axis=0)
        ref1 = jnp.concatenate([_bcast_rows(r31, 2 * HG_BASE), _bcast_rows(r95, 2 * HG_BASE)], axis=0)
        b0 = beta - ref0
        q0 = (q * jnp.exp2(b0)).astype(jnp.bfloat16)
        k0 = (k * jnp.exp2(-b0)).astype(jnp.bfloat16)
        k1 = (k * jnp.exp2(jnp.minimum(ref1 - beta, 0.0))).astype(jnp.bfloat16)
        yield
        q2 = (q[half:] * jnp.exp2(beta[half:] - r63)).astype(jnp.bfloat16)
        k2 = jnp.concatenate([(k[:half] * jnp.exp2(r63 - beta[:half])).astype(jnp.bfloat16),
                              jnp.zeros((half, HG_WIDTH), jnp.bfloat16)], axis=0)
        qs = (q * jnp.exp2(beta)).astype(jnp.bfloat16)
        ks = (k * jnp.exp2(r127 - beta)).astype(jnp.bfloat16)
        dec_end = jnp.exp2(r127)
        yield
        heads = [slice(h * HG_DIM, (h + 1) * HG_DIM) for h in range(HG_HEADS)]
        s01 = [_dot(q0[:, hs], jnp.concatenate([k0[:, hs], k1[:, hs]], axis=0), _NT) for hs in heads]
        s2 = [_dot(q2[:, hs], k2[:, hs], _NT) for hs in heads]
        vt = [v[:, hs].T for hs in heads]
        inc = [_dot(vt[h], ks[:, hs]) for h, hs in enumerate(heads)]
        yield
        g = proj_ref[p0:p0 + SUB, 3 * HG_WIDTH:4 * HG_WIDTH]
        gate = g * jax.nn.sigmoid(g)
        for h, hs in enumerate(heads):
            sc = jnp.where(mask0, s01[h][:, :SUB], 0.0) + jnp.where(mask1, s01[h][:, SUB:], 0.0)
            sc = jnp.concatenate([sc[:half], sc[half:] + s2[h]], axis=0)
            st = state_ref[h]
            o = _dot(jnp.concatenate([sc.astype(jnp.bfloat16), qs[:, hs]], axis=1),
                     jnp.concatenate([vt[h], st.astype(jnp.bfloat16)], axis=1), _NT)
            state_ref[h] = st * dec_end[:, hs] + inc[h]
            o = o * lax.rsqrt(jnp.mean(o * o, axis=-1, keepdims=True) + EPS) * hgw
            ohg_ref[0, r0:r0 + SUB, hs] = (o * gate[:, hs]).astype(jnp.bfloat16)

    def lru_to_scratch(off, val):
        proj0_ref[:, n_hg + off:n_hg + off + COL_CHUNK] = val

    def lru_to_output(off, val):
        lru_ref[0, PROJ_ROWS:2 * PROJ_ROWS, off:off + COL_CHUNK] = val

    def flush_first_block():
        lru_ref[0, 0:PROJ_ROWS, :] = proj0_ref[:, n_hg:D_IN]

    @pl.when((bi == 0) & (si == 0))
    def _():
        for step in in_proj(lambda: x_ref[0, 0:PROJ_ROWS, :], 0, lru_to_scratch):
            step()

    norm_next, *proj_next = in_proj(lambda: xnext_ref[0], 0, lru_to_scratch)
    norm_second, *proj_second = in_proj(lambda: x_ref[0, PROJ_ROWS:2 * PROJ_ROWS, :], 1, lru_to_output)
    phases = (
        ([norm_second] + proj_second + [flush_first_block, norm_next], 0),
        (proj_next, PHASE_TRAIL),
    )
    for blk, (fill, trail) in enumerate(phases):
        spread = fill[:len(fill) - trail]
        points = [sub_tile(blk, j) for j in range(PROJ_ROWS // SUB)]
        n_points = len(points) * HGRN_YIELDS
        seen = done = 0
        for gen in points:
            for _ in gen:
                seen += 1
                upto = -(-seen * len(spread) // n_points)
                for step in spread[done:upto]:
                    step()
                done = upto
        assert seen == n_points and done == len(spread)
        for step in fill[len(fill) - trail:]:
            step()


def _lru_ffn_kernel(x_ref, ohg_ref, lru_ref, cw_ref, cb_ref, wg_ref, bg_ref, ap_ref, wout_ref, nw_ref, wgu_ref,
                    wd_ref, fw_ref, o_ref, xl_ref, hs_ref, mlru_ref, hcar_ref, acc_ref, h_ref, hn_ref, *,
                    tiles_per_seq):
    t = pl.program_id(0)
    seq_start = (t % tiles_per_seq) == 0
    d_ff = wd_ref.shape[0]
    n_ff = d_ff // FF_CHUNK
    last_rows = slice(SLAB_ROWS - TAIL, SLAB_ROWS)

    @pl.when(t == 0)
    def _():
        h_ref[...] = jnp.zeros_like(h_ref)
        hn_ref[...] = jnp.zeros_like(hn_ref)
        hcar_ref[...] = jnp.zeros_like(hcar_ref)
        xl_ref[:, last_rows, :] = jnp.zeros((N_LS, TAIL, LANES), jnp.float32)

    neg_c_sp2 = (-LRU_C * LOG2E) * jax.nn.softplus(-ap_ref[...])
    cw_rows = [jnp.broadcast_to(cw_ref[k:k + 1, :], (N_SEG, LRU_WIDTH)) for k in range(CONV_WIDTH)]
    cb_rows = jnp.broadcast_to(cb_ref[...], (N_SEG, LRU_WIDTH))
    bg = bg_ref[...]

    def store_slabs():
        val = lru_ref[0, :, 0:LRU_WIDTH]
        n_seg = FFN_TILE // SEG
        for ls in range(N_LS):
            lanes = slice(ls * LANES, (ls + 1) * LANES)
            xl_ref[ls, 0:TAIL, :] = jnp.where(seq_start, 0.0, xl_ref[ls, last_rows, :])
            for s in range(n_seg):
                xl_ref[ls, SEG_PITCH * s + TAIL:SEG_PITCH * (s + 1), :] = val[SEG * s:SEG * (s + 1), lanes]
                if s + 1 < n_seg:
                    xl_ref[ls, SEG_PITCH * (s + 1):SEG_PITCH * (s + 1) + TAIL, :] = (
                        val[SEG * (s + 1) - TAIL:SEG * (s + 1), lanes])

    conv_out = {}

    def conv(j):
        slab_start = SEG_PITCH * j * N_SEG + TAIL

        def x_slab(r):
            return jnp.concatenate([xl_ref[ls, pl.ds(slab_start + r, N_SEG, stride=SEG_PITCH), :]
                                    for ls in range(N_LS)], axis=1)

        xs = {r: x_slab(r) for r in range(1 - CONV_WIDTH, SEG)}
        xc_slabs = []
        for r in range(SEG):
            acc = cb_rows
            for tap in range(CONV_WIDTH):
                acc = acc + xs[r - (CONV_WIDTH - 1 - tap)] * cw_rows[tap]
            xc_slabs.append(acc)
        conv_out[j] = jnp.concatenate(xc_slabs, axis=0)

    def recur(j):
        xc = conv_out[j]
        xcb = xc.astype(jnp.bfloat16)
        gates = [_dot(xcb[:, gi * LRU_GROUP:(gi + 1) * LRU_GROUP], wg_ref[gi]) for gi in range(LRU_WIDTH // LRU_GROUP)]
        rr = jnp.concatenate([gt[:, :LRU_GROUP] for gt in gates], axis=1)
        ii = jnp.concatenate([gt[:, LRU_GROUP:] for gt in gates], axis=1)
        rr = jax.nn.sigmoid(rr + bg[0:1])
        ii = jax.nn.sigmoid(ii + bg[1:2])
        a = jnp.exp2(rr * neg_c_sp2)
        b_in = jnp.sqrt(1.0 - a * a) * (ii * xc)
        h_loc, a_loc = [b_in[0:N_SEG]], [a[0:N_SEG]]
        for r in range(1, SEG):
            rows = slice(r * N_SEG, (r + 1) * N_SEG)
            h_loc.append(a[rows] * h_loc[-1] + b_in[rows])
            a_loc.append(a[rows] * a_loc[-1])
        h_in = hcar_ref[...]
        if j == 0:
            h_in = jnp.where(seq_start, 0.0, h_in)
        carries = [h_in]
        for s in range(N_SEG):
            carries.append(a_loc[-1][s:s + 1] * carries[-1] + h_loc[-1][s:s + 1])
        hcar_ref[...] = carries[-1]
        carry = jnp.concatenate(carries[:-1], axis=0)
        hs0 = j * N_SEG * SEG_PITCH
        for r in range(SEG):
            h_r = h_loc[r] + a_loc[r] * carry
            for ls in range(N_LS):
                hs_ref[ls, pl.ds(hs0 + TAIL + r, N_SEG, stride=SEG_PITCH), :] = h_r[:, ls * LANES:(ls + 1) * LANES]
        hh = jnp.concatenate(
            [jnp.concatenate([hs_ref[ls, hs0 + SEG_PITCH * s + TAIL:hs0 + SEG_PITCH * (s + 1), :]
                              for s in range(N_SEG)], axis=0) for ls in range(N_LS)], axis=1)
        gl = lru_ref[0, j * SUB:(j + 1) * SUB, LRU_WIDTH:]
        mlru_ref[j * SUB:(j + 1) * SUB, :] = (hh * jax.nn.gelu(gl, approximate=True)).astype(jnp.bfloat16)

    def first_piece():
        store_slabs()
        conv(0)

    pieces = [first_piece, functools.partial(recur, 0)]
    for j in range(1, FFN_TILE // SUB):
        pieces += [functools.partial(conv, j), functools.partial(recur, j)]

    def out_proj():
        mixed = jnp.concatenate([ohg_ref[0], mlru_ref[...]], axis=1)
        h_ref[...] = x_ref[0] + _dot(mixed, wout_ref[...])

    def ffn_norm():
        hn_ref[...] = _rms(h_ref[...], nw_ref[...]).astype(jnp.bfloat16)

    pieces += [out_proj, ffn_norm]
    assert len(pieces) == n_ff - 1

    def gate_up(c):
        gt = _dot(hn_ref[...], wgu_ref[:, c * FF_CHUNK:(c + 1) * FF_CHUNK])
        up = _dot(hn_ref[...], wgu_ref[:, d_ff + c * FF_CHUNK:d_ff + (c + 1) * FF_CHUNK])
        return gt, up

    def activate(gt, up):
        return (gt * jax.nn.sigmoid(gt) * up).astype(jnp.bfloat16)

    act = activate(*gate_up(0))
    for c in range(n_ff - 1):
        gu = gate_up(c + 1)
        down = _dot(act, wd_ref[c * FF_CHUNK:(c + 1) * FF_CHUNK, :])
        if c == 0:
            acc_ref[...] = h_ref[...] + down
        else:
            acc_ref[...] += down
        pieces[c]()
        act = activate(*gu)
    wd_last = wd_ref[(n_ff - 1) * FF_CHUNK:n_ff * FF_CHUNK, :]
    halves = [slice(i * (FFN_TILE // 2), (i + 1) * (FFN_TILE // 2)) for i in range(2)]
    for rows in halves:
        acc_ref[rows, :] += _dot(act[rows], wd_last)
    for rows in halves:
        o_ref[0, rows, :] = _rms(acc_ref[rows, :], fw_ref[...])


def _block_diag_groups(w):
    per = LRU_GROUP // (LRU_WIDTH // LRU_BLOCKS)
    bd = LRU_WIDTH // LRU_BLOCKS
    w = w.reshape(LRU_WIDTH // LRU_GROUP, per, bd, bd)
    eye = jnp.eye(per, dtype=w.dtype)
    return jnp.einsum('gade,ab->gadbe', w, eye).reshape(LRU_WIDTH // LRU_GROUP, LRU_GROUP, LRU_GROUP)


def kernel(x, mix_norm_w, w_in, hg_lb, hg_norm_w, conv_w, conv_b, lru_wa, lru_ba, lru_wx, lru_bx,
           lru_a, w_out, ffn_norm_w, w_gate_up, w_down, final_norm_w):
    B, S, D = x.shape
    assert D == D_MODEL and w_in.shape == (1, D_MODEL, D_IN) and S % SEQ_TILE == 0 and SEQ_TILE == 2 * PROJ_ROWS
    bf = jnp.bfloat16
    d_ff = w_down.shape[1]
    assert d_ff % FF_CHUNK == 0

    win = w_in[0].astype(bf)
    wout = w_out[0].astype(bf)
    wg = jnp.concatenate([_block_diag_groups(lru_wa[0]), _block_diag_groups(lru_wx[0])], axis=-1).astype(bf)
    bg = jnp.stack([lru_ba[0], lru_bx[0]], axis=0)
    row = lambda a: a.reshape(1, -1)
    once = pl.Buffered(1)

    full = lambda shape: pl.BlockSpec(shape, lambda b, s: (0,) * len(shape), pipeline_mode=once)
    n_s = S // SEQ_TILE

    def next_block(b, s):
        wrap = s == n_s - 1
        return (jnp.minimum(jnp.where(wrap, b + 1, b), B - 1), jnp.where(wrap, 0, s + 1) * (SEQ_TILE // PROJ_ROWS), 0)

    o_hg, lru_in = pl.pallas_call(
        _hgrn_kernel,
        grid=(B, n_s),
        in_specs=[
            pl.BlockSpec((1, SEQ_TILE, D), lambda b, s: (b, s, 0)),
            pl.BlockSpec((1, PROJ_ROWS, D), next_block),
            full((1, D)), full((D, D_IN)), full((2, HG_WIDTH)), full((1, HG_DIM)),
        ],
        out_specs=[
            pl.BlockSpec((1, SEQ_TILE, HG_WIDTH), lambda b, s: (b, s, 0)),
            pl.BlockSpec((1, SEQ_TILE, 2 * LRU_WIDTH), lambda b, s: (b, s, 0)),
        ],
        out_shape=[
            jax.ShapeDtypeStruct((B, S, HG_WIDTH), bf),
            jax.ShapeDtypeStruct((B, S, 2 * LRU_WIDTH), jnp.float32),
        ],
        scratch_shapes=[
            pltpu.VMEM((PROJ_ROWS, D_IN), jnp.float32),
            pltpu.VMEM((PROJ_ROWS, 4 * HG_WIDTH), jnp.float32),
            pltpu.VMEM((HG_HEADS, HG_DIM, HG_DIM), jnp.float32),
        ],
        compiler_params=pltpu.CompilerParams(
            dimension_semantics=("arbitrary", "arbitrary"), vmem_limit_bytes=VMEM_LIMIT),
        name="hgrn",
    )(x, x, row(mix_norm_w[0]), win, hg_lb, row(hg_norm_w[0]))

    wgu = w_gate_up[0].astype(bf)
    wd = w_down[0].astype(bf)
    tiles_per_seq = S // FFN_TILE
    n_tiles = B * tiles_per_seq

    def cur(t):
        c = jnp.maximum(t - 1, 0)
        return (c // tiles_per_seq, c % tiles_per_seq, 0)

    def nxt(t):
        n = jnp.minimum(t, n_tiles - 1)
        return (n // tiles_per_seq, n % tiles_per_seq, 0)

    full1 = lambda shape: pl.BlockSpec(shape, lambda t: (0,) * len(shape), pipeline_mode=once)
    out = pl.pallas_call(
        functools.partial(_lru_ffn_kernel, tiles_per_seq=tiles_per_seq),
        grid=(n_tiles + 1,),
        in_specs=[
            pl.BlockSpec((1, FFN_TILE, D), nxt),
            pl.BlockSpec((1, FFN_TILE, HG_WIDTH), nxt),
            pl.BlockSpec((1, FFN_TILE, 2 * LRU_WIDTH), nxt),
            full1((CONV_WIDTH, LRU_WIDTH)), full1((1, LRU_WIDTH)),
            full1((LRU_WIDTH // LRU_GROUP, LRU_GROUP, 2 * LRU_GROUP)), full1((2, LRU_WIDTH)), full1((1, LRU_WIDTH)),
            full1((D_MIX, D)), full1((1, D)), full1((D, 2 * d_ff)), full1((d_ff, D)), full1((1, D)),
        ],
        out_specs=pl.BlockSpec((1, FFN_TILE, D), cur),
        out_shape=jax.ShapeDtypeStruct((B, S, D), jnp.float32),
        scratch_shapes=[
            pltpu.VMEM((N_LS, SLAB_ROWS, LANES), jnp.float32),
            pltpu.VMEM((N_LS, SLAB_ROWS, LANES), jnp.float32),
            pltpu.VMEM((FFN_TILE, LRU_WIDTH), jnp.bfloat16),
            pltpu.VMEM((1, LRU_WIDTH), jnp.float32),
            pltpu.VMEM((FFN_TILE, D), jnp.float32),
            pltpu.VMEM((FFN_TILE, D), jnp.float32),
            pltpu.VMEM((FFN_TILE, D), jnp.bfloat16),
        ],
        compiler_params=pltpu.CompilerParams(
            dimension_semantics=("arbitrary",), vmem_limit_bytes=VMEM_LIMIT),
        name="lru_ffn",
    )(x, o_hg, lru_in, conv_w[0], row(conv_b[0]), wg, bg, row(lru_a[0]), wout, row(ffn_norm_w[0]), wgu, wd,
      row(final_norm_w))
    return out
```
